```python
import math
import jax, jax.numpy as jnp
from jax import lax
import numpy as np

D_MODEL = 2048
BATCH = 8
SEQ = 2048
DEPTH = 2

HEAD_DIM = 128
N_MEM = 256
MEM_HEADS = 4
MEM_WIDTH = MEM_HEADS * HEAD_DIM
MIX_WIDTH = D_MODEL - MEM_WIDTH
MLSTM_HEADS = 6
MLSTM_DV = MIX_WIDTH // MLSTM_HEADS
MLSTM_DQK = MLSTM_DV // 2
MLSTM_QK_WIDTH = MLSTM_HEADS * MLSTM_DQK
MLSTM_CHUNK = 64
CONV_WIDTH = 4
MOBA_HEADS = MIX_WIDTH // HEAD_DIM
MOBA_BLOCK = 256
MOBA_TOPK = 3
MOBA_QCHUNK = 8
ROPE_THETA = 500000.0
ROPE_DIM = HEAD_DIM // 4
N_EXPERTS = 32
TOP_K = 4
D_FF = D_MODEL
SWIGLU_LIMIT = 7.0
SWIGLU_ALPHA = 1.702
MOE_ROW_BLOCK = 256
LN_EPS = 1e-5
DEEPNORM_ALPHA = (2 * DEPTH) ** 0.25
DEEPNORM_BETA = (8 * DEPTH) ** -0.25
MLSTM_IN = 2 * MLSTM_QK_WIDTH + 2 * MIX_WIDTH + 2 * MLSTM_HEADS + MEM_WIDTH
MOBA_IN = 3 * MIX_WIDTH + MEM_WIDTH

kernel_name = "hybrid_mlstm_moba_memxattn_moe_deepnorm"


def layer_norm(x, g, b):
    xf = x.astype(jnp.float32)
    mu = xf.mean(-1, keepdims=True)
    var = jnp.square(xf - mu).mean(-1, keepdims=True)
    y = (xf - mu) * lax.rsqrt(var + LN_EPS)
    return (y * g.astype(jnp.float32) + b.astype(jnp.float32)).astype(x.dtype)


def partial_rotary(t, pos):
    half = ROPE_DIM // 2
    inv_freq = ROPE_THETA ** (-jnp.arange(half, dtype=jnp.float32) * 2.0 / ROPE_DIM)
    ang = pos.astype(jnp.float32)[:, None] * inv_freq[None, :]
    cos = jnp.cos(ang)[None, :, None, :]
    sin = jnp.sin(ang)[None, :, None, :]
    tf = t.astype(jnp.float32)
    t1, t2, rest = tf[..., :half], tf[..., half:ROPE_DIM], tf[..., ROPE_DIM:]
    out = jnp.concatenate([t1 * cos - t2 * sin, t2 * cos + t1 * sin, rest], axis=-1)
    return out.astype(t.dtype)


def causal_depthwise_conv(u, w, b):
    c = u.shape[-1]
    y = lax.conv_general_dilated(u, w[:, None, :].astype(u.dtype), window_strides=(1,),
                                 padding=[(CONV_WIDTH - 1, 0)],
                                 dimension_numbers=('NWC', 'WIO', 'NWC'),
                                 feature_group_count=c)
    return y + b.astype(u.dtype)


def memory_cross_attention(qm, mem, w_mem_kv):
    bsz, s, _ = qm.shape
    kv = mem @ w_mem_kv
    k, v = jnp.split(kv, 2, axis=-1)
    q = qm.reshape(bsz, s, MEM_HEADS, HEAD_DIM)
    k = k.reshape(bsz, N_MEM, MEM_HEADS, HEAD_DIM)
    v = v.reshape(bsz, N_MEM, MEM_HEADS, HEAD_DIM)
    sc = jnp.einsum('bshd,bmhd->bhsm', q, k).astype(jnp.float32) * (HEAD_DIM ** -0.5)
    p = jax.nn.softmax(sc, axis=-1).astype(v.dtype)
    o = jnp.einsum('bhsm,bmhd->bshd', p, v)
    return o.reshape(bsz, s, MEM_WIDTH)


def mlstm_chunkwise(q, k, v, ig, fg):
    f32 = jnp.float32
    bsz, nh, s, dqk = q.shape
    dv = v.shape[-1]
    L = MLSTM_CHUNK
    nc = s // L
    qc = q.astype(f32).reshape(bsz, nh, nc, L, dqk)
    kc = (k.astype(f32) * (dqk ** -0.5)).reshape(bsz, nh, nc, L, dqk)
    vc = v.astype(f32).reshape(bsz, nh, nc, L, dv)
    log_f = jax.nn.log_sigmoid(fg).reshape(bsz, nh, nc, L)
    log_i = ig.reshape(bsz, nh, nc, L)
    bcum = jnp.cumsum(log_f, axis=-1)
    g = bcum[..., -1]
    a = g[..., None] - bcum + log_i

    def step(carry, xs):
        c_st, n_st, m_st = carry
        a_c, g_c, k_c, v_c = xs
        m_new = jnp.maximum(g_c + m_st, a_c.max(-1))
        decay = jnp.exp(g_c + m_st - m_new)
        w = jnp.exp(a_c - m_new[..., None])
        c_new = decay[..., None, None] * c_st + jnp.einsum('bhl,bhld,bhle->bhde', w, k_c, v_c)
        n_new = decay[..., None] * n_st + jnp.einsum('bhl,bhld->bhd', w, k_c)
        return (c_new, n_new, m_new), (c_st, n_st, m_st)

    init = (jnp.zeros((bsz, nh, dqk, dv), f32), jnp.zeros((bsz, nh, dqk), f32),
            jnp.zeros((bsz, nh), f32))
    xs = (jnp.moveaxis(a, 2, 0), jnp.moveaxis(g, 2, 0), jnp.moveaxis(kc, 2, 0), jnp.moveaxis(vc, 2, 0))
    _, (c_prev, n_prev, m_prev) = lax.scan(step, init, xs)
    c_prev = jnp.moveaxis(c_prev, 0, 2)
    n_prev = jnp.moveaxis(n_prev, 0, 2)
    m_prev = jnp.moveaxis(m_prev, 0, 2)

    inter_log = bcum + m_prev[..., None]
    causal = jnp.tril(jnp.ones((L, L), dtype=bool))
    dmat = bcum[..., :, None] - bcum[..., None, :] + log_i[..., None, :]
    dmat = jnp.where(causal, dmat, -jnp.inf)
    m_t = jnp.maximum(inter_log, dmat.max(-1))
    s_qk = jnp.einsum('bhcld,bhcjd->bhclj', qc, kc) * jnp.exp(dmat - m_t[..., None])
    inter_w = jnp.exp(inter_log - m_t)
    num = (inter_w[..., None] * jnp.einsum('bhcld,bhcde->bhcle', qc, c_prev)
           + jnp.einsum('bhclj,bhcje->bhcle', s_qk, vc))
    den = inter_w * jnp.einsum('bhcld,bhcd->bhcl', qc, n_prev) + s_qk.sum(-1)
    h = num / jnp.maximum(jnp.abs(den), jnp.exp(-m_t))[..., None]
    return h.reshape(bsz, nh, s, dv)


def mlstm_mixer(z, conv_w, conv_b, b_igate, b_fgate):
    bsz, s, _ = z.shape
    f32 = jnp.float32
    qk_raw, v, o_pre, gates, qm = jnp.split(
        z, [2 * MLSTM_QK_WIDTH, 2 * MLSTM_QK_WIDTH + MIX_WIDTH,
            2 * MLSTM_QK_WIDTH + 2 * MIX_WIDTH, 2 * MLSTM_QK_WIDTH + 2 * MIX_WIDTH + 2 * MLSTM_HEADS], axis=-1)
    qk = jax.nn.silu(causal_depthwise_conv(qk_raw, conv_w, conv_b))
    q, k = jnp.split(qk, 2, axis=-1)

    def heads(t, d):
        return t.reshape(bsz, s, MLSTM_HEADS, d).transpose(0, 2, 1, 3)

    ig = (gates[..., :MLSTM_HEADS].astype(f32) + b_igate.astype(f32)).transpose(0, 2, 1)
    fg = (gates[..., MLSTM_HEADS:].astype(f32) + b_fgate.astype(f32)).transpose(0, 2, 1)
    h = mlstm_chunkwise(heads(q, MLSTM_DQK), heads(k, MLSTM_DQK), heads(v, MLSTM_DV), ig, fg)
    h = h.transpose(0, 2, 1, 3).reshape(bsz, s, MIX_WIDTH).astype(z.dtype)
    return jax.nn.sigmoid(o_pre) * h, qm


def moba_attention(q, k, v):
    f32 = jnp.float32
    bsz, s, nh, hd = q.shape
    nb = -(-s // MOBA_BLOCK)
    sp = nb * MOBA_BLOCK
    topk = min(MOBA_TOPK, nb)
    scale = hd ** -0.5
    qh = q.transpose(0, 2, 1, 3)
    pad = ((0, 0), (0, 0), (0, sp - s), (0, 0))
    kb = jnp.pad(k.transpose(0, 2, 1, 3), pad).reshape(bsz, nh, nb, MOBA_BLOCK, hd)
    vb = jnp.pad(v.transpose(0, 2, 1, 3), pad).reshape(bsz, nh, nb, MOBA_BLOCK, hd)

    kbar = kb.astype(f32).mean(axis=3)
    gate = jnp.einsum('bhsd,bhnd->bhsn', qh.astype(f32), kbar)
    q_blk = jnp.arange(s) // MOBA_BLOCK
    fully_past = jnp.arange(nb)[None, :] < q_blk[:, None]
    gate = jnp.where(fully_past, gate, -jnp.inf)
    gval, gidx = lax.top_k(gate, topk)
    gvalid = jnp.isfinite(gval)

    qc_n = MOBA_QCHUNK
    nq = s // qc_n
    q_chunks = qh.reshape(bsz, nh, nq, qc_n, hd).transpose(2, 0, 1, 3, 4)
    idx_chunks = gidx.reshape(bsz, nh, nq, qc_n, topk).transpose(2, 0, 1, 3, 4)
    val_chunks = gvalid.reshape(bsz, nh, nq, qc_n, topk).transpose(2, 0, 1, 3, 4)
    bi = jnp.arange(bsz)[:, None, None, None]
    hi = jnp.arange(nh)[None, :, None, None]
    kpos = jnp.arange(MOBA_BLOCK)

    def chunk(args):
        ci, qc, idx, valid = args
        k_sel = kb[bi, hi, idx]
        v_sel = vb[bi, hi, idx]
        s_sel = jnp.einsum('bhqd,bhqnkd->bhqnk', qc, k_sel).astype(f32) * scale
        s_sel = jnp.where(valid[..., None], s_sel, -jnp.inf).reshape(bsz, nh, qc_n, topk * MOBA_BLOCK)
        start = ci * qc_n
        own = start // MOBA_BLOCK
        k_own = lax.dynamic_index_in_dim(kb, own, axis=2, keepdims=False)
        v_own = lax.dynamic_index_in_dim(vb, own, axis=2, keepdims=False)
        qpos = start + jnp.arange(qc_n) - own * MOBA_BLOCK
        s_own = jnp.einsum('bhqd,bhkd->bhqk', qc, k_own).astype(f32) * scale
        s_own = jnp.where(kpos[None, :] <= qpos[:, None], s_own, -jnp.inf)
        p = jax.nn.softmax(jnp.concatenate([s_sel, s_own], axis=-1), axis=-1).astype(v.dtype)
        p_sel = p[..., :topk * MOBA_BLOCK].reshape(bsz, nh, qc_n, topk, MOBA_BLOCK)
        p_own = p[..., topk * MOBA_BLOCK:]
        return (jnp.einsum('bhqnk,bhqnkd->bhqd', p_sel, v_sel)
                + jnp.einsum('bhqk,bhkd->bhqd', p_own, v_own))

    out = lax.map(chunk, (jnp.arange(nq), q_chunks, idx_chunks, val_chunks))
    return out.transpose(1, 0, 3, 2, 4).reshape(bsz, s, nh * hd)


def moe_ffn(x, w_router, b_router, w_gu, b_gu, w_down, b_down):
    bsz, s, d = x.shape
    f32 = jnp.float32
    xf = x.reshape(-1, d)
    n = xf.shape[0]
    logits = (xf @ w_router + b_router).astype(f32)
    top_logit, top_e = lax.top_k(logits, TOP_K)
    gate = jax.nn.softmax(top_logit, axis=-1)
    nk = n * TOP_K
    e_flat = top_e.reshape(-1)
    tok_flat = jnp.repeat(jnp.arange(n, dtype=jnp.int32), TOP_K)
    g_flat = gate.reshape(-1)
    order = jnp.argsort(e_flat)
    e_sorted = e_flat[order]
    counts = jnp.bincount(e_flat, length=N_EXPERTS)
    padded = (counts + MOE_ROW_BLOCK - 1) // MOE_ROW_BLOCK * MOE_ROW_BLOCK
    start = jnp.cumsum(counts) - counts
    pend = jnp.cumsum(padded)
    pstart = pend - padded
    dest = pstart[e_sorted] + jnp.arange(nk) - start[e_sorted]
    nblk = -(-nk // MOE_ROW_BLOCK) + N_EXPERTS
    p_rows = nblk * MOE_ROW_BLOCK
    buf_tok = jnp.zeros((p_rows,), jnp.int32).at[dest].set(tok_flat[order])
    buf_gate = jnp.zeros((p_rows,), f32).at[dest].set(g_flat[order])
    blk_e = jnp.minimum(jnp.searchsorted(pend, jnp.arange(nblk) * MOE_ROW_BLOCK, side='right'),
                        N_EXPERTS - 1)
    xb = xf[buf_tok].reshape(nblk, MOE_ROW_BLOCK, d)

    def expert_block(args):
        e, xr = args
        hgu = xr @ w_gu[e] + b_gu[e]
        hg, hu = jnp.split(hgu, 2, axis=-1)
        hg = jnp.minimum(hg, SWIGLU_LIMIT)
        hu = jnp.clip(hu, -SWIGLU_LIMIT, SWIGLU_LIMIT)
        hid = (hu + 1.0) * (hg * jax.nn.sigmoid(SWIGLU_ALPHA * hg))
        return hid @ w_down[e] + b_down[e]

    yb = lax.map(expert_block, (blk_e, xb)).reshape(p_rows, d)
    out = jnp.zeros((n, d), x.dtype).at[buf_tok].add(yb * buf_gate[:, None].astype(yb.dtype))
    return out.reshape(bsz, s, d)


def setup_inputs(seed: int = 0) -> dict:
    key = jax.random.key(seed)
    ks = iter(jax.random.split(key, 32))
    f32 = jnp.float32
    n_ml = (DEPTH + 1) // 2
    n_mb = DEPTH // 2

    def nrm(shape, scale):
        return jax.random.normal(next(ks), shape, f32) * scale

    return {
        "x": nrm((BATCH, SEQ, D_MODEL), 1.0),
        "mem": nrm((BATCH, N_MEM, D_MODEL), 1.0),
        "mlstm_w_in": nrm((n_ml, D_MODEL, MLSTM_IN), D_MODEL ** -0.5),
        "mlstm_conv_w": nrm((n_ml, CONV_WIDTH, 2 * MLSTM_QK_WIDTH), CONV_WIDTH ** -0.5),
        "mlstm_conv_b": nrm((n_ml, 2 * MLSTM_QK_WIDTH), 0.01),
        "mlstm_b_igate": nrm((n_ml, MLSTM_HEADS), 0.1),
        "mlstm_b_fgate": 3.0 + 3.0 * jax.random.uniform(next(ks), (n_ml, MLSTM_HEADS), f32),
        "moba_w_in": nrm((n_mb, D_MODEL, MOBA_IN), D_MODEL ** -0.5),
        "w_mem_kv": nrm((DEPTH, D_MODEL, 2 * MEM_WIDTH), D_MODEL ** -0.5),
        "w_out": nrm((DEPTH, MIX_WIDTH + MEM_WIDTH, D_MODEL), DEEPNORM_BETA * D_MODEL ** -0.5),
        "ln1_g": 1.0 + nrm((DEPTH, D_MODEL), 0.02),
        "ln1_b": nrm((DEPTH, D_MODEL), 0.02),
        "w_router": nrm((DEPTH, D_MODEL, N_EXPERTS), D_MODEL ** -0.5),
        "b_router": nrm((DEPTH, N_EXPERTS), 0.01),
        "w_gu": nrm((DEPTH, N_EXPERTS, D_MODEL, 2 * D_FF), D_MODEL ** -0.5),
        "b_gu": nrm((DEPTH, N_EXPERTS, 2 * D_FF), 0.01),
        "w_down": nrm((DEPTH, N_EXPERTS, D_FF, D_MODEL), DEEPNORM_BETA * D_FF ** -0.5),
        "b_down": nrm((DEPTH, N_EXPERTS, D_MODEL), 0.01),
        "ln2_g": 1.0 + nrm((DEPTH, D_MODEL), 0.02),
        "ln2_b": nrm((DEPTH, D_MODEL), 0.02),
    }


def reference(x, mem, mlstm_w_in, mlstm_conv_w, mlstm_conv_b, mlstm_b_igate, mlstm_b_fgate,
              moba_w_in, w_mem_kv, w_out, ln1_g, ln1_b, w_router, b_router, w_gu, b_gu,
              w_down, b_down, ln2_g, ln2_b):
    bsz, s, _ = x.shape
    pos = jnp.arange(s)
    for i in range(DEPTH):
        j = i // 2
        if i % 2 == 0:
            z = x @ mlstm_w_in[j]
            h_mix, qm = mlstm_mixer(z, mlstm_conv_w[j], mlstm_conv_b[j], mlstm_b_igate[j], mlstm_b_fgate[j])
        else:
            z = x @ moba_w_in[j]
            q, k, v, qm = jnp.split(z, [MIX_WIDTH, 2 * MIX_WIDTH, 3 * MIX_WIDTH], axis=-1)
            q = partial_rotary(q.reshape(bsz, s, MOBA_HEADS, HEAD_DIM), pos)
            k = partial_rotary(k.reshape(bsz, s, MOBA_HEADS, HEAD_DIM), pos)
            v = v.reshape(bsz, s, MOBA_HEADS, HEAD_DIM)
            h_mix = moba_attention(q, k, v)
        h_mem = memory_cross_attention(qm, mem, w_mem_kv[i])
        y = jnp.concatenate([h_mix, h_mem], axis=-1) @ w_out[i]
        x = layer_norm(DEEPNORM_ALPHA * x + y, ln1_g[i], ln1_b[i])
        f = moe_ffn(x, w_router[i], b_router[i], w_gu[i], b_gu[i], w_down[i], b_down[i])
        x = layer_norm(DEEPNORM_ALPHA * x + f, ln2_g[i], ln2_b[i])
    return x
```

```python
import functools

import jax
import jax.numpy as jnp
from jax import lax
from jax.experimental import pallas as pl
from jax.experimental.pallas import tpu as pltpu

F32 = jnp.float32
BF16 = jnp.bfloat16

HEAD_DIM = 128
MEM_HEADS = 4
MEM_WIDTH = MEM_HEADS * HEAD_DIM
MLSTM_HEADS = 6
CONV_WIDTH = 4
MOBA_BLOCK = 256
MOBA_TOPK = 3
ROPE_THETA = 500000.0
ROPE_DIM = HEAD_DIM // 4
N_EXPERTS = 32
TOP_K = 4
SWIGLU_LIMIT = 7.0
SWIGLU_ALPHA = 1.702
LN_EPS = 1e-5
DEPTH = 2
DEEPNORM_ALPHA = (2 * DEPTH) ** 0.25

LANES = 128
SUBLANES = 8
VMEM_LIMIT_BYTES = 56 * 1024 * 1024
MLSTM_CHUNK = 128
MOE_ROW_BLOCK = 256
NEG_BIG = -1e30


def _cparams(sem):
    return pltpu.CompilerParams(dimension_semantics=sem, vmem_limit_bytes=VMEM_LIMIT_BYTES)


def _dot(a, b):
    return jnp.dot(a, b, preferred_element_type=F32)


def _dot_nt(a, b):
    return lax.dot_general(a, b, (((1,), (1,)), ((), ())), preferred_element_type=F32)


def _dot_tn(a, b):
    return lax.dot_general(a, b, (((0,), (0,)), ((), ())), preferred_element_type=F32)


def _split_bf16(x):
    hi = x.astype(BF16)
    lo = (x - hi.astype(F32)).astype(BF16)
    return hi, lo


def _layer_norm_rows(r, g, b):
    mu = jnp.mean(r, axis=-1, keepdims=True)
    d = r - mu
    var = jnp.mean(d * d, axis=-1, keepdims=True)
    return d * lax.rsqrt(var + LN_EPS) * g + b


def _matmul_kernel(a_ref, b_ref, o_ref, a_bf_ref):
    @pl.when(pl.program_id(1) == 0)
    def _():
        a_bf_ref[...] = a_ref[...].astype(BF16)

    o_ref[...] = _dot(a_bf_ref[...], b_ref[...]).astype(o_ref.dtype)


def _matmul(a, b, out_dtype, tm, tn):
    m, k = a.shape
    n = b.shape[1]
    return pl.pallas_call(
        _matmul_kernel,
        grid=(m // tm, n // tn),
        in_specs=[pl.BlockSpec((tm, k), lambda i, j: (i, 0)),
                  pl.BlockSpec((k, tn), lambda i, j: (0, j))],
        out_specs=pl.BlockSpec((tm, tn), lambda i, j: (i, j)),
        out_shape=jax.ShapeDtypeStruct((m, n), out_dtype),
        scratch_shapes=[pltpu.VMEM((tm, k), BF16)],
        compiler_params=_cparams(("parallel", "arbitrary")),
        name="matmul",
    )(a, b)


def _matmul3_kernel(a_ref, b_ref, o_ref):
    a_hi, a_lo = _split_bf16(a_ref[...])
    b_hi, b_lo = _split_bf16(b_ref[...])
    o_ref[...] = _dot(a_hi, b_hi) + _dot(a_hi, b_lo) + _dot(a_lo, b_hi)


def _matmul3(a, b, tm):
    m, k = a.shape
    n = b.shape[1]
    return pl.pallas_call(
        _matmul3_kernel,
        grid=(m // tm,),
        in_specs=[pl.BlockSpec((tm, k), lambda i: (i, 0)),
                  pl.BlockSpec((k, n), lambda i: (0, 0))],
        out_specs=pl.BlockSpec((tm, n), lambda i: (i, 0)),
        out_shape=jax.ShapeDtypeStruct((m, n), F32),
        compiler_params=_cparams(("parallel",)),
        name="matmul3",
    )(a, b)


def _log_sigmoid(x):
    return jnp.minimum(x, 0.0) - jnp.log1p(jnp.exp(-jnp.abs(x)))


def _mlstm_kernel(qk_ref, v_ref, og_ref, g_ref, gt_ref, cw_ref, cb_ref, bias_ref, biast_ref,
                  out_ref, ext_ref, c_ref, n_ref, m_ref, *, chunk, nh, dqk, dv):
    L = chunk
    qkw = nh * dqk

    @pl.when(pl.program_id(1) == 0)
    def _():
        ext_ref[0:SUBLANES, :] = jnp.zeros((SUBLANES, 2 * qkw), F32)
        c_ref[...] = jnp.zeros_like(c_ref)
        n_ref[...] = jnp.zeros_like(n_ref)
        m_ref[...] = jnp.zeros_like(m_ref)

    ext_ref[SUBLANES:SUBLANES + L, :] = qk_ref[...]
    cw = cw_ref[...]
    y = jnp.broadcast_to(cb_ref[...], (L, 2 * qkw))
    for w in range(CONV_WIDTH):
        y = y + ext_ref[pl.ds(SUBLANES - (CONV_WIDTH - 1) + w, L), :] * cw[w:w + 1, :]
    ext_ref[0:SUBLANES, :] = ext_ref[L:L + SUBLANES, :]
    qk = y * jax.nn.sigmoid(y)

    gates = g_ref[...] + bias_ref[...]
    gates_t = gt_ref[...] + biast_ref[...]
    row = lax.broadcasted_iota(jnp.int32, (L, L), 0)
    col = lax.broadcasted_iota(jnp.int32, (L, L), 1)
    causal = col <= row

    for h in range(nh):
        q = qk[:, h * dqk:(h + 1) * dqk]
        k = qk[:, qkw + h * dqk:qkw + (h + 1) * dqk] * (dqk ** -0.5)
        v_bf = v_ref[:, h * dv:(h + 1) * dv].astype(BF16)
        q_bf = q.astype(BF16)
        ig_c = gates[:, h:h + 1]
        ig_r = gates_t[h:h + 1, :]
        lf_c = _log_sigmoid(gates[:, nh + h:nh + h + 1])
        lf_r = _log_sigmoid(gates_t[nh + h:nh + h + 1, :])
        bcum_c = jnp.sum(jnp.where(causal, lf_r, 0.0), axis=1, keepdims=True)
        bcum_r = jnp.sum(jnp.where(row <= col, lf_c, 0.0), axis=0, keepdims=True)
        g_tot = jnp.sum(lf_r, axis=1, keepdims=True)
        m_prev = m_ref[h:h + 1, 0:1]

        dmat = jnp.where(causal, bcum_c - bcum_r + ig_r, -jnp.inf)
        inter_log = bcum_c + m_prev
        m_t = jnp.maximum(inter_log, jnp.max(dmat, axis=1, keepdims=True))
        s_qk = _dot_nt(q_bf, k.astype(BF16)) * jnp.exp(dmat - m_t)
        inter_w = jnp.exp(inter_log - m_t)
        c_prev = c_ref[h]
        n_prev = n_ref[h:h + 1, :]
        num = inter_w * _dot(q_bf, c_prev.astype(BF16)) + _dot(s_qk.astype(BF16), v_bf)
        den = inter_w * jnp.sum(q * n_prev, axis=1, keepdims=True) \
            + jnp.sum(s_qk, axis=1, keepdims=True)
        hval = num / jnp.maximum(jnp.abs(den), jnp.exp(-m_t))
        o_gate = jax.nn.sigmoid(og_ref[:, h * dv:(h + 1) * dv])
        out_ref[:, h * dv:(h + 1) * dv] = (o_gate * hval).astype(out_ref.dtype)

        a_c = g_tot - bcum_c + ig_c
        m_new = jnp.maximum(g_tot + m_prev, jnp.max(a_c, axis=0, keepdims=True))
        decay = jnp.exp(g_tot + m_prev - m_new)
        kw = k * jnp.exp(a_c - m_new)
        c_ref[h] = decay * c_prev + _dot_tn(kw.astype(BF16), v_bf)
        n_ref[h:h + 1, :] = decay * n_prev + jnp.sum(kw, axis=0, keepdims=True)
        m_ref[h:h + 1, :] = jnp.broadcast_to(m_new, (1, LANES))


def _mlstm(z, gates, gates_t, conv_w, conv_b, bias_row, bias_col, *, nh, dqk, dv, chunk):
    bsz, s, _ = z.shape
    qkw2 = 2 * nh * dqk
    vw = nh * dv
    assert qkw2 == vw, "q|k, v and output-gate column blocks must share one block width"
    nc = s // chunk
    kern = functools.partial(_mlstm_kernel, chunk=chunk, nh=nh, dqk=dqk, dv=dv)
    return pl.pallas_call(
        kern,
        grid=(bsz, nc),
        in_specs=[
            pl.BlockSpec((None, chunk, vw), lambda b, c: (b, c, 0)),
            pl.BlockSpec((None, chunk, vw), lambda b, c: (b, c, 1)),
            pl.BlockSpec((None, chunk, vw), lambda b, c: (b, c, 2)),
            pl.BlockSpec((None, chunk, LANES), lambda b, c: (b, c, 0)),
            pl.BlockSpec((None, 2 * SUBLANES, chunk), lambda b, c: (b, 0, c)),
            pl.BlockSpec((CONV_WIDTH, qkw2), lambda b, c: (0, 0)),
            pl.BlockSpec((1, qkw2), lambda b, c: (0, 0)),
            pl.BlockSpec((1, LANES), lambda b, c: (0, 0)),
            pl.BlockSpec((2 * SUBLANES, 1), lambda b, c: (0, 0)),
        ],
        out_specs=pl.BlockSpec((None, chunk, vw), lambda b, c: (b, c, 0)),
        out_shape=jax.ShapeDtypeStruct((bsz, s, vw), BF16),
        scratch_shapes=[
            pltpu.VMEM((chunk + SUBLANES, qkw2), F32),
            pltpu.VMEM((nh, dqk, dv), F32),
            pltpu.VMEM((SUBLANES, dqk), F32),
            pltpu.VMEM((SUBLANES, LANES), F32),
        ],
        compiler_params=_cparams(("parallel", "arbitrary")),
        name="mlstm",
    )(z, z, z, gates, gates_t, conv_w, conv_b, bias_row, bias_col)


def _mem_attn_kernel(q_ref, k_ref, v_ref, o_ref):
    scale = HEAD_DIM ** -0.5
    for h in range(MEM_HEADS):
        sl = slice(h * HEAD_DIM, (h + 1) * HEAD_DIM)
        q = q_ref[:, sl].astype(BF16)
        sc = _dot_nt(q, k_ref[:, sl]) * scale
        sc = sc - jnp.max(sc, axis=1, keepdims=True)
        p = jnp.exp(sc)
        p = p / jnp.sum(p, axis=1, keepdims=True)
        o_ref[:, sl] = _dot(p.astype(BF16), v_ref[:, sl]).astype(o_ref.dtype)


def _mem_attn(z, kv, qm_block, tq):
    bsz, s, _ = z.shape
    n_mem = kv.shape[1]
    return pl.pallas_call(
        _mem_attn_kernel,
        grid=(bsz, s // tq),
        in_specs=[
            pl.BlockSpec((None, tq, MEM_WIDTH), lambda b, i: (b, i, qm_block)),
            pl.BlockSpec((None, n_mem, MEM_WIDTH), lambda b, i: (b, 0, 0)),
            pl.BlockSpec((None, n_mem, MEM_WIDTH), lambda b, i: (b, 0, 1)),
        ],
        out_specs=pl.BlockSpec((None, tq, MEM_WIDTH), lambda b, i: (b, i, 0)),
        out_shape=jax.ShapeDtypeStruct((bsz, s, MEM_WIDTH), BF16),
        compiler_params=_cparams(("parallel", "parallel")),
        name="mem_attn",
    )(z, kv, kv)


def _moba_prep_kernel(q_ref, k_ref, v_ref, cos_ref, s1_ref, s2_ref,
                      qo_ref, ko_ref, vo_ref, kbar_ref, *, nh):
    cosf = cos_ref[...]
    s1 = s1_ref[...]
    s2 = s2_ref[...]
    half = ROPE_DIM // 2
    for h in range(nh):
        sl = slice(h * HEAD_DIM, (h + 1) * HEAD_DIM)
        for src, dst, is_k in ((q_ref, qo_ref, False), (k_ref, ko_ref, True)):
            x = src[:, sl]
            xr = (x * cosf + pltpu.roll(x, HEAD_DIM - half, 1) * s1
                  + pltpu.roll(x, half, 1) * s2)
            dst[:, sl] = xr.astype(dst.dtype)
            if is_k:
                kbar_ref[:, sl] = jnp.mean(xr, axis=0, keepdims=True)
    vo_ref[...] = v_ref[...].astype(vo_ref.dtype)


def _moba_prep(z, cos_t, s1_t, s2_t, nh):
    bsz, s, _ = z.shape
    w = nh * HEAD_DIM
    nb = s // MOBA_BLOCK
    t = MOBA_BLOCK
    kern = functools.partial(_moba_prep_kernel, nh=nh)
    tab = pl.BlockSpec((t, HEAD_DIM), lambda b, i: (i, 0))
    big = jax.ShapeDtypeStruct((bsz, s, w), BF16)
    return pl.pallas_call(
        kern,
        grid=(bsz, nb),
        in_specs=[pl.BlockSpec((None, t, w), lambda b, i: (b, i, 0)),
                  pl.BlockSpec((None, t, w), lambda b, i: (b, i, 1)),
                  pl.BlockSpec((None, t, w), lambda b, i: (b, i, 2)),
                  tab, tab, tab],
        out_specs=[pl.BlockSpec((None, t, w), lambda b, i: (b, i, 0)),
                   pl.BlockSpec((None, t, w), lambda b, i: (b, i, 0)),
                   pl.BlockSpec((None, t, w), lambda b, i: (b, i, 0)),
                   pl.BlockSpec((None, None, 1, w), lambda b, i: (b, i, 0, 0))],
        out_shape=[big, big, big, jax.ShapeDtypeStruct((bsz, nb, 1, w), F32)],
        compiler_params=_cparams(("parallel", "parallel")),
        name="moba_prep",
    )(z, z, z, cos_t, s1_t, s2_t)


def _moba_attn_kernel(q_ref, k_ref, v_ref, kbar_ref, o_ref, *, nb):
    t = MOBA_BLOCK
    qb = pl.program_id(2)
    q = q_ref[...]
    scale = HEAD_DIM ** -0.5

    kb_hi, kb_lo = _split_bf16(kbar_ref[...])
    gate = _dot_nt(q, kb_hi) + _dot_nt(q, kb_lo)
    blk = lax.broadcasted_iota(jnp.int32, (t, nb), 1)
    valid = blk < qb
    gm = jnp.where(valid, gate, -jnp.inf)
    rank = jnp.zeros((t, nb), jnp.int32)
    for n2 in range(nb):
        gc = gm[:, n2:n2 + 1]
        beats = (gc > gm) | ((gc == gm) & (n2 < blk))
        rank = rank + beats.astype(jnp.int32)
    sel = jnp.where(valid & (rank < MOBA_TOPK), 1.0, 0.0)

    row = lax.broadcasted_iota(jnp.int32, (t, t), 0)
    col = lax.broadcasted_iota(jnp.int32, (t, t), 1)
    own = pl.multiple_of(qb * t, t)
    s = _dot_nt(q, k_ref[pl.ds(own, t), :]) * scale
    s = jnp.where(col <= row, s, NEG_BIG)
    m0 = jnp.max(s, axis=1, keepdims=True)
    p = jnp.exp(s - m0)
    l0 = jnp.sum(p, axis=1, keepdims=True)
    acc0 = _dot(p.astype(BF16), v_ref[pl.ds(own, t), :])

    def body(n, carry):
        m, l, acc = carry
        start = pl.multiple_of(n * t, t)
        s = _dot_nt(q, k_ref[pl.ds(start, t), :]) * scale
        chosen = jnp.sum(jnp.where(blk == n, sel, 0.0), axis=1, keepdims=True) > 0.0
        s = jnp.where(chosen, s, NEG_BIG)
        m_new = jnp.maximum(m, jnp.max(s, axis=1, keepdims=True))
        alpha = jnp.exp(m - m_new)
        p = jnp.exp(s - m_new)
        l = alpha * l + jnp.sum(p, axis=1, keepdims=True)
        acc = alpha * acc + _dot(p.astype(BF16), v_ref[pl.ds(start, t), :])
        return m_new, l, acc

    _, l, acc = lax.fori_loop(0, qb, body, (m0, l0, acc0))
    o_ref[...] = (acc / l).astype(o_ref.dtype)


def _moba_attn(q, k, v, kbar, nh):
    bsz, s, _ = q.shape
    nb = s // MOBA_BLOCK
    t = MOBA_BLOCK
    kern = functools.partial(_moba_attn_kernel, nb=nb)
    return pl.pallas_call(
        kern,
        grid=(bsz, nh, nb),
        in_specs=[pl.BlockSpec((None, t, HEAD_DIM), lambda b, h, i: (b, i, h)),
                  pl.BlockSpec((None, s, HEAD_DIM), lambda b, h, i: (b, 0, h)),
                  pl.BlockSpec((None, s, HEAD_DIM), lambda b, h, i: (b, 0, h)),
                  pl.BlockSpec((None, nb, HEAD_DIM), lambda b, h, i: (b, 0, h))],
        out_specs=pl.BlockSpec((None, t, HEAD_DIM), lambda b, h, i: (b, i, h)),
        out_shape=jax.ShapeDtypeStruct((bsz, s, nh * HEAD_DIM), BF16),
        compiler_params=_cparams(("parallel", "parallel", "arbitrary")),
        name="moba_attn",
    )(q, k, v, kbar)


def _outproj_kernel(hmix_ref, hmem_ref, w1_ref, w2_ref, x_ref, g_ref, b_ref, wr_ref, br_ref,
                    x1_ref, tope_ref, gate_ref):
    y = _dot(hmix_ref[...], w1_ref[...]) + _dot(hmem_ref[...], w2_ref[...])
    x1 = _layer_norm_rows(DEEPNORM_ALPHA * x_ref[...] + y, g_ref[...], b_ref[...])
    x1_ref[...] = x1

    x_hi, x_lo = _split_bf16(x1)
    w_hi, w_lo = _split_bf16(wr_ref[...])
    logits = _dot_nt(w_hi, x_hi) + _dot_nt(w_hi, x_lo) + _dot_nt(w_lo, x_hi) + br_ref[...]
    ne, tm = logits.shape
    eid = lax.broadcasted_iota(jnp.int32, (ne, tm), 0)
    vals = []
    for k in range(TOP_K):
        mx = jnp.max(logits, axis=0, keepdims=True)
        idx = jnp.min(jnp.where(logits == mx, eid, ne), axis=0, keepdims=True)
        tope_ref[k:k + 1, :] = idx
        vals.append(mx)
        logits = jnp.where(eid == idx, -jnp.inf, logits)
    ex = [jnp.exp(vk - vals[0]) for vk in vals]
    tot = ex[0] + ex[1] + ex[2] + ex[3]
    for k in range(TOP_K):
        gate_ref[k:k + 1, :] = ex[k] / tot


def _outproj(hmix, hmem, w1, w2, x, g, b, wr_t, br_col, tm):
    m, d = x.shape
    return pl.pallas_call(
        _outproj_kernel,
        grid=(m // tm,),
        in_specs=[pl.BlockSpec((tm, hmix.shape[1]), lambda i: (i, 0)),
                  pl.BlockSpec((tm, hmem.shape[1]), lambda i: (i, 0)),
                  pl.BlockSpec(w1.shape, lambda i: (0, 0)),
                  pl.BlockSpec(w2.shape, lambda i: (0, 0)),
                  pl.BlockSpec((tm, d), lambda i: (i, 0)),
                  pl.BlockSpec((1, d), lambda i: (0, 0)),
                  pl.BlockSpec((1, d), lambda i: (0, 0)),
                  pl.BlockSpec(wr_t.shape, lambda i: (0, 0)),
                  pl.BlockSpec(br_col.shape, lambda i: (0, 0))],
        out_specs=[pl.BlockSpec((tm, d), lambda i: (i, 0)),
                   pl.BlockSpec((TOP_K, tm), lambda i: (0, i)),
                   pl.BlockSpec((TOP_K, tm), lambda i: (0, i))],
        out_shape=[jax.ShapeDtypeStruct((m, d), F32),
                   jax.ShapeDtypeStruct((TOP_K, m), jnp.int32),
                   jax.ShapeDtypeStruct((TOP_K, m), F32)],
        compiler_params=_cparams(("parallel",)),
        name="outproj_ln_router",
    )(hmix, hmem, w1, w2, x, g, b, wr_t, br_col)


def _gather_rows_kernel(nused_ref, tok_ref, x_hbm, o_ref, buf, sem):
    i = pl.program_id(0)
    rows = buf.shape[0]

    def row_copy(src_row, dst_row):
        return pltpu.make_async_copy(x_hbm.at[pl.ds(src_row, 1), :],
                                     buf.at[pl.ds(dst_row, 1), :], sem)

    @pl.when(i < nused_ref[0])
    def _():
        def issue(r, carry):
            row_copy(tok_ref[0, r], r).start()
            return carry

        lax.fori_loop(0, rows, issue, 0, unroll=8)

        def drain(r, carry):
            row_copy(0, r).wait()
            return carry

        lax.fori_loop(0, rows, drain, 0, unroll=8)
        o_ref[...] = buf[...].astype(o_ref.dtype)

    @pl.when(i >= nused_ref[0])
    def _():
        o_ref[...] = jnp.zeros_like(o_ref)


def _gather_rows(x, row_tok, nused, nblk):
    d = x.shape[1]
    r = MOE_ROW_BLOCK
    return pl.pallas_call(
        _gather_rows_kernel,
        grid_spec=pltpu.PrefetchScalarGridSpec(
            num_scalar_prefetch=1,
            grid=(nblk,),
            in_specs=[pl.BlockSpec((None, 1, r), lambda i, nu: (i, 0, 0),
                                   memory_space=pltpu.SMEM),
                      pl.BlockSpec(memory_space=pl.ANY)],
            out_specs=pl.BlockSpec((r, d), lambda i, nu: (i, 0)),
            scratch_shapes=[pltpu.VMEM((r, d), F32), pltpu.SemaphoreType.DMA(())]),
        out_shape=jax.ShapeDtypeStruct((nblk * r, d), BF16),
        compiler_params=_cparams(("arbitrary",)),
        name="moe_gather",
    )(nused, row_tok.reshape(nblk, 1, r), x)


def _expert_changed(blk_e_ref, i):
    prev = blk_e_ref[jnp.maximum(i - 1, 0)]
    return (i == 0) | (blk_e_ref[i] != prev)


def _gmm_gu_kernel(blk_e_ref, nused_ref, xs_ref, wg_ref, wu_ref, bg_ref, bu_ref, o_ref,
                   wg_bf, wu_bf):
    i = pl.program_id(1)

    @pl.when(i < nused_ref[0])
    def _():
        @pl.when(_expert_changed(blk_e_ref, i))
        def _():
            wg_bf[...] = wg_ref[...].astype(BF16)
            wu_bf[...] = wu_ref[...].astype(BF16)

        x = xs_ref[...]
        hg = _dot(x, wg_bf[...]) + bg_ref[...]
        hu = _dot(x, wu_bf[...]) + bu_ref[...]
        hg = jnp.minimum(hg, SWIGLU_LIMIT)
        hu = jnp.clip(hu, -SWIGLU_LIMIT, SWIGLU_LIMIT)
        hid = (hu + 1.0) * (hg * jax.nn.sigmoid(SWIGLU_ALPHA * hg))
        o_ref[...] = hid.astype(o_ref.dtype)

    @pl.when(i >= nused_ref[0])
    def _():
        o_ref[...] = jnp.zeros_like(o_ref)


def _gmm_gu(xs, w_gu, b_gu, blk_e, nused, layer, tf):
    rows, d = xs.shape
    r = MOE_ROW_BLOCK
    nblk = rows // r
    dff = w_gu.shape[-1] // 2
    nj = dff // tf
    b3 = b_gu.reshape(b_gu.shape[0], b_gu.shape[1], 1, 2 * dff)
    return pl.pallas_call(
        _gmm_gu_kernel,
        grid_spec=pltpu.PrefetchScalarGridSpec(
            num_scalar_prefetch=2,
            grid=(nj, nblk),
            in_specs=[
                pl.BlockSpec((r, d), lambda j, i, be, nu: (i, 0)),
                pl.BlockSpec((None, None, d, tf), lambda j, i, be, nu: (layer, be[i], 0, j)),
                pl.BlockSpec((None, None, d, tf), lambda j, i, be, nu: (layer, be[i], 0, nj + j)),
                pl.BlockSpec((None, None, 1, tf), lambda j, i, be, nu: (layer, be[i], 0, j)),
                pl.BlockSpec((None, None, 1, tf), lambda j, i, be, nu: (layer, be[i], 0, nj + j)),
            ],
            out_specs=pl.BlockSpec((r, tf), lambda j, i, be, nu: (i, j)),
            scratch_shapes=[pltpu.VMEM((d, tf), BF16), pltpu.VMEM((d, tf), BF16)]),
        out_shape=jax.ShapeDtypeStruct((rows, dff), BF16),
        compiler_params=_cparams(("arbitrary", "arbitrary")),
        name="moe_gate_up",
    )(blk_e, nused, xs, w_gu, w_gu, b3, b3)


def _gmm_down_kernel(blk_e_ref, nused_ref, h_ref, w_ref, b_ref, o_ref, w_bf):
    i = pl.program_id(1)

    @pl.when(i < nused_ref[0])
    def _():
        @pl.when(_expert_changed(blk_e_ref, i))
        def _():
            w_bf[...] = w_ref[...].astype(BF16)

        o_ref[...] = _dot(h_ref[...], w_bf[...]) + b_ref[...]

    @pl.when(i >= nused_ref[0])
    def _():
        o_ref[...] = jnp.zeros_like(o_ref)


def _gmm_down(hid, w_down, b_down, blk_e, nused, layer, tn):
    rows, dff = hid.shape
    r = MOE_ROW_BLOCK
    nblk = rows // r
    d = w_down.shape[-1]
    b3 = b_down.reshape(b_down.shape[0], b_down.shape[1], 1, d)
    return pl.pallas_call(
        _gmm_down_kernel,
        grid_spec=pltpu.PrefetchScalarGridSpec(
            num_scalar_prefetch=2,
            grid=(d // tn, nblk),
            in_specs=[
                pl.BlockSpec((r, dff), lambda j, i, be, nu: (i, 0)),
                pl.BlockSpec((None, None, dff, tn), lambda j, i, be, nu: (layer, be[i], 0, j)),
                pl.BlockSpec((None, None, 1, tn), lambda j, i, be, nu: (layer, be[i], 0, j)),
            ],
            out_specs=pl.BlockSpec((r, tn), lambda j, i, be, nu: (i, j)),
            scratch_shapes=[pltpu.VMEM((dff, tn), BF16)]),
        out_shape=jax.ShapeDtypeStruct((rows, d), F32),
        compiler_params=_cparams(("arbitrary", "arbitrary")),
        name="moe_down",
    )(blk_e, nused, hid, w_down, b3)


def _combine_kernel(pos_ref, ys_hbm, x_ref, gate_ref, g_ref, b_ref, o_ref, buf, sem):
    tb = x_ref.shape[0]

    def row_copy(src_row, k, dst_row):
        return pltpu.make_async_copy(ys_hbm.at[pl.ds(src_row, 1), :],
                                     buf.at[k, pl.ds(dst_row, 1), :], sem)

    def issue(r, carry):
        for k in range(TOP_K):
            row_copy(pos_ref[0, r * TOP_K + k], k, r).start()
        return carry

    lax.fori_loop(0, tb, issue, 0, unroll=4)

    def drain(r, carry):
        for k in range(TOP_K):
            row_copy(0, k, r).wait()
        return carry

    lax.fori_loop(0, tb, drain, 0, unroll=4)

    gate = gate_ref[...]
    f = gate[:, 0:1] * buf[0]
    for k in range(1, TOP_K):
        f = f + gate[:, k:k + 1] * buf[k]
    o_ref[...] = _layer_norm_rows(DEEPNORM_ALPHA * x_ref[...] + f, g_ref[...], b_ref[...])


def _combine(ys, pos, gate, x, g, b, tb):
    n, d = x.shape
    nblk = n // tb
    return pl.pallas_call(
        _combine_kernel,
        grid=(nblk,),
        in_specs=[pl.BlockSpec((None, 1, tb * TOP_K), lambda i: (i, 0, 0),
                               memory_space=pltpu.SMEM),
                  pl.BlockSpec(memory_space=pl.ANY),
                  pl.BlockSpec((tb, d), lambda i: (i, 0)),
                  pl.BlockSpec((tb, TOP_K), lambda i: (i, 0)),
                  pl.BlockSpec((1, d), lambda i: (0, 0)),
                  pl.BlockSpec((1, d), lambda i: (0, 0))],
        out_specs=pl.BlockSpec((tb, d), lambda i: (i, 0)),
        out_shape=jax.ShapeDtypeStruct((n, d), F32),
        scratch_shapes=[pltpu.VMEM((TOP_K, tb, d), F32), pltpu.SemaphoreType.DMA(())],
        compiler_params=_cparams(("arbitrary",)),
        name="moe_combine",
    )(pos.reshape(nblk, 1, tb * TOP_K), ys, x, gate, g, b)


def _dispatch_plan(top_e_t):
    n = top_e_t.shape[1]
    nk = n * TOP_K
    r = MOE_ROW_BLOCK
    e_flat = top_e_t.T.reshape(-1)
    order = jnp.argsort(e_flat, stable=True).astype(jnp.int32)
    e_sorted = e_flat[order]
    counts = jnp.sum(e_flat[:, None] == jnp.arange(N_EXPERTS, dtype=jnp.int32)[None, :],
                     axis=0, dtype=jnp.int32)
    padded = (counts + r - 1) // r * r
    start = jnp.cumsum(counts) - counts
    pend = jnp.cumsum(padded)
    pstart = pend - padded
    dest = pstart[e_sorted] + jnp.arange(nk, dtype=jnp.int32) - start[e_sorted]
    nblk = -(-nk // r) + N_EXPERTS
    row_tok = jnp.zeros((nblk * r,), jnp.int32).at[dest].set(order // TOP_K)
    pos = jnp.zeros((nk,), jnp.int32).at[order].set(dest).reshape(n, TOP_K)
    blk_e = jnp.minimum(
        jnp.searchsorted(pend, jnp.arange(nblk, dtype=jnp.int32) * r, side='right'),
        N_EXPERTS - 1).astype(jnp.int32)
    nused = (pend[-1:] // r).astype(jnp.int32)
    return row_tok, pos, blk_e, nused, nblk


def _moe_and_norm(x1, top_e_t, gate_t, w_gu, b_gu, w_down, b_down, g, b, layer):
    row_tok, pos, blk_e, nused, nblk = _dispatch_plan(top_e_t)
    xs = _gather_rows(x1, row_tok, nused, nblk)
    hid = _gmm_gu(xs, w_gu, b_gu, blk_e, nused, layer, tf=512)
    ys = _gmm_down(hid, w_down, b_down, blk_e, nused, layer, tn=512)
    return _combine(ys, pos, gate_t.T, x1, g, b, tb=128)


def _rotary_tables(s):
    half = ROPE_DIM // 2
    inv_freq = ROPE_THETA ** (-jnp.arange(half, dtype=F32) * 2.0 / ROPE_DIM)
    ang = jnp.arange(s, dtype=F32)[:, None] * inv_freq[None, :]
    cos, sin = jnp.cos(ang), jnp.sin(ang)
    rest = HEAD_DIM - ROPE_DIM
    cos_t = jnp.concatenate([cos, cos, jnp.ones((s, rest), F32)], axis=1)
    s1_t = jnp.concatenate([-sin, jnp.zeros((s, half + rest), F32)], axis=1)
    s2_t = jnp.concatenate([jnp.zeros((s, half), F32), sin, jnp.zeros((s, rest), F32)], axis=1)
    return cos_t, s1_t, s2_t


def kernel(x, mem, mlstm_w_in, mlstm_conv_w, mlstm_conv_b, mlstm_b_igate, mlstm_b_fgate, moba_w_in, w_mem_kv, w_out, ln1_g, ln1_b, w_router, b_router, w_gu, b_gu, w_down, b_down, ln2_g, ln2_b):
    bsz, s, d = x.shape
    n = bsz * s
    n_mem = mem.shape[1]
    mix_w = d - MEM_WIDTH
    nh_ml = MLSTM_HEADS
    dv = mix_w // nh_ml
    dqk = dv // 2
    qkw2 = 2 * nh_ml * dqk
    nh_mb = mix_w // HEAD_DIM
    qm_block = 3 * mix_w // MEM_WIDTH
    assert qkw2 == mix_w and 3 * mix_w % MEM_WIDTH == 0

    xf = x.reshape(n, d)
    memf = mem.reshape(bsz * n_mem, d)
    for i in range(DEPTH):
        j = i // 2
        if i % 2 == 0:
            w_in = mlstm_w_in[j]
            gate_lo = qkw2 + 2 * mix_w
            gate_hi = gate_lo + 2 * nh_ml
            w_main = jnp.concatenate([w_in[:, :gate_lo], w_in[:, gate_hi:]], axis=1).astype(BF16)
            w_gate = jnp.pad(w_in[:, gate_lo:gate_hi], ((0, 0), (0, LANES - 2 * nh_ml)))
            z = _matmul(xf, w_main, F32, tm=1024, tn=512).reshape(bsz, s, -1)
            gates = _matmul3(xf, w_gate, tm=1024).reshape(bsz, s, LANES)
            gates_t = jnp.swapaxes(gates[:, :, :2 * SUBLANES], 1, 2)
            bias = jnp.concatenate([mlstm_b_igate[j], mlstm_b_fgate[j]])
            bias_row = jnp.pad(bias, (0, LANES - 2 * nh_ml)).reshape(1, LANES)
            bias_col = jnp.pad(bias, (0, 2 * SUBLANES - 2 * nh_ml)).reshape(2 * SUBLANES, 1)
            h_mix = _mlstm(z, gates, gates_t, mlstm_conv_w[j], mlstm_conv_b[j].reshape(1, -1),
                           bias_row, bias_col, nh=nh_ml, dqk=dqk, dv=dv, chunk=MLSTM_CHUNK)
        else:
            z = _matmul(xf, moba_w_in[j].astype(BF16), F32, tm=1024, tn=512).reshape(bsz, s, -1)
            cos_t, s1_t, s2_t = _rotary_tables(s)
            q_r, k_r, v_b, kbar = _moba_prep(z, cos_t, s1_t, s2_t, nh_mb)
            h_mix = _moba_attn(q_r, k_r, v_b, kbar.reshape(bsz, s // MOBA_BLOCK, mix_w), nh_mb)
        kv = _matmul(memf, w_mem_kv[i].astype(BF16), BF16, tm=1024, tn=512)
        h_mem = _mem_attn(z, kv.reshape(bsz, n_mem, 2 * MEM_WIDTH), qm_block, tq=512)
        w_o = w_out[i].astype(BF16)
        x1, top_e_t, gate_t = _outproj(
            h_mix.reshape(n, mix_w), h_mem.reshape(n, MEM_WIDTH), w_o[:mix_w], w_o[mix_w:], xf,
            ln1_g[i].reshape(1, d), ln1_b[i].reshape(1, d),
            w_router[i].T, b_router[i].reshape(N_EXPERTS, 1), tm=512)
        xf = _moe_and_norm(x1, top_e_t, gate_t, w_gu, b_gu, w_down, b_down,
                           ln2_g[i].reshape(1, d), ln2_b[i].reshape(1, d), i)
    return xf.reshape(bsz, s, d)
```

```python
import functools

import jax
import jax.numpy as jnp
from jax import lax
from jax.experimental import pallas as pl
from jax.experimental.pallas import tpu as pltpu

F32 = jnp.float32
BF16 = jnp.bfloat16
U32 = jnp.uint32
I32 = jnp.int32

HEAD_DIM = 128
MEM_HEADS = 4
MEM_WIDTH = MEM_HEADS * HEAD_DIM
MLSTM_HEADS = 6
CONV_WIDTH = 4
MOBA_BLOCK = 256
MOBA_TOPK = 3
ROPE_THETA = 500000.0
ROPE_DIM = HEAD_DIM // 4
N_EXPERTS = 32
TOP_K = 4
SWIGLU_LIMIT = 7.0
SWIGLU_ALPHA = 1.702
LN_EPS = 1e-5
DEPTH = 2
DEEPNORM_ALPHA = (2 * DEPTH) ** 0.25

LANES = 128
SUBLANES = 8
VMEM_LIMIT_BYTES = 56 * 1024 * 1024
MLSTM_CHUNK = 128
MOE_ROW_BLOCK = 256
MOE_SUPER_ROWS = 2048
MOE_FF_TILE = 256
NEG_BIG = -1e30


def _cparams(sem):
    return pltpu.CompilerParams(dimension_semantics=sem, vmem_limit_bytes=VMEM_LIMIT_BYTES)


def _dot(a, b):
    return jnp.dot(a, b, preferred_element_type=F32)


def _dot_nt(a, b):
    return lax.dot_general(a, b, (((1,), (1,)), ((), ())), preferred_element_type=F32)


def _dot_tn(a, b):
    return lax.dot_general(a, b, (((0,), (0,)), ((), ())), preferred_element_type=F32)


def _split_bf16(x):
    hi = x.astype(BF16)
    lo = (x - hi.astype(F32)).astype(BF16)
    return hi, lo


def _layer_norm_rows(r, g, b):
    mu = jnp.mean(r, axis=-1, keepdims=True)
    d = r - mu
    var = jnp.mean(d * d, axis=-1, keepdims=True)
    return d * lax.rsqrt(var + LN_EPS) * g + b


def _pack_bf16_pairs(x):
    w = x.shape[1] // 2
    lo = pltpu.bitcast(x[:, :w].astype(BF16).astype(F32), U32)
    hi = pltpu.bitcast(x[:, w:].astype(BF16).astype(F32), U32)
    return (hi & jnp.uint32(0xFFFF0000)) | (lo >> 16)


def _unpack_bf16_pairs(p):
    lo = pltpu.bitcast(p << 16, F32).astype(BF16)
    hi = pltpu.bitcast(p & jnp.uint32(0xFFFF0000), F32).astype(BF16)
    return lo, hi


def _matmul_kernel(a_ref, b_ref, o_ref, a_bf_ref):
    @pl.when(pl.program_id(1) == 0)
    def _():
        a_bf_ref[...] = a_ref[...].astype(BF16)

    o_ref[...] = _dot(a_bf_ref[...], b_ref[...]).astype(o_ref.dtype)


def _matmul(a, b, out_dtype, tm, tn):
    m, k = a.shape
    n = b.shape[1]
    return pl.pallas_call(
        _matmul_kernel,
        grid=(m // tm, n // tn),
        in_specs=[pl.BlockSpec((tm, k), lambda i, j: (i, 0)),
                  pl.BlockSpec((k, tn), lambda i, j: (0, j))],
        out_specs=pl.BlockSpec((tm, tn), lambda i, j: (i, j)),
        out_shape=jax.ShapeDtypeStruct((m, n), out_dtype),
        scratch_shapes=[pltpu.VMEM((tm, k), BF16)],
        compiler_params=_cparams(("parallel", "arbitrary")),
        name="matmul",
    )(a, b)


def _matmul3_kernel(a_ref, b_ref, o_ref):
    a_hi, a_lo = _split_bf16(a_ref[...])
    b_hi, b_lo = _split_bf16(b_ref[...])
    o_ref[...] = _dot(a_hi, b_hi) + _dot(a_hi, b_lo) + _dot(a_lo, b_hi)


def _matmul3(a, b, tm):
    m, k = a.shape
    n = b.shape[1]
    return pl.pallas_call(
        _matmul3_kernel,
        grid=(m // tm,),
        in_specs=[pl.BlockSpec((tm, k), lambda i: (i, 0)),
                  pl.BlockSpec((k, n), lambda i: (0, 0))],
        out_specs=pl.BlockSpec((tm, n), lambda i: (i, 0)),
        out_shape=jax.ShapeDtypeStruct((m, n), F32),
        compiler_params=_cparams(("parallel",)),
        name="matmul3",
    )(a, b)


def _log_sigmoid(x):
    return jnp.minimum(x, 0.0) - jnp.log1p(jnp.exp(-jnp.abs(x)))


def _mlstm_kernel(qk_ref, v_ref, og_ref, g_ref, gt_ref, cw_ref, cb_ref, bias_ref, biast_ref,
                  out_ref, ext_ref, c_ref, n_ref, m_ref, *, chunk, nh, dqk, dv):
    L = chunk
    qkw = nh * dqk

    @pl.when(pl.program_id(1) == 0)
    def _():
        ext_ref[0:SUBLANES, :] = jnp.zeros((SUBLANES, 2 * qkw), F32)
        c_ref[...] = jnp.zeros_like(c_ref)
        n_ref[...] = jnp.zeros_like(n_ref)
        m_ref[...] = jnp.zeros_like(m_ref)

    ext_ref[SUBLANES:SUBLANES + L, :] = qk_ref[...]
    cw = cw_ref[...]
    y = jnp.broadcast_to(cb_ref[...], (L, 2 * qkw))
    for w in range(CONV_WIDTH):
        y = y + ext_ref[pl.ds(SUBLANES - (CONV_WIDTH - 1) + w, L), :] * cw[w:w + 1, :]
    ext_ref[0:SUBLANES, :] = ext_ref[L:L + SUBLANES, :]
    qk = y * jax.nn.sigmoid(y)

    gates = g_ref[...] + bias_ref[...]
    gates_t = gt_ref[...] + biast_ref[...]
    row = lax.broadcasted_iota(jnp.int32, (L, L), 0)
    col = lax.broadcasted_iota(jnp.int32, (L, L), 1)
    causal = col <= row

    for h in range(nh):
        q = qk[:, h * dqk:(h + 1) * dqk]
        k = qk[:, qkw + h * dqk:qkw + (h + 1) * dqk] * (dqk ** -0.5)
        v_bf = v_ref[:, h * dv:(h + 1) * dv].astype(BF16)
        q_bf = q.astype(BF16)
        ig_c = gates[:, h:h + 1]
        ig_r = gates_t[h:h + 1, :]
        lf_c = _log_sigmoid(gates[:, nh + h:nh + h + 1])
        lf_r = _log_sigmoid(gates_t[nh + h:nh + h + 1, :])
        bcum_c = jnp.sum(jnp.where(causal, lf_r, 0.0), axis=1, keepdims=True)
        bcum_r = jnp.sum(jnp.where(row <= col, lf_c, 0.0), axis=0, keepdims=True)
        g_tot = jnp.sum(lf_r, axis=1, keepdims=True)
        m_prev = m_ref[h:h + 1, 0:1]

        dmat = jnp.where(causal, bcum_c - bcum_r + ig_r, -jnp.inf)
        inter_log = bcum_c + m_prev
        m_t = jnp.maximum(inter_log, jnp.max(dmat, axis=1, keepdims=True))
        s_qk = _dot_nt(q_bf, k.astype(BF16)) * jnp.exp(dmat - m_t)
        inter_w = jnp.exp(inter_log - m_t)
        c_prev = c_ref[h]
        n_prev = n_ref[h:h + 1, :]
        num = inter_w * _dot(q_bf, c_prev.astype(BF16)) + _dot(s_qk.astype(BF16), v_bf)
        den = inter_w * jnp.sum(q * n_prev, axis=1, keepdims=True) \
            + jnp.sum(s_qk, axis=1, keepdims=True)
        hval = num / jnp.maximum(jnp.abs(den), jnp.exp(-m_t))
        o_gate = jax.nn.sigmoid(og_ref[:, h * dv:(h + 1) * dv])
        out_ref[:, h * dv:(h + 1) * dv] = (o_gate * hval).astype(out_ref.dtype)

        a_c = g_tot - bcum_c + ig_c
        m_new = jnp.maximum(g_tot + m_prev, jnp.max(a_c, axis=0, keepdims=True))
        decay = jnp.exp(g_tot + m_prev - m_new)
        kw = k * jnp.exp(a_c - m_new)
        c_ref[h] = decay * c_prev + _dot_tn(kw.astype(BF16), v_bf)
        n_ref[h:h + 1, :] = decay * n_prev + jnp.sum(kw, axis=0, keepdims=True)
        m_ref[h:h + 1, :] = jnp.broadcast_to(m_new, (1, LANES))


def _mlstm(z, gates, gates_t, conv_w, conv_b, bias_row, bias_col, *, nh, dqk, dv, chunk):
    bsz, s, _ = z.shape
    qkw2 = 2 * nh * dqk
    vw = nh * dv
    assert qkw2 == vw, "q|k, v and output-gate column blocks must share one block width"
    nc = s // chunk
    kern = functools.partial(_mlstm_kernel, chunk=chunk, nh=nh, dqk=dqk, dv=dv)
    return pl.pallas_call(
        kern,
        grid=(bsz, nc),
        in_specs=[
            pl.BlockSpec((None, chunk, vw), lambda b, c: (b, c, 0)),
            pl.BlockSpec((None, chunk, vw), lambda b, c: (b, c, 1)),
            pl.BlockSpec((None, chunk, vw), lambda b, c: (b, c, 2)),
            pl.BlockSpec((None, chunk, LANES), lambda b, c: (b, c, 0)),
            pl.BlockSpec((None, 2 * SUBLANES, chunk), lambda b, c: (b, 0, c)),
            pl.BlockSpec((CONV_WIDTH, qkw2), lambda b, c: (0, 0)),
            pl.BlockSpec((1, qkw2), lambda b, c: (0, 0)),
            pl.BlockSpec((1, LANES), lambda b, c: (0, 0)),
            pl.BlockSpec((2 * SUBLANES, 1), lambda b, c: (0, 0)),
        ],
        out_specs=pl.BlockSpec((None, chunk, vw), lambda b, c: (b, c, 0)),
        out_shape=jax.ShapeDtypeStruct((bsz, s, vw), BF16),
        scratch_shapes=[
            pltpu.VMEM((chunk + SUBLANES, qkw2), F32),
            pltpu.VMEM((nh, dqk, dv), F32),
            pltpu.VMEM((SUBLANES, dqk), F32),
            pltpu.VMEM((SUBLANES, LANES), F32),
        ],
        compiler_params=_cparams(("parallel", "arbitrary")),
        name="mlstm",
    )(z, z, z, gates, gates_t, conv_w, conv_b, bias_row, bias_col)


def _mem_attn_kernel(q_ref, k_ref, v_ref, o_ref):
    scale = HEAD_DIM ** -0.5
    for h in range(MEM_HEADS):
        sl = slice(h * HEAD_DIM, (h + 1) * HEAD_DIM)
        q = q_ref[:, sl].astype(BF16)
        sc = _dot_nt(q, k_ref[:, sl]) * scale
        sc = sc - jnp.max(sc, axis=1, keepdims=True)
        p = jnp.exp(sc)
        p = p / jnp.sum(p, axis=1, keepdims=True)
        o_ref[:, sl] = _dot(p.astype(BF16), v_ref[:, sl]).astype(o_ref.dtype)


def _mem_attn(z, kv, qm_block, tq):
    bsz, s, _ = z.shape
    n_mem = kv.shape[1]
    return pl.pallas_call(
        _mem_attn_kernel,
        grid=(bsz, s // tq),
        in_specs=[
            pl.BlockSpec((None, tq, MEM_WIDTH), lambda b, i: (b, i, qm_block)),
            pl.BlockSpec((None, n_mem, MEM_WIDTH), lambda b, i: (b, 0, 0)),
            pl.BlockSpec((None, n_mem, MEM_WIDTH), lambda b, i: (b, 0, 1)),
        ],
        out_specs=pl.BlockSpec((None, tq, MEM_WIDTH), lambda b, i: (b, i, 0)),
        out_shape=jax.ShapeDtypeStruct((bsz, s, MEM_WIDTH), BF16),
        compiler_params=_cparams(("parallel", "parallel")),
        name="mem_attn",
    )(z, kv, kv)


def _moba_prep_kernel(q_ref, k_ref, v_ref, cos_ref, s1_ref, s2_ref,
                      qo_ref, ko_ref, vo_ref, kbar_ref, *, nh):
    cosf = cos_ref[...]
    s1 = s1_ref[...]
    s2 = s2_ref[...]
    half = ROPE_DIM // 2
    for h in range(nh):
        sl = slice(h * HEAD_DIM, (h + 1) * HEAD_DIM)
        for src, dst, is_k in ((q_ref, qo_ref, False), (k_ref, ko_ref, True)):
            x = src[:, sl]
            xr = (x * cosf + pltpu.roll(x, HEAD_DIM - half, 1) * s1
                  + pltpu.roll(x, half, 1) * s2)
            if is_k:
                dst[:, sl] = xr.astype(dst.dtype)
                kbar_ref[:, sl] = jnp.mean(xr, axis=0, keepdims=True)
            else:
                dst[:, sl] = (xr * (HEAD_DIM ** -0.5)).astype(dst.dtype)
    vo_ref[...] = v_ref[...].astype(vo_ref.dtype)


def _moba_prep(z, cos_t, s1_t, s2_t, nh):
    bsz, s, _ = z.shape
    w = nh * HEAD_DIM
    nb = s // MOBA_BLOCK
    t = MOBA_BLOCK
    kern = functools.partial(_moba_prep_kernel, nh=nh)
    tab = pl.BlockSpec((t, HEAD_DIM), lambda b, i: (i, 0))
    big = jax.ShapeDtypeStruct((bsz, s, w), BF16)
    return pl.pallas_call(
        kern,
        grid=(bsz, nb),
        in_specs=[pl.BlockSpec((None, t, w), lambda b, i: (b, i, 0)),
                  pl.BlockSpec((None, t, w), lambda b, i: (b, i, 1)),
                  pl.BlockSpec((None, t, w), lambda b, i: (b, i, 2)),
                  tab, tab, tab],
        out_specs=[pl.BlockSpec((None, t, w), lambda b, i: (b, i, 0)),
                   pl.BlockSpec((None, t, w), lambda b, i: (b, i, 0)),
                   pl.BlockSpec((None, t, w), lambda b, i: (b, i, 0)),
                   pl.BlockSpec((None, None, 1, w), lambda b, i: (b, i, 0, 0))],
        out_shape=[big, big, big, jax.ShapeDtypeStruct((bsz, nb, 1, w), F32)],
        compiler_params=_cparams(("parallel", "parallel")),
        name="moba_prep",
    )(z, z, z, cos_t, s1_t, s2_t)


def _moba_attn_kernel(q_ref, k_ref, v_ref, kbar_ref, o_ref, *, nb):
    t = MOBA_BLOCK
    qb = pl.program_id(2)
    q = q_ref[...]

    kb_hi, kb_lo = _split_bf16(kbar_ref[...])
    gate = _dot_nt(q, kb_hi) + _dot_nt(q, kb_lo)
    blk = lax.broadcasted_iota(jnp.int32, (t, nb), 1)
    valid = blk < qb
    gm = jnp.where(valid, gate, -jnp.inf)
    rank = jnp.zeros((t, nb), jnp.int32)
    for n2 in range(nb):
        gc = gm[:, n2:n2 + 1]
        beats = (gc > gm) | ((gc == gm) & (n2 < blk))
        rank = rank + beats.astype(jnp.int32)
    sel_bias = jnp.where(valid & (rank < MOBA_TOPK), 0.0, NEG_BIG)

    row = lax.broadcasted_iota(jnp.int32, (t, t), 0)
    col = lax.broadcasted_iota(jnp.int32, (t, t), 1)
    causal = col <= row

    for c in range(nb):
        @pl.when(qb == c)
        def _(c=c):
            pieces = []
            for n in range(c + 1):
                s = _dot_nt(q, k_ref[n * t:(n + 1) * t, :])
                if n == c:
                    pieces.append(jnp.where(causal, s, NEG_BIG))
                else:
                    pieces.append(s + sel_bias[:, n:n + 1])
            mx = pieces[0]
            for piece in pieces[1:]:
                mx = jnp.maximum(mx, piece)
            m = jnp.max(mx, axis=1, keepdims=True)
            psum = None
            acc = None
            for n in range(c + 1):
                p = jnp.exp(pieces[n] - m)
                pv = _dot(p.astype(BF16), v_ref[n * t:(n + 1) * t, :])
                psum = p if psum is None else psum + p
                acc = pv if acc is None else acc + pv
            l = jnp.sum(psum, axis=1, keepdims=True)
            o_ref[...] = (acc / l).astype(o_ref.dtype)


def _moba_attn(q, k, v, kbar, nh):
    bsz, s, _ = q.shape
    nb = s // MOBA_BLOCK
    t = MOBA_BLOCK
    kern = functools.partial(_moba_attn_kernel, nb=nb)
    return pl.pallas_call(
        kern,
        grid=(bsz, nh, nb),
        in_specs=[pl.BlockSpec((None, t, HEAD_DIM), lambda b, h, i: (b, i, h)),
                  pl.BlockSpec((None, s, HEAD_DIM), lambda b, h, i: (b, 0, h)),
                  pl.BlockSpec((None, s, HEAD_DIM), lambda b, h, i: (b, 0, h)),
                  pl.BlockSpec((None, nb, HEAD_DIM), lambda b, h, i: (b, 0, h))],
        out_specs=pl.BlockSpec((None, t, HEAD_DIM), lambda b, h, i: (b, i, h)),
        out_shape=jax.ShapeDtypeStruct((bsz, s, nh * HEAD_DIM), BF16),
        compiler_params=_cparams(("parallel", "parallel", "arbitrary")),
        name="moba_attn",
    )(q, k, v, kbar)


def _outproj_kernel(hmix_ref, hmem_ref, w1_ref, w2_ref, x_ref, g_ref, b_ref, wr_ref, br_ref,
                    x1_ref, x1p_ref, tope_ref, gate_ref, rank_ref, cnt_ref, tri_ref, run_ref):
    i = pl.program_id(0)
    y = _dot(hmix_ref[...], w1_ref[...]) + _dot(hmem_ref[...], w2_ref[...])
    x1 = _layer_norm_rows(DEEPNORM_ALPHA * x_ref[...] + y, g_ref[...], b_ref[...])
    x1_ref[...] = x1
    x1p_ref[...] = _pack_bf16_pairs(x1)

    x_hi, x_lo = _split_bf16(x1)
    w_hi, w_lo = _split_bf16(wr_ref[...])
    logits = _dot_nt(w_hi, x_hi) + _dot_nt(w_hi, x_lo) + _dot_nt(w_lo, x_hi) + br_ref[...]
    ne, tm = logits.shape

    @pl.when(i == 0)
    def _():
        r = lax.broadcasted_iota(jnp.int32, (tm, tm), 0)
        c = lax.broadcasted_iota(jnp.int32, (tm, tm), 1)
        tri_ref[...] = jnp.where(r <= c, 1.0, 0.0).astype(BF16)
        run_ref[...] = jnp.zeros_like(run_ref)

    eid = lax.broadcasted_iota(jnp.int32, (ne, tm), 0)
    run = run_ref[...]
    vals = []
    for k in range(TOP_K):
        mx = jnp.max(logits, axis=0, keepdims=True)
        idx = jnp.min(jnp.where(logits == mx, eid, ne), axis=0, keepdims=True)
        tope_ref[k:k + 1, :] = idx
        vals.append(mx)
        hit = eid == idx
        logits = jnp.where(hit, -jnp.inf, logits)
        incl = _dot(jnp.where(hit, 1.0, 0.0).astype(BF16), tri_ref[...])
        rank = jnp.sum(jnp.where(hit, run + incl - 1.0, 0.0), axis=0, keepdims=True)
        rank_ref[k:k + 1, :] = rank.astype(jnp.int32)
        run = run + incl[:, tm - 1:tm]
    run_ref[...] = run
    cnt_ref[...] = jnp.broadcast_to(run, cnt_ref.shape)
    ex = [jnp.exp(vk - vals[0]) for vk in vals]
    tot = ex[0] + ex[1] + ex[2] + ex[3]
    for k in range(TOP_K):
        gate_ref[k:k + 1, :] = ex[k] / tot


def _outproj(hmix, hmem, w1, w2, x, g, b, wr_t, br_col, tm):
    m, d = x.shape
    ne = wr_t.shape[0]
    return pl.pallas_call(
        _outproj_kernel,
        grid=(m // tm,),
        in_specs=[pl.BlockSpec((tm, hmix.shape[1]), lambda i: (i, 0)),
                  pl.BlockSpec((tm, hmem.shape[1]), lambda i: (i, 0)),
                  pl.BlockSpec(w1.shape, lambda i: (0, 0)),
                  pl.BlockSpec(w2.shape, lambda i: (0, 0)),
                  pl.BlockSpec((tm, d), lambda i: (i, 0)),
                  pl.BlockSpec((1, d), lambda i: (0, 0)),
                  pl.BlockSpec((1, d), lambda i: (0, 0)),
                  pl.BlockSpec(wr_t.shape, lambda i: (0, 0)),
                  pl.BlockSpec(br_col.shape, lambda i: (0, 0))],
        out_specs=[pl.BlockSpec((tm, d), lambda i: (i, 0)),
                   pl.BlockSpec((tm, d // 2), lambda i: (i, 0)),
                   pl.BlockSpec((TOP_K, tm), lambda i: (0, i)),
                   pl.BlockSpec((TOP_K, tm), lambda i: (0, i)),
                   pl.BlockSpec((TOP_K, tm), lambda i: (0, i)),
                   pl.BlockSpec((ne, LANES), lambda i: (0, 0))],
        out_shape=[jax.ShapeDtypeStruct((m, d), F32),
                   jax.ShapeDtypeStruct((m, d // 2), U32),
                   jax.ShapeDtypeStruct((TOP_K, m), jnp.int32),
                   jax.ShapeDtypeStruct((TOP_K, m), F32),
                   jax.ShapeDtypeStruct((TOP_K, m), jnp.int32),
                   jax.ShapeDtypeStruct((ne, LANES), F32)],
        scratch_shapes=[pltpu.VMEM((tm, tm), BF16), pltpu.VMEM((ne, 1), F32)],
        compiler_params=_cparams(("arbitrary",)),
        name="outproj_ln_router",
    )(hmix, hmem, w1, w2, x, g, b, wr_t, br_col)


def _scatter_rows_kernel(zoff_ref, zpad_ref, meta_ref, dest_ref, x_hbm, xs_hbm, zbuf, sem, zsem,
                         *, tb, nblk):
    i = pl.program_id(0)
    r = MOE_ROW_BLOCK
    nbits = r.bit_length() - 1

    def zero_fill(wait):
        def piece(rows, dst_row):
            cp = pltpu.make_async_copy(zbuf.at[pl.ds(0, rows), :],
                                       xs_hbm.at[pl.ds(dst_row, rows), :], zsem)
            cp.wait() if wait else cp.start()

        def per_expert(e, carry):
            off = zoff_ref[e]
            pad = zpad_ref[e]
            lead = pad & (SUBLANES - 1)
            for j in range(SUBLANES - 1):
                @pl.when(j < lead)
                def _(j=j):
                    piece(1, off + j)

            off = off + lead
            for bit in range(SUBLANES.bit_length() - 1, nbits):
                rows = 1 << bit

                @pl.when((pad & rows) != 0)
                def _(rows=rows, off=off):
                    piece(rows, pl.multiple_of(off, SUBLANES))

                off = off + (pad & rows)
            return carry

        lax.fori_loop(0, N_EXPERTS, per_expert, 0)

        def per_block(blk, carry):
            piece(r, pl.multiple_of(blk * r, r))
            return carry

        lax.fori_loop(meta_ref[1], nblk, per_block, 0)

    @pl.when(i == 0)
    def _():
        zbuf[...] = jnp.zeros_like(zbuf)
        zero_fill(wait=False)

    def row_copy(tok, dst_row):
        return pltpu.make_async_copy(x_hbm.at[pl.ds(tok, 1), :],
                                     xs_hbm.at[pl.ds(dst_row, 1), :], sem)

    def issue(t, carry):
        for k in range(TOP_K):
            row_copy(i * tb + t, dest_ref[0, k * tb + t]).start()
        return carry

    lax.fori_loop(0, tb, issue, 0, unroll=8)

    def drain(t, carry):
        for k in range(TOP_K):
            row_copy(0, 0).wait()
        return carry

    lax.fori_loop(0, tb, drain, 0, unroll=8)

    @pl.when(i == 0)
    def _():
        zero_fill(wait=True)


def _scatter_rows(xp, dest_t, zoff, zpad, meta, nblk, tb):
    n, w = xp.shape
    r = MOE_ROW_BLOCK
    nstep = n // tb
    dest_blk = dest_t.reshape(TOP_K, nstep, tb).transpose(1, 0, 2).reshape(nstep, 1, TOP_K * tb)
    kern = functools.partial(_scatter_rows_kernel, tb=tb, nblk=nblk)
    return pl.pallas_call(
        kern,
        grid_spec=pltpu.PrefetchScalarGridSpec(
            num_scalar_prefetch=3,
            grid=(nstep,),
            in_specs=[pl.BlockSpec((None, 1, TOP_K * tb), lambda i, a, b, c: (i, 0, 0),
                                   memory_space=pltpu.SMEM),
                      pl.BlockSpec(memory_space=pl.ANY)],
            out_specs=pl.BlockSpec(memory_space=pl.ANY),
            scratch_shapes=[pltpu.VMEM((r, w), U32), pltpu.SemaphoreType.DMA(()),
                            pltpu.SemaphoreType.DMA(())]),
        out_shape=jax.ShapeDtypeStruct((nblk * r, w), U32),
        compiler_params=_cparams(("arbitrary",)),
        name="moe_scatter",
    )(zoff, zpad, meta, dest_blk, xp)


def _moe_ffn_kernel(sbe_ref, sbrow_ref, sbnsub_ref, meta_ref,
                    xs_hbm, wg_ref, wu_ref, wd_ref, bg_ref, bu_ref, bd_ref, ys_hbm,
                    xraw, acc, wg_bf, wu_bf, wd_bf, in_sem, out_sem, *, nblk):
    s = pl.program_id(0)
    f = pl.program_id(1)
    ns = pl.num_programs(0)
    nf = pl.num_programs(1)
    r = MOE_ROW_BLOCK
    half = xraw.shape[2]
    nsub = sbnsub_ref[s]
    slot = s % 2

    def in_copy(ss, sl, sub):
        src = pl.multiple_of(sbrow_ref[ss] + sub * r, r)
        return pltpu.make_async_copy(xs_hbm.at[pl.ds(src, r), :],
                                     xraw.at[sl, pl.ds(pl.multiple_of(sub * r, r), r), :],
                                     in_sem.at[sl])

    def out_copy(sub):
        row0 = pl.multiple_of(sub * r, r)
        dst = pl.multiple_of(sbrow_ref[s] + sub * r, r)
        return pltpu.make_async_copy(acc.at[pl.ds(row0, r), :], ys_hbm.at[pl.ds(dst, r), :], out_sem)

    def start_load(ss, sl):
        def go(sub, carry):
            in_copy(ss, sl, sub).start()
            return carry
        lax.fori_loop(0, sbnsub_ref[ss], go, 0)

    @pl.when(f == 0)
    def _():
        @pl.when(s == 0)
        def _():
            start_load(0, 0)

        @pl.when(s + 1 < ns)
        def _():
            start_load(s + 1, 1 - slot)

        def land(sub, carry):
            in_copy(s, slot, sub).wait()
            return carry
        lax.fori_loop(0, nsub, land, 0)

    @pl.when(nsub > 0)
    def _():
        wg_bf[...] = wg_ref[...].astype(BF16)
        wu_bf[...] = wu_ref[...].astype(BF16)
        wd_bf[...] = wd_ref[...].astype(BF16)

        def sub_block(sub, carry):
            row0 = pl.multiple_of(sub * r, r)
            x_lo, x_hi = _unpack_bf16_pairs(xraw[slot, pl.ds(row0, r), :])
            hg = _dot(x_lo, wg_bf[0:half, :]) + _dot(x_hi, wg_bf[half:, :]) + bg_ref[...]
            hu = _dot(x_lo, wu_bf[0:half, :]) + _dot(x_hi, wu_bf[half:, :]) + bu_ref[...]
            hg = jnp.minimum(hg, SWIGLU_LIMIT)
            hu = jnp.clip(hu, -SWIGLU_LIMIT, SWIGLU_LIMIT)
            hid = (hu + 1.0) * (hg * jax.nn.sigmoid(SWIGLU_ALPHA * hg))
            part = _dot(hid.astype(BF16), wd_bf[...])

            @pl.when(f == 0)
            def _():
                acc[pl.ds(row0, r), :] = part

            @pl.when((f > 0) & (f < nf - 1))
            def _():
                acc[pl.ds(row0, r), :] += part

            @pl.when(f == nf - 1)
            def _():
                acc[pl.ds(row0, r), :] += part + bd_ref[...]
                out_copy(sub).start()

            return carry

        lax.fori_loop(0, nsub, sub_block, 0)

        @pl.when(f == nf - 1)
        def _():
            def flush(sub, carry):
                out_copy(sub).wait()
                return carry
            lax.fori_loop(0, nsub, flush, 0)

    @pl.when((s == ns - 1) & (f == nf - 1))
    def _():
        acc[0:r, :] = jnp.zeros((r, acc.shape[1]), F32)

        def tail_copy(blk):
            return pltpu.make_async_copy(acc.at[pl.ds(0, r), :],
                                         ys_hbm.at[pl.ds(pl.multiple_of(blk * r, r), r), :], out_sem)

        def go(blk, carry):
            tail_copy(blk).start()
            return carry
        lax.fori_loop(meta_ref[1], nblk, go, 0)

        def done(blk, carry):
            tail_copy(blk).wait()
            return carry
        lax.fori_loop(meta_ref[1], nblk, done, 0)


def _moe_ffn(xs, w_gu, b_gu, w_down, b_down, sbe, sbrow, sbnsub, meta, layer, nblk):
    rows, half = xs.shape
    d = 2 * half
    dff = w_gu.shape[-1] // 2
    tf = MOE_FF_TILE
    nf = dff // tf
    ns = sbe.shape[0]
    bgu = b_gu.reshape(b_gu.shape[0], b_gu.shape[1], 1, 2 * dff)
    bdn = b_down.reshape(b_down.shape[0], b_down.shape[1], 1, d)

    def feff(s, f, meta):
        return jnp.where(s < meta[0], f, nf - 1)

    kern = functools.partial(_moe_ffn_kernel, nblk=nblk)
    return pl.pallas_call(
        kern,
        grid_spec=pltpu.PrefetchScalarGridSpec(
            num_scalar_prefetch=4,
            grid=(ns, nf),
            in_specs=[
                pl.BlockSpec(memory_space=pl.ANY),
                pl.BlockSpec((None, None, d, tf),
                             lambda s, f, se, sr, sn, mt: (layer, se[s], 0, feff(s, f, mt))),
                pl.BlockSpec((None, None, d, tf),
                             lambda s, f, se, sr, sn, mt: (layer, se[s], 0, nf + feff(s, f, mt))),
                pl.BlockSpec((None, None, tf, d),
                             lambda s, f, se, sr, sn, mt: (layer, se[s], feff(s, f, mt), 0)),
                pl.BlockSpec((None, None, 1, tf),
                             lambda s, f, se, sr, sn, mt: (layer, se[s], 0, feff(s, f, mt))),
                pl.BlockSpec((None, None, 1, tf),
                             lambda s, f, se, sr, sn, mt: (layer, se[s], 0, nf + feff(s, f, mt))),
                pl.BlockSpec((None, None, 1, d),
                             lambda s, f, se, sr, sn, mt: (layer, se[s], 0, 0)),
            ],
            out_specs=pl.BlockSpec(memory_space=pl.ANY),
            scratch_shapes=[pltpu.VMEM((2, MOE_SUPER_ROWS, half), U32),
                            pltpu.VMEM((MOE_SUPER_ROWS, d), F32),
                            pltpu.VMEM((d, tf), BF16), pltpu.VMEM((d, tf), BF16),
                            pltpu.VMEM((tf, d), BF16),
                            pltpu.SemaphoreType.DMA((2,)), pltpu.SemaphoreType.DMA(())]),
        out_shape=jax.ShapeDtypeStruct((rows, d), F32),
        compiler_params=_cparams(("arbitrary", "arbitrary")),
        name="moe_ffn",
    )(sbe, sbrow, sbnsub, meta, xs, w_gu, w_gu, w_down, bgu, bgu, bdn)


def _combine_kernel(pos_ref, ys_hbm, x_ref, gate_ref, g_ref, b_ref, o_ref, buf, sem):
    tb = x_ref.shape[0]

    def row_copy(src_row, k, dst_row):
        return pltpu.make_async_copy(ys_hbm.at[pl.ds(src_row, 1), :],
                                     buf.at[k, pl.ds(dst_row, 1), :], sem)

    def issue(r, carry):
        for k in range(TOP_K):
            row_copy(pos_ref[0, r * TOP_K + k], k, r).start()
        return carry

    lax.fori_loop(0, tb, issue, 0, unroll=4)

    def drain(r, carry):
        for k in range(TOP_K):
            row_copy(0, k, r).wait()
        return carry

    lax.fori_loop(0, tb, drain, 0, unroll=4)

    gate = gate_ref[...]
    f = gate[:, 0:1] * buf[0]
    for k in range(1, TOP_K):
        f = f + gate[:, k:k + 1] * buf[k]
    o_ref[...] = _layer_norm_rows(DEEPNORM_ALPHA * x_ref[...] + f, g_ref[...], b_ref[...])


def _combine(ys, pos, gate, x, g, b, tb):
    n, d = x.shape
    nblk = n // tb
    return pl.pallas_call(
        _combine_kernel,
        grid=(nblk,),
        in_specs=[pl.BlockSpec((None, 1, tb * TOP_K), lambda i: (i, 0, 0),
                               memory_space=pltpu.SMEM),
                  pl.BlockSpec(memory_space=pl.ANY),
                  pl.BlockSpec((tb, d), lambda i: (i, 0)),
                  pl.BlockSpec((tb, TOP_K), lambda i: (i, 0)),
                  pl.BlockSpec((1, d), lambda i: (0, 0)),
                  pl.BlockSpec((1, d), lambda i: (0, 0))],
        out_specs=pl.BlockSpec((tb, d), lambda i: (i, 0)),
        out_shape=jax.ShapeDtypeStruct((n, d), F32),
        scratch_shapes=[pltpu.VMEM((TOP_K, tb, d), F32), pltpu.SemaphoreType.DMA(())],
        compiler_params=_cparams(("arbitrary",)),
        name="moe_combine",
    )(pos.reshape(nblk, 1, tb * TOP_K), ys, x, gate, g, b)


def _dispatch_plan(top_e_t, rank_t, cnt):
    n = top_e_t.shape[1]
    r = MOE_ROW_BLOCK
    sb = MOE_SUPER_ROWS
    nblk = -(-(n * TOP_K) // r) + N_EXPERTS
    counts = cnt[:, 0].astype(I32)
    padded = (counts + r - 1) // r * r
    pend = jnp.cumsum(padded)
    pstart = pend - padded
    eids = jnp.arange(N_EXPERTS, dtype=I32)
    base = jnp.sum(jnp.where(top_e_t[:, :, None] == eids[None, None, :],
                             pstart[None, None, :], 0), axis=-1)
    dest_t = base + rank_t
    nused = pend[-1] // r
    nsb = (padded + sb - 1) // sb
    sb_end = jnp.cumsum(nsb)
    sb_start = sb_end - nsb
    total = sb_end[-1]
    n_sb_max = (nblk * r) // sb + N_EXPERTS
    sid = jnp.arange(n_sb_max, dtype=I32)
    used = sid < total
    e_of = jnp.minimum(jnp.sum(sb_end[None, :] <= jnp.minimum(sid, total - 1)[:, None], axis=1),
                       N_EXPERTS - 1).astype(I32)
    idx = jnp.minimum(sid, total - 1) - sb_start[e_of]
    row0 = jnp.where(used, pstart[e_of] + idx * sb, 0).astype(I32)
    nsub = jnp.where(used, jnp.clip((padded[e_of] - idx * sb) // r, 0, sb // r), 0).astype(I32)
    meta = jnp.stack([total, nused]).astype(I32)
    zoff = (pstart + counts).astype(I32)
    zpad = (padded - counts).astype(I32)
    return dest_t.astype(I32), zoff, zpad, meta, e_of, row0, nsub, nblk


def _moe_and_norm(x1, x1p, top_e_t, gate_t, rank_t, cnt, w_gu, b_gu, w_down, b_down, g, b, layer):
    dest_t, zoff, zpad, meta, sbe, sbrow, sbnsub, nblk = _dispatch_plan(top_e_t, rank_t, cnt)
    xs = _scatter_rows(x1p, dest_t, zoff, zpad, meta, nblk, tb=512)
    ys = _moe_ffn(xs, w_gu, b_gu, w_down, b_down, sbe, sbrow, sbnsub, meta, layer, nblk)
    return _combine(ys, dest_t.T, gate_t.T, x1, g, b, tb=128)


def _rotary_tables(s):
    half = ROPE_DIM // 2
    inv_freq = ROPE_THETA ** (-jnp.arange(half, dtype=F32) * 2.0 / ROPE_DIM)
    ang = jnp.arange(s, dtype=F32)[:, None] * inv_freq[None, :]
    cos, sin = jnp.cos(ang), jnp.sin(ang)
    rest = HEAD_DIM - ROPE_DIM
    cos_t = jnp.concatenate([cos, cos, jnp.ones((s, rest), F32)], axis=1)
    s1_t = jnp.concatenate([-sin, jnp.zeros((s, half + rest), F32)], axis=1)
    s2_t = jnp.concatenate([jnp.zeros((s, half), F32), sin, jnp.zeros((s, rest), F32)], axis=1)
    return cos_t, s1_t, s2_t


def kernel(x, mem, mlstm_w_in, mlstm_conv_w, mlstm_conv_b, mlstm_b_igate, mlstm_b_fgate, moba_w_in, w_mem_kv, w_out, ln1_g, ln1_b, w_router, b_router, w_gu, b_gu, w_down, b_down, ln2_g, ln2_b):
    bsz, s, d = x.shape
    n = bsz * s
    n_mem = mem.shape[1]
    mix_w = d - MEM_WIDTH
    nh_ml = MLSTM_HEADS
    dv = mix_w // nh_ml
    dqk = dv // 2
    qkw2 = 2 * nh_ml * dqk
    nh_mb = mix_w // HEAD_DIM
    qm_block = 3 * mix_w // MEM_WIDTH
    assert qkw2 == mix_w and 3 * mix_w % MEM_WIDTH == 0

    xf = x.reshape(n, d)
    memf = mem.reshape(bsz * n_mem, d)
    for i in range(DEPTH):
        j = i // 2
        if i % 2 == 0:
            w_in = mlstm_w_in[j]
            gate_lo = qkw2 + 2 * mix_w
            gate_hi = gate_lo + 2 * nh_ml
            w_main = jnp.concatenate([w_in[:, :gate_lo], w_in[:, gate_hi:]], axis=1).astype(BF16)
            w_gate = jnp.pad(w_in[:, gate_lo:gate_hi], ((0, 0), (0, LANES - 2 * nh_ml)))
            z = _matmul(xf, w_main, F32, tm=1024, tn=512).reshape(bsz, s, -1)
            gates = _matmul3(xf, w_gate, tm=1024).reshape(bsz, s, LANES)
            gates_t = jnp.swapaxes(gates[:, :, :2 * SUBLANES], 1, 2)
            bias = jnp.concatenate([mlstm_b_igate[j], mlstm_b_fgate[j]])
            bias_row = jnp.pad(bias, (0, LANES - 2 * nh_ml)).reshape(1, LANES)
            bias_col = jnp.pad(bias, (0, 2 * SUBLANES - 2 * nh_ml)).reshape(2 * SUBLANES, 1)
            h_mix = _mlstm(z, gates, gates_t, mlstm_conv_w[j], mlstm_conv_b[j].reshape(1, -1),
                           bias_row, bias_col, nh=nh_ml, dqk=dqk, dv=dv, chunk=MLSTM_CHUNK)
        else:
            z = _matmul(xf, moba_w_in[j].astype(BF16), F32, tm=1024, tn=512).reshape(bsz, s, -1)
            cos_t, s1_t, s2_t = _rotary_tables(s)
            q_r, k_r, v_b, kbar = _moba_prep(z, cos_t, s1_t, s2_t, nh_mb)
            h_mix = _moba_attn(q_r, k_r, v_b, kbar.reshape(bsz, s // MOBA_BLOCK, mix_w), nh_mb)
        kv = _matmul(memf, w_mem_kv[i].astype(BF16), BF16, tm=1024, tn=512)
        h_mem = _mem_attn(z, kv.reshape(bsz, n_mem, 2 * MEM_WIDTH), qm_block, tq=512)
        w_o = w_out[i].astype(BF16)
        x1, x1p, top_e_t, gate_t, rank_t, cnt = _outproj(
            h_mix.reshape(n, mix_w), h_mem.reshape(n, MEM_WIDTH), w_o[:mix_w], w_o[mix_w:], xf,
            ln1_g[i].reshape(1, d), ln1_b[i].reshape(1, d),
            w_router[i].T, b_router[i].reshape(N_EXPERTS, 1), tm=512)
        xf = _moe_and_norm(x1, x1p, top_e_t, gate_t, rank_t, cnt, w_gu, b_gu, w_down, b_down,
                           ln2_g[i].reshape(1, d), ln2_b[i].reshape(1, d), i)
    return xf.reshape(bsz, s, d)
```

```python
import functools

import jax
import jax.numpy as jnp
from jax import lax
from jax.experimental import pallas as pl
from jax.experimental.pallas import tpu as pltpu

F32 = jnp.float32
BF16 = jnp.bfloat16
U32 = jnp.uint32
I32 = jnp.int32

HEAD_DIM = 128
MEM_HEADS = 4
MEM_WIDTH = MEM_HEADS * HEAD_DIM
MLSTM_HEADS = 6
CONV_WIDTH = 4
MOBA_BLOCK = 256
MOBA_TOPK = 3
ROPE_THETA = 500000.0
ROPE_DIM = HEAD_DIM // 4
N_EXPERTS = 32
TOP_K = 4
SWIGLU_LIMIT = 7.0
SWIGLU_ALPHA = 1.702
LN_EPS = 1e-5
DEPTH = 2
DEEPNORM_ALPHA = (2 * DEPTH) ** 0.25

LANES = 128
SUBLANES = 8
VMEM_LIMIT_BYTES = 56 * 1024 * 1024
MLSTM_CHUNK = 128
MOE_ROW_BLOCK = 256
MOE_SUPER_ROWS = 1024
MOE_FF_TILE = 512
NEG_BIG = -1e30


def _cparams(sem):
    return pltpu.CompilerParams(dimension_semantics=sem, vmem_limit_bytes=VMEM_LIMIT_BYTES)


def _dot(a, b):
    return jnp.dot(a, b, preferred_element_type=F32)


def _dot_nt(a, b):
    return lax.dot_general(a, b, (((1,), (1,)), ((), ())), preferred_element_type=F32)


def _dot_tn(a, b):
    return lax.dot_general(a, b, (((0,), (0,)), ((), ())), preferred_element_type=F32)


def _split_bf16(x):
    hi = x.astype(BF16)
    lo = (x - hi.astype(F32)).astype(BF16)
    return hi, lo


def _layer_norm_rows(r, g, b):
    mu = jnp.mean(r, axis=-1, keepdims=True)
    d = r - mu
    var = jnp.mean(d * d, axis=-1, keepdims=True)
    return d * lax.rsqrt(var + LN_EPS) * g + b


def _pack_bf16_pairs(x):
    w = x.shape[1] // 2
    lo = pltpu.bitcast(x[:, :w].astype(BF16).astype(F32), U32)
    hi = pltpu.bitcast(x[:, w:].astype(BF16).astype(F32), U32)
    return (hi & jnp.uint32(0xFFFF0000)) | (lo >> 16)


def _unpack_bf16_pairs(p):
    lo = pltpu.bitcast(p << 16, F32).astype(BF16)
    hi = pltpu.bitcast(p & jnp.uint32(0xFFFF0000), F32).astype(BF16)
    return lo, hi


def _matmul_kernel(a_ref, b_ref, o_ref, a_bf_ref):
    @pl.when(pl.program_id(1) == 0)
    def _():
        a_bf_ref[...] = a_ref[...].astype(BF16)

    o_ref[...] = _dot(a_bf_ref[...], b_ref[...]).astype(o_ref.dtype)


def _matmul(a, b, out_dtype, tm, tn):
    m, k = a.shape
    n = b.shape[1]
    return pl.pallas_call(
        _matmul_kernel,
        grid=(m // tm, n // tn),
        in_specs=[pl.BlockSpec((tm, k), lambda i, j: (i, 0)),
                  pl.BlockSpec((k, tn), lambda i, j: (0, j))],
        out_specs=pl.BlockSpec((tm, tn), lambda i, j: (i, j)),
        out_shape=jax.ShapeDtypeStruct((m, n), out_dtype),
        scratch_shapes=[pltpu.VMEM((tm, k), BF16)],
        compiler_params=_cparams(("parallel", "arbitrary")),
        name="matmul",
    )(a, b)


def _matmul3_kernel(a_ref, b_ref, o_ref):
    a_hi, a_lo = _split_bf16(a_ref[...])
    b_hi, b_lo = _split_bf16(b_ref[...])
    o_ref[...] = _dot(a_hi, b_hi) + _dot(a_hi, b_lo) + _dot(a_lo, b_hi)


def _matmul3(a, b, tm):
    m, k = a.shape
    n = b.shape[1]
    return pl.pallas_call(
        _matmul3_kernel,
        grid=(m // tm,),
        in_specs=[pl.BlockSpec((tm, k), lambda i: (i, 0)),
                  pl.BlockSpec((k, n), lambda i: (0, 0))],
        out_specs=pl.BlockSpec((tm, n), lambda i: (i, 0)),
        out_shape=jax.ShapeDtypeStruct((m, n), F32),
        compiler_params=_cparams(("parallel",)),
        name="matmul3",
    )(a, b)


def _log_sigmoid(x):
    return jnp.minimum(x, 0.0) - jnp.log1p(jnp.exp(-jnp.abs(x)))


def _mlstm_kernel(qk_ref, v_ref, og_ref, g_ref, gt_ref, cw_ref, cb_ref, bias_ref, biast_ref,
                  out_ref, ext_ref, c_ref, n_ref, m_ref, *, chunk, nh, dqk, dv):
    L = chunk
    qkw = nh * dqk

    @pl.when(pl.program_id(1) == 0)
    def _():
        ext_ref[0:SUBLANES, :] = jnp.zeros((SUBLANES, 2 * qkw), F32)
        c_ref[...] = jnp.zeros_like(c_ref)
        n_ref[...] = jnp.zeros_like(n_ref)
        m_ref[...] = jnp.zeros_like(m_ref)

    ext_ref[SUBLANES:SUBLANES + L, :] = qk_ref[...]
    cw = cw_ref[...]
    y = jnp.broadcast_to(cb_ref[...], (L, 2 * qkw))
    for w in range(CONV_WIDTH):
        y = y + ext_ref[pl.ds(SUBLANES - (CONV_WIDTH - 1) + w, L), :] * cw[w:w + 1, :]
    ext_ref[0:SUBLANES, :] = ext_ref[L:L + SUBLANES, :]
    qk = y * jax.nn.sigmoid(y)

    gates = g_ref[...] + bias_ref[...]
    gates_t = gt_ref[...] + biast_ref[...]
    row = lax.broadcasted_iota(jnp.int32, (L, L), 0)
    col = lax.broadcasted_iota(jnp.int32, (L, L), 1)
    causal = col <= row

    for h in range(nh):
        q = qk[:, h * dqk:(h + 1) * dqk]
        k = qk[:, qkw + h * dqk:qkw + (h + 1) * dqk] * (dqk ** -0.5)
        v_bf = v_ref[:, h * dv:(h + 1) * dv].astype(BF16)
        q_bf = q.astype(BF16)
        ig_c = gates[:, h:h + 1]
        ig_r = gates_t[h:h + 1, :]
        lf_c = _log_sigmoid(gates[:, nh + h:nh + h + 1])
        lf_r = _log_sigmoid(gates_t[nh + h:nh + h + 1, :])
        bcum_c = jnp.sum(jnp.where(causal, lf_r, 0.0), axis=1, keepdims=True)
        bcum_r = jnp.sum(jnp.where(row <= col, lf_c, 0.0), axis=0, keepdims=True)
        g_tot = jnp.sum(lf_r, axis=1, keepdims=True)
        m_prev = m_ref[h:h + 1, 0:1]

        dmat = jnp.where(causal, bcum_c - bcum_r + ig_r, -jnp.inf)
        inter_log = bcum_c + m_prev
        m_t = jnp.maximum(inter_log, jnp.max(dmat, axis=1, keepdims=True))
        s_qk = _dot_nt(q_bf, k.astype(BF16)) * jnp.exp(dmat - m_t)
        inter_w = jnp.exp(inter_log - m_t)
        c_prev = c_ref[h]
        n_prev = n_ref[h:h + 1, :]
        num = inter_w * _dot(q_bf, c_prev.astype(BF16)) + _dot(s_qk.astype(BF16), v_bf)
        den = inter_w * jnp.sum(q * n_prev, axis=1, keepdims=True) \
            + jnp.sum(s_qk, axis=1, keepdims=True)
        hval = num / jnp.maximum(jnp.abs(den), jnp.exp(-m_t))
        o_gate = jax.nn.sigmoid(og_ref[:, h * dv:(h + 1) * dv])
        out_ref[:, h * dv:(h + 1) * dv] = (o_gate * hval).astype(out_ref.dtype)

        a_c = g_tot - bcum_c + ig_c
        m_new = jnp.maximum(g_tot + m_prev, jnp.max(a_c, axis=0, keepdims=True))
        decay = jnp.exp(g_tot + m_prev - m_new)
        kw = k * jnp.exp(a_c - m_new)
        c_ref[h] = decay * c_prev + _dot_tn(kw.astype(BF16), v_bf)
        n_ref[h:h + 1, :] = decay * n_prev + jnp.sum(kw, axis=0, keepdims=True)
        m_ref[h:h + 1, :] = jnp.broadcast_to(m_new, (1, LANES))


def _mlstm(z, gates, gates_t, conv_w, conv_b, bias_row, bias_col, *, nh, dqk, dv, chunk):
    bsz, s, _ = z.shape
    qkw2 = 2 * nh * dqk
    vw = nh * dv
    assert qkw2 == vw, "q|k, v and output-gate column blocks must share one block width"
    nc = s // chunk
    kern = functools.partial(_mlstm_kernel, chunk=chunk, nh=nh, dqk=dqk, dv=dv)
    return pl.pallas_call(
        kern,
        grid=(bsz, nc),
        in_specs=[
            pl.BlockSpec((None, chunk, vw), lambda b, c: (b, c, 0)),
            pl.BlockSpec((None, chunk, vw), lambda b, c: (b, c, 1)),
            pl.BlockSpec((None, chunk, vw), lambda b, c: (b, c, 2)),
            pl.BlockSpec((None, chunk, LANES), lambda b, c: (b, c, 0)),
            pl.BlockSpec((None, 2 * SUBLANES, chunk), lambda b, c: (b, 0, c)),
            pl.BlockSpec((CONV_WIDTH, qkw2), lambda b, c: (0, 0)),
            pl.BlockSpec((1, qkw2), lambda b, c: (0, 0)),
            pl.BlockSpec((1, LANES), lambda b, c: (0, 0)),
            pl.BlockSpec((2 * SUBLANES, 1), lambda b, c: (0, 0)),
        ],
        out_specs=pl.BlockSpec((None, chunk, vw), lambda b, c: (b, c, 0)),
        out_shape=jax.ShapeDtypeStruct((bsz, s, vw), BF16),
        scratch_shapes=[
            pltpu.VMEM((chunk + SUBLANES, qkw2), F32),
            pltpu.VMEM((nh, dqk, dv), F32),
            pltpu.VMEM((SUBLANES, dqk), F32),
            pltpu.VMEM((SUBLANES, LANES), F32),
        ],
        compiler_params=_cparams(("parallel", "arbitrary")),
        name="mlstm",
    )(z, z, z, gates, gates_t, conv_w, conv_b, bias_row, bias_col)


def _mem_attn_kernel(q_ref, k_ref, v_ref, o_ref):
    scale = HEAD_DIM ** -0.5
    for h in range(MEM_HEADS):
        sl = slice(h * HEAD_DIM, (h + 1) * HEAD_DIM)
        q = q_ref[:, sl].astype(BF16)
        sc = _dot_nt(q, k_ref[:, sl]) * scale
        sc = sc - jnp.max(sc, axis=1, keepdims=True)
        p = jnp.exp(sc)
        p = p / jnp.sum(p, axis=1, keepdims=True)
        o_ref[:, sl] = _dot(p.astype(BF16), v_ref[:, sl]).astype(o_ref.dtype)


def _mem_attn(z, kv, qm_block, tq):
    bsz, s, _ = z.shape
    n_mem = kv.shape[1]
    return pl.pallas_call(
        _mem_attn_kernel,
        grid=(bsz, s // tq),
        in_specs=[
            pl.BlockSpec((None, tq, MEM_WIDTH), lambda b, i: (b, i, qm_block)),
            pl.BlockSpec((None, n_mem, MEM_WIDTH), lambda b, i: (b, 0, 0)),
            pl.BlockSpec((None, n_mem, MEM_WIDTH), lambda b, i: (b, 0, 1)),
        ],
        out_specs=pl.BlockSpec((None, tq, MEM_WIDTH), lambda b, i: (b, i, 0)),
        out_shape=jax.ShapeDtypeStruct((bsz, s, MEM_WIDTH), BF16),
        compiler_params=_cparams(("parallel", "parallel")),
        name="mem_attn",
    )(z, kv, kv)


def _moba_prep_kernel(q_ref, k_ref, v_ref, cos_ref, s1_ref, s2_ref,
                      qo_ref, ko_ref, vo_ref, kbar_ref, *, nh):
    cosf = cos_ref[...]
    s1 = s1_ref[...]
    s2 = s2_ref[...]
    half = ROPE_DIM // 2
    for h in range(nh):
        sl = slice(h * HEAD_DIM, (h + 1) * HEAD_DIM)
        for src, dst, is_k in ((q_ref, qo_ref, False), (k_ref, ko_ref, True)):
            x = src[:, sl]
            xr = (x * cosf + pltpu.roll(x, HEAD_DIM - half, 1) * s1
                  + pltpu.roll(x, half, 1) * s2)
            if is_k:
                dst[:, sl] = xr.astype(dst.dtype)
                kbar_ref[:, sl] = jnp.mean(xr, axis=0, keepdims=True)
            else:
                dst[:, sl] = (xr * (HEAD_DIM ** -0.5)).astype(dst.dtype)
    vo_ref[...] = v_ref[...].astype(vo_ref.dtype)


def _moba_prep(z, cos_t, s1_t, s2_t, nh):
    bsz, s, _ = z.shape
    w = nh * HEAD_DIM
    nb = s // MOBA_BLOCK
    t = MOBA_BLOCK
    kern = functools.partial(_moba_prep_kernel, nh=nh)
    tab = pl.BlockSpec((t, HEAD_DIM), lambda b, i: (i, 0))
    big = jax.ShapeDtypeStruct((bsz, s, w), BF16)
    return pl.pallas_call(
        kern,
        grid=(bsz, nb),
        in_specs=[pl.BlockSpec((None, t, w), lambda b, i: (b, i, 0)),
                  pl.BlockSpec((None, t, w), lambda b, i: (b, i, 1)),
                  pl.BlockSpec((None, t, w), lambda b, i: (b, i, 2)),
                  tab, tab, tab],
        out_specs=[pl.BlockSpec((None, t, w), lambda b, i: (b, i, 0)),
                   pl.BlockSpec((None, t, w), lambda b, i: (b, i, 0)),
                   pl.BlockSpec((None, t, w), lambda b, i: (b, i, 0)),
                   pl.BlockSpec((None, None, 1, w), lambda b, i: (b, i, 0, 0))],
        out_shape=[big, big, big, jax.ShapeDtypeStruct((bsz, nb, 1, w), F32)],
        compiler_params=_cparams(("parallel", "parallel")),
        name="moba_prep",
    )(z, z, z, cos_t, s1_t, s2_t)


def _moba_attn_kernel(q_ref, k_ref, v_ref, kbar_ref, o_ref, *, nb):
    t = MOBA_BLOCK
    qb = pl.program_id(2)
    q = q_ref[...]

    kb_hi, kb_lo = _split_bf16(kbar_ref[...])
    gate = _dot_nt(kb_hi, q) + _dot_nt(kb_lo, q)
    blk = lax.broadcasted_iota(jnp.int32, (nb, t), 0)
    valid = blk < qb
    gm = jnp.where(valid, gate, -jnp.inf)
    rank = jnp.zeros((nb, t), jnp.int32)
    for n2 in range(nb):
        gc = gm[n2:n2 + 1, :]
        beats = (gc > gm) | ((gc == gm) & (n2 < blk))
        rank = rank + beats.astype(jnp.int32)
    bias_t = jnp.where(valid & (rank < MOBA_TOPK), 0.0, NEG_BIG)
    pad_rows = 2 * SUBLANES - nb
    bias_t = jnp.concatenate([bias_t, jnp.zeros((pad_rows, t), F32)], axis=0).astype(BF16)
    eye = jnp.where(lax.broadcasted_iota(jnp.int32, (nb + pad_rows, LANES), 0)
                    == lax.broadcasted_iota(jnp.int32, (nb + pad_rows, LANES), 1), 1.0, 0.0)
    sel_bias = _dot_tn(bias_t, eye.astype(BF16))

    row = lax.broadcasted_iota(jnp.int32, (t, t), 0)
    col = lax.broadcasted_iota(jnp.int32, (t, t), 1)
    causal = col <= row

    for c in range(nb):
        @pl.when(qb == c)
        def _(c=c):
            pieces = []
            for n in range(c + 1):
                s = _dot_nt(q, k_ref[n * t:(n + 1) * t, :])
                if n == c:
                    pieces.append(jnp.where(causal, s, NEG_BIG))
                else:
                    pieces.append(s + sel_bias[:, n:n + 1])
            mx = pieces[0]
            for piece in pieces[1:]:
                mx = jnp.maximum(mx, piece)
            m = jnp.max(mx, axis=1, keepdims=True)
            psum = None
            acc = None
            for n in range(c + 1):
                p = jnp.exp(pieces[n] - m)
                pv = _dot(p.astype(BF16), v_ref[n * t:(n + 1) * t, :])
                psum = p if psum is None else psum + p
                acc = pv if acc is None else acc + pv
            l = jnp.sum(psum, axis=1, keepdims=True)
            o_ref[...] = (acc / l).astype(o_ref.dtype)


def _moba_attn(q, k, v, kbar, nh):
    bsz, s, _ = q.shape
    nb = s // MOBA_BLOCK
    t = MOBA_BLOCK
    kern = functools.partial(_moba_attn_kernel, nb=nb)
    return pl.pallas_call(
        kern,
        grid=(bsz, nh, nb),
        in_specs=[pl.BlockSpec((None, t, HEAD_DIM), lambda b, h, i: (b, i, h)),
                  pl.BlockSpec((None, s, HEAD_DIM), lambda b, h, i: (b, 0, h)),
                  pl.BlockSpec((None, s, HEAD_DIM), lambda b, h, i: (b, 0, h)),
                  pl.BlockSpec((None, nb, HEAD_DIM), lambda b, h, i: (b, 0, h))],
        out_specs=pl.BlockSpec((None, t, HEAD_DIM), lambda b, h, i: (b, i, h)),
        out_shape=jax.ShapeDtypeStruct((bsz, s, nh * HEAD_DIM), BF16),
        compiler_params=_cparams(("parallel", "parallel", "arbitrary")),
        name="moba_attn",
    )(q, k, v, kbar)


def _outproj_kernel(hmix_ref, hmem_ref, w1_ref, w2_ref, x_ref, g_ref, b_ref, wr_ref, br_ref,
                    x1_ref, x1p_ref, tope_ref, gate_ref, rank_ref, cnt_ref, tri_ref, run_ref):
    i = pl.program_id(0)
    y = _dot(hmix_ref[...], w1_ref[...]) + _dot(hmem_ref[...], w2_ref[...])
    x1 = _layer_norm_rows(DEEPNORM_ALPHA * x_ref[...] + y, g_ref[...], b_ref[...])
    x1_ref[...] = x1
    x1p_ref[...] = _pack_bf16_pairs(x1)

    x_hi, x_lo = _split_bf16(x1)
    w_hi, w_lo = _split_bf16(wr_ref[...])
    logits = _dot_nt(w_hi, x_hi) + _dot_nt(w_hi, x_lo) + _dot_nt(w_lo, x_hi) + br_ref[...]
    ne, tm = logits.shape

    @pl.when(i == 0)
    def _():
        r = lax.broadcasted_iota(jnp.int32, (tm, tm), 0)
        c = lax.broadcasted_iota(jnp.int32, (tm, tm), 1)
        tri_ref[...] = jnp.where(r <= c, 1.0, 0.0).astype(BF16)
        run_ref[...] = jnp.zeros_like(run_ref)

    eid = lax.broadcasted_iota(jnp.int32, (ne, tm), 0)
    run = run_ref[...]
    vals = []
    for k in range(TOP_K):
        mx = jnp.max(logits, axis=0, keepdims=True)
        idx = jnp.min(jnp.where(logits == mx, eid, ne), axis=0, keepdims=True)
        tope_ref[k:k + 1, :] = idx
        vals.append(mx)
        hit = eid == idx
        logits = jnp.where(hit, -jnp.inf, logits)
        incl = _dot(jnp.where(hit, 1.0, 0.0).astype(BF16), tri_ref[...])
        rank = jnp.sum(jnp.where(hit, run + incl - 1.0, 0.0), axis=0, keepdims=True)
        rank_ref[k:k + 1, :] = rank.astype(jnp.int32)
        run = run + incl[:, tm - 1:tm]
    run_ref[...] = run
    cnt_ref[...] = jnp.broadcast_to(run, cnt_ref.shape)
    ex = [jnp.exp(vk - vals[0]) for vk in vals]
    tot = ex[0] + ex[1] + ex[2] + ex[3]
    for k in range(TOP_K):
        gate_ref[k:k + 1, :] = ex[k] / tot


def _outproj(hmix, hmem, w1, w2, x, g, b, wr_t, br_col, tm):
    m, d = x.shape
    ne = wr_t.shape[0]
    return pl.pallas_call(
        _outproj_kernel,
        grid=(m // tm,),
        in_specs=[pl.BlockSpec((tm, hmix.shape[1]), lambda i: (i, 0)),
                  pl.BlockSpec((tm, hmem.shape[1]), lambda i: (i, 0)),
                  pl.BlockSpec(w1.shape, lambda i: (0, 0)),
                  pl.BlockSpec(w2.shape, lambda i: (0, 0)),
                  pl.BlockSpec((tm, d), lambda i: (i, 0)),
                  pl.BlockSpec((1, d), lambda i: (0, 0)),
                  pl.BlockSpec((1, d), lambda i: (0, 0)),
                  pl.BlockSpec(wr_t.shape, lambda i: (0, 0)),
                  pl.BlockSpec(br_col.shape, lambda i: (0, 0))],
        out_specs=[pl.BlockSpec((tm, d), lambda i: (i, 0)),
                   pl.BlockSpec((tm, d // 2), lambda i: (i, 0)),
                   pl.BlockSpec((TOP_K, tm), lambda i: (0, i)),
                   pl.BlockSpec((TOP_K, tm), lambda i: (0, i)),
                   pl.BlockSpec((TOP_K, tm), lambda i: (0, i)),
                   pl.BlockSpec((ne, LANES), lambda i: (0, 0))],
        out_shape=[jax.ShapeDtypeStruct((m, d), F32),
                   jax.ShapeDtypeStruct((m, d // 2), U32),
                   jax.ShapeDtypeStruct((TOP_K, m), jnp.int32),
                   jax.ShapeDtypeStruct((TOP_K, m), F32),
                   jax.ShapeDtypeStruct((TOP_K, m), jnp.int32),
                   jax.ShapeDtypeStruct((ne, LANES), F32)],
        scratch_shapes=[pltpu.VMEM((tm, tm), BF16), pltpu.VMEM((ne, 1), F32)],
        compiler_params=_cparams(("arbitrary",)),
        name="outproj_ln_router",
    )(hmix, hmem, w1, w2, x, g, b, wr_t, br_col)


INVERT_UNROLL = 16


def _invert_kernel(trips_ref, dest_ref, tok_ref, *, n):
    def clear(i, carry):
        for u in range(INVERT_UNROLL):
            tok_ref[i * INVERT_UNROLL + u] = 0
        return carry

    lax.fori_loop(0, trips_ref[0], clear, 0)
    for k in range(TOP_K):
        def put(i, carry, k=k):
            for u in range(INVERT_UNROLL):
                t = i * INVERT_UNROLL + u
                tok_ref[dest_ref[k * n + t]] = t
            return carry

        lax.fori_loop(0, trips_ref[1], put, 0)


def _invert(dest_flat, rows, n):
    assert rows % INVERT_UNROLL == 0 and n % INVERT_UNROLL == 0
    trips = jnp.array([rows // INVERT_UNROLL, n // INVERT_UNROLL], I32)
    return pl.pallas_call(
        functools.partial(_invert_kernel, n=n),
        in_specs=[pl.BlockSpec(memory_space=pltpu.SMEM), pl.BlockSpec(memory_space=pltpu.SMEM)],
        out_specs=pl.BlockSpec(memory_space=pltpu.SMEM),
        out_shape=jax.ShapeDtypeStruct((rows,), I32),
        name="moe_invert",
    )(trips, dest_flat)


def _gather_rows_kernel(tok_ref, meta_ref, x_hbm, o_ref, buf, sem):
    i = pl.program_id(0)
    r = buf.shape[1]
    nused = meta_ref[1]
    slot = i % 2

    def row_copy(src_row, sl, dst_row):
        return pltpu.make_async_copy(x_hbm.at[pl.ds(src_row, 1), :],
                                     buf.at[sl, pl.ds(dst_row, 1), :], sem.at[sl])

    def request(blk, sl):
        def go(j, carry):
            row_copy(tok_ref[blk * r + j], sl, j).start()
            return carry
        lax.fori_loop(0, r, go, 0, unroll=8)

    @pl.when((i == 0) & (nused > 0))
    def _():
        request(0, 0)

    @pl.when(i + 1 < nused)
    def _():
        request(i + 1, 1 - slot)

    @pl.when(i < nused)
    def _():
        def land(j, carry):
            row_copy(0, slot, j).wait()
            return carry
        lax.fori_loop(0, r, land, 0, unroll=8)
        o_ref[...] = buf[slot]

    @pl.when(i >= nused)
    def _():
        o_ref[...] = jnp.zeros_like(o_ref)


def _gather_rows(xp, row_tok, meta, nblk):
    w = xp.shape[1]
    r = MOE_ROW_BLOCK
    return pl.pallas_call(
        _gather_rows_kernel,
        grid_spec=pltpu.PrefetchScalarGridSpec(
            num_scalar_prefetch=2,
            grid=(nblk,),
            in_specs=[pl.BlockSpec(memory_space=pl.ANY)],
            out_specs=pl.BlockSpec((r, w), lambda i, tk, mt: (i, 0)),
            scratch_shapes=[pltpu.VMEM((2, r, w), U32), pltpu.SemaphoreType.DMA((2,))]),
        out_shape=jax.ShapeDtypeStruct((nblk * r, w), U32),
        compiler_params=_cparams(("arbitrary",)),
        name="moe_gather",
    )(row_tok, meta, xp)


def _moe_ffn_kernel(sbe_ref, sbrow_ref, sbnsub_ref, meta_ref,
                    xs_hbm, wg_ref, wu_ref, wd_ref, bg_ref, bu_ref, bd_ref, ys_hbm,
                    xraw, acc, wg_bf, wu_bf, wd_bf, in_sem, out_sem, *, nblk):
    s = pl.program_id(0)
    f = pl.program_id(1)
    ns = pl.num_programs(0)
    nf = pl.num_programs(1)
    r = MOE_ROW_BLOCK
    half = xraw.shape[2]
    nsub = sbnsub_ref[s]
    slot = s % 2

    def in_copy(ss, sl, sub):
        src = pl.multiple_of(sbrow_ref[ss] + sub * r, r)
        return pltpu.make_async_copy(xs_hbm.at[pl.ds(src, r), :],
                                     xraw.at[sl, pl.ds(pl.multiple_of(sub * r, r), r), :],
                                     in_sem.at[sl])

    def out_copy(sub):
        row0 = pl.multiple_of(sub * r, r)
        dst = pl.multiple_of(sbrow_ref[s] + sub * r, r)
        return pltpu.make_async_copy(acc.at[pl.ds(row0, r), :], ys_hbm.at[pl.ds(dst, r), :], out_sem)

    def start_load(ss, sl):
        def go(sub, carry):
            in_copy(ss, sl, sub).start()
            return carry
        lax.fori_loop(0, sbnsub_ref[ss], go, 0)

    @pl.when(f == 0)
    def _():
        @pl.when(s == 0)
        def _():
            start_load(0, 0)

        @pl.when(s + 1 < ns)
        def _():
            start_load(s + 1, 1 - slot)

        def land(sub, carry):
            in_copy(s, slot, sub).wait()
            return carry
        lax.fori_loop(0, nsub, land, 0)

    @pl.when(nsub > 0)
    def _():
        wg_bf[...] = wg_ref[...].astype(BF16)
        wu_bf[...] = wu_ref[...].astype(BF16)
        wd_bf[...] = wd_ref[...].astype(BF16)

        def sub_block(sub, carry):
            row0 = pl.multiple_of(sub * r, r)
            x_lo, x_hi = _unpack_bf16_pairs(xraw[slot, pl.ds(row0, r), :])
            hg = _dot(x_lo, wg_bf[0:half, :]) + _dot(x_hi, wg_bf[half:, :]) + bg_ref[...]
            hu = _dot(x_lo, wu_bf[0:half, :]) + _dot(x_hi, wu_bf[half:, :]) + bu_ref[...]
            hg = jnp.minimum(hg, SWIGLU_LIMIT)
            hu = jnp.clip(hu, -SWIGLU_LIMIT, SWIGLU_LIMIT)
            hid = (hu + 1.0) * (hg * jax.nn.sigmoid(SWIGLU_ALPHA * hg))
            part = _dot(hid.astype(BF16), wd_bf[...])

            @pl.when(f == 0)
            def _():
                acc[pl.ds(row0, r), :] = part

            @pl.when((f > 0) & (f < nf - 1))
            def _():
                acc[pl.ds(row0, r), :] += part

            @pl.when(f == nf - 1)
            def _():
                acc[pl.ds(row0, r), :] += part + bd_ref[...]
                out_copy(sub).start()

            return carry

        lax.fori_loop(0, nsub, sub_block, 0)

        @pl.when(f == nf - 1)
        def _():
            def flush(sub, carry):
                out_copy(sub).wait()
                return carry
            lax.fori_loop(0, nsub, flush, 0)

    @pl.when((s == ns - 1) & (f == nf - 1))
    def _():
        acc[0:r, :] = jnp.zeros((r, acc.shape[1]), F32)

        def tail_copy(blk):
            return pltpu.make_async_copy(acc.at[pl.ds(0, r), :],
                                         ys_hbm.at[pl.ds(pl.multiple_of(blk * r, r), r), :], out_sem)

        def go(blk, carry):
            tail_copy(blk).start()
            return carry
        lax.fori_loop(meta_ref[1], nblk, go, 0)

        def done(blk, carry):
            tail_copy(blk).wait()
            return carry
        lax.fori_loop(meta_ref[1], nblk, done, 0)


def _moe_ffn(xs, w_gu, b_gu, w_down, b_down, sbe, sbrow, sbnsub, meta, layer, nblk):
    rows, half = xs.shape
    d = 2 * half
    dff = w_gu.shape[-1] // 2
    tf = MOE_FF_TILE
    nf = dff // tf
    ns = sbe.shape[0]
    bgu = b_gu.reshape(b_gu.shape[0], b_gu.shape[1], 1, 2 * dff)
    bdn = b_down.reshape(b_down.shape[0], b_down.shape[1], 1, d)

    def feff(s, f, meta):
        return jnp.where(s < meta[0], f, nf - 1)

    kern = functools.partial(_moe_ffn_kernel, nblk=nblk)
    return pl.pallas_call(
        kern,
        grid_spec=pltpu.PrefetchScalarGridSpec(
            num_scalar_prefetch=4,
            grid=(ns, nf),
            in_specs=[
                pl.BlockSpec(memory_space=pl.ANY),
                pl.BlockSpec((None, None, d, tf),
                             lambda s, f, se, sr, sn, mt: (layer, se[s], 0, feff(s, f, mt))),
                pl.BlockSpec((None, None, d, tf),
                             lambda s, f, se, sr, sn, mt: (layer, se[s], 0, nf + feff(s, f, mt))),
                pl.BlockSpec((None, None, tf, d),
                             lambda s, f, se, sr, sn, mt: (layer, se[s], feff(s, f, mt), 0)),
                pl.BlockSpec((None, None, 1, tf),
                             lambda s, f, se, sr, sn, mt: (layer, se[s], 0, feff(s, f, mt))),
                pl.BlockSpec((None, None, 1, tf),
                             lambda s, f, se, sr, sn, mt: (layer, se[s], 0, nf + feff(s, f, mt))),
                pl.BlockSpec((None, None, 1, d),
                             lambda s, f, se, sr, sn, mt: (layer, se[s], 0, 0)),
            ],
            out_specs=pl.BlockSpec(memory_space=pl.ANY),
            scratch_shapes=[pltpu.VMEM((2, MOE_SUPER_ROWS, half), U32),
                            pltpu.VMEM((MOE_SUPER_ROWS, d), F32),
                            pltpu.VMEM((d, tf), BF16), pltpu.VMEM((d, tf), BF16),
                            pltpu.VMEM((tf, d), BF16),
                            pltpu.SemaphoreType.DMA((2,)), pltpu.SemaphoreType.DMA(())]),
        out_shape=jax.ShapeDtypeStruct((rows, d), F32),
        compiler_params=_cparams(("arbitrary", "arbitrary")),
        name="moe_ffn",
    )(sbe, sbrow, sbnsub, meta, xs, w_gu, w_gu, w_down, bgu, bgu, bdn)


def _combine_kernel(pos_ref, ys_hbm, x_ref, gate_ref, g_ref, b_ref, o_ref, buf, sem):
    tb = x_ref.shape[0]

    def row_copy(src_row, k, dst_row):
        return pltpu.make_async_copy(ys_hbm.at[pl.ds(src_row, 1), :],
                                     buf.at[k, pl.ds(dst_row, 1), :], sem)

    def issue(r, carry):
        for k in range(TOP_K):
            row_copy(pos_ref[0, r * TOP_K + k], k, r).start()
        return carry

    lax.fori_loop(0, tb, issue, 0, unroll=4)

    def drain(r, carry):
        for k in range(TOP_K):
            row_copy(0, k, r).wait()
        return carry

    lax.fori_loop(0, tb, drain, 0, unroll=4)

    gate = gate_ref[...]
    f = gate[:, 0:1] * buf[0]
    for k in range(1, TOP_K):
        f = f + gate[:, k:k + 1] * buf[k]
    o_ref[...] = _layer_norm_rows(DEEPNORM_ALPHA * x_ref[...] + f, g_ref[...], b_ref[...])


def _combine(ys, pos, gate, x, g, b, tb):
    n, d = x.shape
    nblk = n // tb
    return pl.pallas_call(
        _combine_kernel,
        grid=(nblk,),
        in_specs=[pl.BlockSpec((None, 1, tb * TOP_K), lambda i: (i, 0, 0),
                               memory_space=pltpu.SMEM),
                  pl.BlockSpec(memory_space=pl.ANY),
                  pl.BlockSpec((tb, d), lambda i: (i, 0)),
                  pl.BlockSpec((tb, TOP_K), lambda i: (i, 0)),
                  pl.BlockSpec((1, d), lambda i: (0, 0)),
                  pl.BlockSpec((1, d), lambda i: (0, 0))],
        out_specs=pl.BlockSpec((tb, d), lambda i: (i, 0)),
        out_shape=jax.ShapeDtypeStruct((n, d), F32),
        scratch_shapes=[pltpu.VMEM((TOP_K, tb, d), F32), pltpu.SemaphoreType.DMA(())],
        compiler_params=_cparams(("arbitrary",)),
        name="moe_combine",
    )(pos.reshape(nblk, 1, tb * TOP_K), ys, x, gate, g, b)


def _dispatch_plan(top_e_t, rank_t, cnt):
    n = top_e_t.shape[1]
    r = MOE_ROW_BLOCK
    sb = MOE_SUPER_ROWS
    nblk = -(-(n * TOP_K) // r) + N_EXPERTS
    counts = cnt[:, 0].astype(I32)
    padded = (counts + r - 1) // r * r
    pend = jnp.cumsum(padded)
    pstart = pend - padded
    eids = jnp.arange(N_EXPERTS, dtype=I32)
    base = jnp.sum(jnp.where(top_e_t[:, :, None] == eids[None, None, :],
                             pstart[None, None, :], 0), axis=-1)
    dest_t = base + rank_t
    nused = pend[-1] // r
    nsb = (padded + sb - 1) // sb
    sb_end = jnp.cumsum(nsb)
    sb_start = sb_end - nsb
    total = sb_end[-1]
    n_sb_max = (nblk * r) // sb + N_EXPERTS
    sid = jnp.arange(n_sb_max, dtype=I32)
    used = sid < total
    e_of = jnp.minimum(jnp.sum(sb_end[None, :] <= jnp.minimum(sid, total - 1)[:, None], axis=1),
                       N_EXPERTS - 1).astype(I32)
    idx = jnp.minimum(sid, total - 1) - sb_start[e_of]
    row0 = jnp.where(used, pstart[e_of] + idx * sb, 0).astype(I32)
    nsub = jnp.where(used, jnp.clip((padded[e_of] - idx * sb) // r, 0, sb // r), 0).astype(I32)
    meta = jnp.stack([total, nused]).astype(I32)
    return dest_t.astype(I32), meta, e_of, row0, nsub, nblk


def _moe_and_norm(x1, x1p, top_e_t, gate_t, rank_t, cnt, w_gu, b_gu, w_down, b_down, g, b, layer):
    dest_t, meta, sbe, sbrow, sbnsub, nblk = _dispatch_plan(top_e_t, rank_t, cnt)
    row_tok = _invert(dest_t.reshape(-1), nblk * MOE_ROW_BLOCK, dest_t.shape[1])
    xs = _gather_rows(x1p, row_tok, meta, nblk)
    ys = _moe_ffn(xs, w_gu, b_gu, w_down, b_down, sbe, sbrow, sbnsub, meta, layer, nblk)
    return _combine(ys, dest_t.T, gate_t.T, x1, g, b, tb=128)


def _rotary_tables(s):
    half = ROPE_DIM // 2
    inv_freq = ROPE_THETA ** (-jnp.arange(half, dtype=F32) * 2.0 / ROPE_DIM)
    ang = jnp.arange(s, dtype=F32)[:, None] * inv_freq[None, :]
    cos, sin = jnp.cos(ang), jnp.sin(ang)
    rest = HEAD_DIM - ROPE_DIM
    cos_t = jnp.concatenate([cos, cos, jnp.ones((s, rest), F32)], axis=1)
    s1_t = jnp.concatenate([-sin, jnp.zeros((s, half + rest), F32)], axis=1)
    s2_t = jnp.concatenate([jnp.zeros((s, half), F32), sin, jnp.zeros((s, rest), F32)], axis=1)
    return cos_t, s1_t, s2_t


def kernel(x, mem, mlstm_w_in, mlstm_conv_w, mlstm_conv_b, mlstm_b_igate, mlstm_b_fgate, moba_w_in, w_mem_kv, w_out, ln1_g, ln1_b, w_router, b_router, w_gu, b_gu, w_down, b_down, ln2_g, ln2_b):
    bsz, s, d = x.shape
    n = bsz * s
    n_mem = mem.shape[1]
    mix_w = d - MEM_WIDTH
    nh_ml = MLSTM_HEADS
    dv = mix_w // nh_ml
    dqk = dv // 2
    qkw2 = 2 * nh_ml * dqk
    nh_mb = mix_w // HEAD_DIM
    qm_block = 3 * mix_w // MEM_WIDTH
    assert qkw2 == mix_w and 3 * mix_w % MEM_WIDTH == 0

    xf = x.reshape(n, d)
    memf = mem.reshape(bsz * n_mem, d)
    for i in range(DEPTH):
        j = i // 2
        if i % 2 == 0:
            w_in = mlstm_w_in[j]
            gate_lo = qkw2 + 2 * mix_w
            gate_hi = gate_lo + 2 * nh_ml
            w_main = jnp.concatenate([w_in[:, :gate_lo], w_in[:, gate_hi:]], axis=1).astype(BF16)
            w_gate = jnp.pad(w_in[:, gate_lo:gate_hi], ((0, 0), (0, LANES - 2 * nh_ml)))
            z = _matmul(xf, w_main, F32, tm=1024, tn=512).reshape(bsz, s, -1)
            gates = _matmul3(xf, w_gate, tm=1024).reshape(bsz, s, LANES)
            gates_t = jnp.swapaxes(gates[:, :, :2 * SUBLANES], 1, 2)
            bias = jnp.concatenate([mlstm_b_igate[j], mlstm_b_fgate[j]])
            bias_row = jnp.pad(bias, (0, LANES - 2 * nh_ml)).reshape(1, LANES)
            bias_col = jnp.pad(bias, (0, 2 * SUBLANES - 2 * nh_ml)).reshape(2 * SUBLANES, 1)
            h_mix = _mlstm(z, gates, gates_t, mlstm_conv_w[j], mlstm_conv_b[j].reshape(1, -1),
                           bias_row, bias_col, nh=nh_ml, dqk=dqk, dv=dv, chunk=MLSTM_CHUNK)
        else:
            z = _matmul(xf, moba_w_in[j].astype(BF16), F32, tm=1024, tn=512).reshape(bsz, s, -1)
            cos_t, s1_t, s2_t = _rotary_tables(s)
            q_r, k_r, v_b, kbar = _moba_prep(z, cos_t, s1_t, s2_t, nh_mb)
            h_mix = _moba_attn(q_r, k_r, v_b, kbar.reshape(bsz, s // MOBA_BLOCK, mix_w), nh_mb)
        kv = _matmul(memf, w_mem_kv[i].astype(BF16), BF16, tm=1024, tn=512)
        h_mem = _mem_attn(z, kv.reshape(bsz, n_mem, 2 * MEM_WIDTH), qm_block, tq=512)
        w_o = w_out[i].astype(BF16)
        x1, x1p, top_e_t, gate_t, rank_t, cnt = _outproj(
            h_mix.reshape(n, mix_w), h_mem.reshape(n, MEM_WIDTH), w_o[:mix_w], w_o[mix_w:], xf,
            ln1_g[i].reshape(1, d), ln1_b[i].reshape(1, d),
            w_router[i].T, b_router[i].reshape(N_EXPERTS, 1), tm=512)
        xf = _moe_and_norm(x1, x1p, top_e_t, gate_t, rank_t, cnt, w_gu, b_gu, w_down, b_down,
                           ln2_g[i].reshape(1, d), ln2_b[i].reshape(1, d), i)
    return xf.reshape(bsz, s, d)
```

```python
import functools

import jax
import jax.numpy as jnp
from jax import lax
from jax.experimental import pallas as pl
from jax.experimental.pallas import tpu as pltpu

F32 = jnp.float32
BF16 = jnp.bfloat16
U32 = jnp.uint32
I32 = jnp.int32

HEAD_DIM = 128
MEM_HEADS = 4
MEM_WIDTH = MEM_HEADS * HEAD_DIM
MLSTM_HEADS = 6
CONV_WIDTH = 4
MOBA_BLOCK = 256
MOBA_TOPK = 3
ROPE_THETA = 500000.0
ROPE_DIM = HEAD_DIM // 4
N_EXPERTS = 32
TOP_K = 4
SWIGLU_LIMIT = 7.0
SWIGLU_ALPHA = 1.702
LN_EPS = 1e-5
DEPTH = 2
DEEPNORM_ALPHA = (2 * DEPTH) ** 0.25

LANES = 128
SUBLANES = 8
VMEM_LIMIT_BYTES = 56 * 1024 * 1024
MLSTM_CHUNK = 128
MOE_ROW_BLOCK = 256
MOBA_HEAD_GROUP = 4
MOE_SUPER_ROWS = 1280
MOE_FF_TILE = 512
NEG_BIG = -1e30


def _cparams(sem):
    return pltpu.CompilerParams(dimension_semantics=sem, vmem_limit_bytes=VMEM_LIMIT_BYTES)


def _dot(a, b):
    return jnp.dot(a, b, preferred_element_type=F32)


def _dot_nt(a, b):
    return lax.dot_general(a, b, (((1,), (1,)), ((), ())), preferred_element_type=F32)


def _dot_tn(a, b):
    return lax.dot_general(a, b, (((0,), (0,)), ((), ())), preferred_element_type=F32)


def _split_bf16(x):
    hi = x.astype(BF16)
    lo = (x - hi.astype(F32)).astype(BF16)
    return hi, lo


def _layer_norm_rows(r, g, b):
    mu = jnp.mean(r, axis=-1, keepdims=True)
    d = r - mu
    var = jnp.mean(d * d, axis=-1, keepdims=True)
    return d * lax.rsqrt(var + LN_EPS) * g + b


def _pack_bf16_pairs(x):
    w = x.shape[1] // 2
    lo = pltpu.bitcast(x[:, :w].astype(BF16).astype(F32), U32)
    hi = pltpu.bitcast(x[:, w:].astype(BF16).astype(F32), U32)
    return (hi & jnp.uint32(0xFFFF0000)) | (lo >> 16)


def _unpack_bf16_pairs(p):
    lo = pltpu.bitcast(p << 16, F32).astype(BF16)
    hi = pltpu.bitcast(p & jnp.uint32(0xFFFF0000), F32).astype(BF16)
    return lo, hi


def _matmul_kernel(a_ref, b_ref, o_ref, a_bf_ref):
    @pl.when(pl.program_id(1) == 0)
    def _():
        a_bf_ref[...] = a_ref[...].astype(BF16)

    o_ref[...] = _dot(a_bf_ref[...], b_ref[...]).astype(o_ref.dtype)


def _matmul(a, b, out_dtype, tm, tn):
    m, k = a.shape
    n = b.shape[1]
    return pl.pallas_call(
        _matmul_kernel,
        grid=(m // tm, n // tn),
        in_specs=[pl.BlockSpec((tm, k), lambda i, j: (i, 0)),
                  pl.BlockSpec((k, tn), lambda i, j: (0, j))],
        out_specs=pl.BlockSpec((tm, tn), lambda i, j: (i, j)),
        out_shape=jax.ShapeDtypeStruct((m, n), out_dtype),
        scratch_shapes=[pltpu.VMEM((tm, k), BF16)],
        compiler_params=_cparams(("parallel", "arbitrary")),
        name="matmul",
    )(a, b)


def _matmul3_kernel(a_ref, b_ref, o_ref):
    a_hi, a_lo = _split_bf16(a_ref[...])
    b_hi, b_lo = _split_bf16(b_ref[...])
    o_ref[...] = _dot(a_hi, b_hi) + _dot(a_hi, b_lo) + _dot(a_lo, b_hi)


def _matmul3(a, b, tm):
    m, k = a.shape
    n = b.shape[1]
    return pl.pallas_call(
        _matmul3_kernel,
        grid=(m // tm,),
        in_specs=[pl.BlockSpec((tm, k), lambda i: (i, 0)),
                  pl.BlockSpec((k, n), lambda i: (0, 0))],
        out_specs=pl.BlockSpec((tm, n), lambda i: (i, 0)),
        out_shape=jax.ShapeDtypeStruct((m, n), F32),
        compiler_params=_cparams(("parallel",)),
        name="matmul3",
    )(a, b)


def _log_sigmoid(x):
    return jnp.minimum(x, 0.0) - jnp.log1p(jnp.exp(-jnp.abs(x)))


def _mlstm_kernel(qk_ref, v_ref, og_ref, g_ref, gt_ref, cw_ref, cb_ref, bias_ref, biast_ref,
                  out_ref, ext_ref, c_ref, n_ref, m_ref, *, chunk, nh, dqk, dv):
    L = chunk
    qkw = nh * dqk

    @pl.when(pl.program_id(1) == 0)
    def _():
        ext_ref[0:SUBLANES, :] = jnp.zeros((SUBLANES, 2 * qkw), F32)
        c_ref[...] = jnp.zeros_like(c_ref)
        n_ref[...] = jnp.zeros_like(n_ref)
        m_ref[...] = jnp.zeros_like(m_ref)

    ext_ref[SUBLANES:SUBLANES + L, :] = qk_ref[...]
    cw = cw_ref[...]
    y = jnp.broadcast_to(cb_ref[...], (L, 2 * qkw))
    for w in range(CONV_WIDTH):
        y = y + ext_ref[pl.ds(SUBLANES - (CONV_WIDTH - 1) + w, L), :] * cw[w:w + 1, :]
    ext_ref[0:SUBLANES, :] = ext_ref[L:L + SUBLANES, :]
    qk = y * jax.nn.sigmoid(y)

    gates = g_ref[...] + bias_ref[...]
    gates_t = gt_ref[...] + biast_ref[...]
    row = lax.broadcasted_iota(jnp.int32, (L, L), 0)
    col = lax.broadcasted_iota(jnp.int32, (L, L), 1)
    causal = col <= row

    for h in range(nh):
        q = qk[:, h * dqk:(h + 1) * dqk]
        k = qk[:, qkw + h * dqk:qkw + (h + 1) * dqk] * (dqk ** -0.5)
        v_bf = v_ref[:, h * dv:(h + 1) * dv].astype(BF16)
        q_bf = q.astype(BF16)
        ig_c = gates[:, h:h + 1]
        ig_r = gates_t[h:h + 1, :]
        lf_c = _log_sigmoid(gates[:, nh + h:nh + h + 1])
        lf_r = _log_sigmoid(gates_t[nh + h:nh + h + 1, :])
        bcum_c = jnp.sum(jnp.where(causal, lf_r, 0.0), axis=1, keepdims=True)
        bcum_r = jnp.sum(jnp.where(row <= col, lf_c, 0.0), axis=0, keepdims=True)
        g_tot = jnp.sum(lf_r, axis=1, keepdims=True)
        m_prev = m_ref[h:h + 1, 0:1]

        dmat = jnp.where(causal, bcum_c - bcum_r + ig_r, -jnp.inf)
        inter_log = bcum_c + m_prev
        m_t = jnp.maximum(inter_log, jnp.max(dmat, axis=1, keepdims=True))
        s_qk = _dot_nt(q_bf, k.astype(BF16)) * jnp.exp(dmat - m_t)
        inter_w = jnp.exp(inter_log - m_t)
        c_prev = c_ref[h]
        n_prev = n_ref[h:h + 1, :]
        num = inter_w * _dot(q_bf, c_prev.astype(BF16)) + _dot(s_qk.astype(BF16), v_bf)
        den = inter_w * jnp.sum(q * n_prev, axis=1, keepdims=True) \
            + jnp.sum(s_qk, axis=1, keepdims=True)
        hval = num / jnp.maximum(jnp.abs(den), jnp.exp(-m_t))
        o_gate = jax.nn.sigmoid(og_ref[:, h * dv:(h + 1) * dv])
        out_ref[:, h * dv:(h + 1) * dv] = (o_gate * hval).astype(out_ref.dtype)

        a_c = g_tot - bcum_c + ig_c
        m_new = jnp.maximum(g_tot + m_prev, jnp.max(a_c, axis=0, keepdims=True))
        decay = jnp.exp(g_tot + m_prev - m_new)
        kw = k * jnp.exp(a_c - m_new)
        c_ref[h] = decay * c_prev + _dot_tn(kw.astype(BF16), v_bf)
        n_ref[h:h + 1, :] = decay * n_prev + jnp.sum(kw, axis=0, keepdims=True)
        m_ref[h:h + 1, :] = jnp.broadcast_to(m_new, (1, LANES))


def _mlstm(z, gates, gates_t, conv_w, conv_b, bias_row, bias_col, *, nh, dqk, dv, chunk):
    bsz, s, _ = z.shape
    qkw2 = 2 * nh * dqk
    vw = nh * dv
    assert qkw2 == vw, "q|k, v and output-gate column blocks must share one block width"
    nc = s // chunk
    kern = functools.partial(_mlstm_kernel, chunk=chunk, nh=nh, dqk=dqk, dv=dv)
    return pl.pallas_call(
        kern,
        grid=(bsz, nc),
        in_specs=[
            pl.BlockSpec((None, chunk, vw), lambda b, c: (b, c, 0)),
            pl.BlockSpec((None, chunk, vw), lambda b, c: (b, c, 1)),
            pl.BlockSpec((None, chunk, vw), lambda b, c: (b, c, 2)),
            pl.BlockSpec((None, chunk, LANES), lambda b, c: (b, c, 0)),
            pl.BlockSpec((None, 2 * SUBLANES, chunk), lambda b, c: (b, 0, c)),
            pl.BlockSpec((CONV_WIDTH, qkw2), lambda b, c: (0, 0)),
            pl.BlockSpec((1, qkw2), lambda b, c: (0, 0)),
            pl.BlockSpec((1, LANES), lambda b, c: (0, 0)),
            pl.BlockSpec((2 * SUBLANES, 1), lambda b, c: (0, 0)),
        ],
        out_specs=pl.BlockSpec((None, chunk, vw), lambda b, c: (b, c, 0)),
        out_shape=jax.ShapeDtypeStruct((bsz, s, vw), BF16),
        scratch_shapes=[
            pltpu.VMEM((chunk + SUBLANES, qkw2), F32),
            pltpu.VMEM((nh, dqk, dv), F32),
            pltpu.VMEM((SUBLANES, dqk), F32),
            pltpu.VMEM((SUBLANES, LANES), F32),
        ],
        compiler_params=_cparams(("parallel", "arbitrary")),
        name="mlstm",
    )(z, z, z, gates, gates_t, conv_w, conv_b, bias_row, bias_col)


def _mem_attn_kernel(q_ref, k_ref, v_ref, o_ref):
    scale = HEAD_DIM ** -0.5
    for h in range(MEM_HEADS):
        sl = slice(h * HEAD_DIM, (h + 1) * HEAD_DIM)
        q = q_ref[:, sl].astype(BF16)
        sc = _dot_nt(q, k_ref[:, sl]) * scale
        sc = sc - jnp.max(sc, axis=1, keepdims=True)
        p = jnp.exp(sc)
        p = p / jnp.sum(p, axis=1, keepdims=True)
        o_ref[:, sl] = _dot(p.astype(BF16), v_ref[:, sl]).astype(o_ref.dtype)


def _mem_attn(z, kv, qm_block, tq):
    bsz, s, _ = z.shape
    n_mem = kv.shape[1]
    return pl.pallas_call(
        _mem_attn_kernel,
        grid=(bsz, s // tq),
        in_specs=[
            pl.BlockSpec((None, tq, MEM_WIDTH), lambda b, i: (b, i, qm_block)),
            pl.BlockSpec((None, n_mem, MEM_WIDTH), lambda b, i: (b, 0, 0)),
            pl.BlockSpec((None, n_mem, MEM_WIDTH), lambda b, i: (b, 0, 1)),
        ],
        out_specs=pl.BlockSpec((None, tq, MEM_WIDTH), lambda b, i: (b, i, 0)),
        out_shape=jax.ShapeDtypeStruct((bsz, s, MEM_WIDTH), BF16),
        compiler_params=_cparams(("parallel", "parallel")),
        name="mem_attn",
    )(z, kv, kv)


def _moba_prep_kernel(q_ref, k_ref, v_ref, cos_ref, s1_ref, s2_ref,
                      qo_ref, ko_ref, vo_ref, kbar_ref, *, nh):
    cosf = cos_ref[...]
    s1 = s1_ref[...]
    s2 = s2_ref[...]
    half = ROPE_DIM // 2
    for h in range(nh):
        sl = slice(h * HEAD_DIM, (h + 1) * HEAD_DIM)
        for src, dst, is_k in ((q_ref, qo_ref, False), (k_ref, ko_ref, True)):
            x = src[:, sl]
            xr = (x * cosf + pltpu.roll(x, HEAD_DIM - half, 1) * s1
                  + pltpu.roll(x, half, 1) * s2)
            if is_k:
                dst[:, sl] = xr.astype(dst.dtype)
                kbar_ref[:, sl] = jnp.mean(xr, axis=0, keepdims=True)
            else:
                dst[:, sl] = (xr * (HEAD_DIM ** -0.5)).astype(dst.dtype)
    vo_ref[...] = v_ref[...].astype(vo_ref.dtype)


def _moba_prep(z, cos_t, s1_t, s2_t, nh):
    bsz, s, _ = z.shape
    w = nh * HEAD_DIM
    nb = s // MOBA_BLOCK
    t = MOBA_BLOCK
    kern = functools.partial(_moba_prep_kernel, nh=nh)
    tab = pl.BlockSpec((t, HEAD_DIM), lambda b, i: (i, 0))
    big = jax.ShapeDtypeStruct((bsz, s, w), BF16)
    return pl.pallas_call(
        kern,
        grid=(bsz, nb),
        in_specs=[pl.BlockSpec((None, t, w), lambda b, i: (b, i, 0)),
                  pl.BlockSpec((None, t, w), lambda b, i: (b, i, 1)),
                  pl.BlockSpec((None, t, w), lambda b, i: (b, i, 2)),
                  tab, tab, tab],
        out_specs=[pl.BlockSpec((None, t, w), lambda b, i: (b, i, 0)),
                   pl.BlockSpec((None, t, w), lambda b, i: (b, i, 0)),
                   pl.BlockSpec((None, t, w), lambda b, i: (b, i, 0)),
                   pl.BlockSpec((None, None, 1, w), lambda b, i: (b, i, 0, 0))],
        out_shape=[big, big, big, jax.ShapeDtypeStruct((bsz, nb, 1, w), F32)],
        compiler_params=_cparams(("parallel", "parallel")),
        name="moba_prep",
    )(z, z, z, cos_t, s1_t, s2_t)


def _moba_attn_kernel(q_ref, k_ref, v_ref, kbar_ref, o_ref, *, nb, hg):
    t = MOBA_BLOCK
    qb = pl.program_id(2)
    blk = lax.broadcasted_iota(jnp.int32, (nb, t), 0)
    valid = blk < qb
    pad_rows = 2 * SUBLANES - nb
    eye = jnp.where(lax.broadcasted_iota(jnp.int32, (nb + pad_rows, LANES), 0)
                    == lax.broadcasted_iota(jnp.int32, (nb + pad_rows, LANES), 1),
                    1.0, 0.0).astype(BF16)

    qs = []
    sel_biases = []
    for hh in range(hg):
        sl = slice(hh * HEAD_DIM, (hh + 1) * HEAD_DIM)
        q = q_ref[:, sl]
        kb_hi, kb_lo = _split_bf16(kbar_ref[:, sl])
        gate = _dot_nt(kb_hi, q) + _dot_nt(kb_lo, q)
        gm = jnp.where(valid, gate, -jnp.inf)
        rank = jnp.zeros((nb, t), jnp.int32)
        for n2 in range(nb):
            gc = gm[n2:n2 + 1, :]
            beats = (gc > gm) | ((gc == gm) & (n2 < blk))
            rank = rank + beats.astype(jnp.int32)
        bias_t = jnp.where(valid & (rank < MOBA_TOPK), 0.0, NEG_BIG)
        bias_t = jnp.concatenate([bias_t, jnp.zeros((pad_rows, t), F32)], axis=0).astype(BF16)
        sel_biases.append(_dot_tn(bias_t, eye))
        qs.append(q)

    row = lax.broadcasted_iota(jnp.int32, (t, t), 0)
    col = lax.broadcasted_iota(jnp.int32, (t, t), 1)
    causal = col <= row

    for c in range(nb):
        @pl.when(qb == c)
        def _(c=c):
            for hh in range(hg):
                sl = slice(hh * HEAD_DIM, (hh + 1) * HEAD_DIM)
                pieces = []
                for n in range(c + 1):
                    s = _dot_nt(qs[hh], k_ref[n * t:(n + 1) * t, sl])
                    if n == c:
                        pieces.append(jnp.where(causal, s, NEG_BIG))
                    else:
                        pieces.append(s + sel_biases[hh][:, n:n + 1])
                mx = pieces[0]
                for piece in pieces[1:]:
                    mx = jnp.maximum(mx, piece)
                m = jnp.max(mx, axis=1, keepdims=True)
                psum = None
                acc = None
                for n in range(c + 1):
                    p = jnp.exp(pieces[n] - m)
                    pv = _dot(p.astype(BF16), v_ref[n * t:(n + 1) * t, sl])
                    psum = p if psum is None else psum + p
                    acc = pv if acc is None else acc + pv
                l = jnp.sum(psum, axis=1, keepdims=True)
                o_ref[:, sl] = (acc / l).astype(o_ref.dtype)


def _moba_attn(q, k, v, kbar, nh, hg):
    bsz, s, _ = q.shape
    nb = s // MOBA_BLOCK
    t = MOBA_BLOCK
    w = hg * HEAD_DIM
    kern = functools.partial(_moba_attn_kernel, nb=nb, hg=hg)
    return pl.pallas_call(
        kern,
        grid=(bsz, nh // hg, nb),
        in_specs=[pl.BlockSpec((None, t, w), lambda b, h, i: (b, i, h)),
                  pl.BlockSpec((None, s, w), lambda b, h, i: (b, 0, h)),
                  pl.BlockSpec((None, s, w), lambda b, h, i: (b, 0, h)),
                  pl.BlockSpec((None, nb, w), lambda b, h, i: (b, 0, h))],
        out_specs=pl.BlockSpec((None, t, w), lambda b, h, i: (b, i, h)),
        out_shape=jax.ShapeDtypeStruct((bsz, s, nh * HEAD_DIM), BF16),
        compiler_params=_cparams(("parallel", "parallel", "arbitrary")),
        name="moba_attn",
    )(q, k, v, kbar)


def _outproj_kernel(hmix_ref, hmem_ref, w1_ref, w2_ref, x_ref, g_ref, b_ref, wr_ref, br_ref,
                    x1_ref, x1p_ref, tope_ref, gate_ref, rank_ref, cnt_ref, tri_ref, run_ref):
    i = pl.program_id(0)
    y = _dot(hmix_ref[...], w1_ref[...]) + _dot(hmem_ref[...], w2_ref[...])
    x1 = _layer_norm_rows(DEEPNORM_ALPHA * x_ref[...] + y, g_ref[...], b_ref[...])
    x1_ref[...] = x1
    x1p_ref[...] = _pack_bf16_pairs(x1)

    x_hi, x_lo = _split_bf16(x1)
    w_hi, w_lo = _split_bf16(wr_ref[...])
    logits = _dot_nt(w_hi, x_hi) + _dot_nt(w_hi, x_lo) + _dot_nt(w_lo, x_hi) + br_ref[...]
    ne, tm = logits.shape

    @pl.when(i == 0)
    def _():
        r = lax.broadcasted_iota(jnp.int32, (tm, tm), 0)
        c = lax.broadcasted_iota(jnp.int32, (tm, tm), 1)
        tri_ref[...] = jnp.where(r <= c, 1.0, 0.0).astype(BF16)
        run_ref[...] = jnp.zeros_like(run_ref)

    eid = lax.broadcasted_iota(jnp.int32, (ne, tm), 0)
    run = run_ref[...]
    vals = []
    for k in range(TOP_K):
        mx = jnp.max(logits, axis=0, keepdims=True)
        idx = jnp.min(jnp.where(logits == mx, eid, ne), axis=0, keepdims=True)
        tope_ref[k:k + 1, :] = idx
        vals.append(mx)
        hit = eid == idx
        logits = jnp.where(hit, -jnp.inf, logits)
        incl = _dot(jnp.where(hit, 1.0, 0.0).astype(BF16), tri_ref[...])
        rank = jnp.sum(jnp.where(hit, run + incl - 1.0, 0.0), axis=0, keepdims=True)
        rank_ref[k:k + 1, :] = rank.astype(jnp.int32)
        run = run + incl[:, tm - 1:tm]
    run_ref[...] = run
    cnt_ref[...] = jnp.broadcast_to(run, cnt_ref.shape)
    ex = [jnp.exp(vk - vals[0]) for vk in vals]
    tot = ex[0] + ex[1] + ex[2] + ex[3]
    for k in range(TOP_K):
        gate_ref[k:k + 1, :] = ex[k] / tot


def _outproj(hmix, hmem, w1, w2, x, g, b, wr_t, br_col, tm):
    m, d = x.shape
    ne = wr_t.shape[0]
    return pl.pallas_call(
        _outproj_kernel,
        grid=(m // tm,),
        in_specs=[pl.BlockSpec((tm, hmix.shape[1]), lambda i: (i, 0)),
                  pl.BlockSpec((tm, hmem.shape[1]), lambda i: (i, 0)),
                  pl.BlockSpec(w1.shape, lambda i: (0, 0)),
                  pl.BlockSpec(w2.shape, lambda i: (0, 0)),
                  pl.BlockSpec((tm, d), lambda i: (i, 0)),
                  pl.BlockSpec((1, d), lambda i: (0, 0)),
                  pl.BlockSpec((1, d), lambda i: (0, 0)),
                  pl.BlockSpec(wr_t.shape, lambda i: (0, 0)),
                  pl.BlockSpec(br_col.shape, lambda i: (0, 0))],
        out_specs=[pl.BlockSpec((tm, d), lambda i: (i, 0)),
                   pl.BlockSpec((tm, d // 2), lambda i: (i, 0)),
                   pl.BlockSpec((TOP_K, tm), lambda i: (0, i)),
                   pl.BlockSpec((TOP_K, tm), lambda i: (0, i)),
                   pl.BlockSpec((TOP_K, tm), lambda i: (0, i)),
                   pl.BlockSpec((ne, LANES), lambda i: (0, 0))],
        out_shape=[jax.ShapeDtypeStruct((m, d), F32),
                   jax.ShapeDtypeStruct((m, d // 2), U32),
                   jax.ShapeDtypeStruct((TOP_K, m), jnp.int32),
                   jax.ShapeDtypeStruct((TOP_K, m), F32),
                   jax.ShapeDtypeStruct((TOP_K, m), jnp.int32),
                   jax.ShapeDtypeStruct((ne, LANES), F32)],
        scratch_shapes=[pltpu.VMEM((tm, tm), BF16), pltpu.VMEM((ne, 1), F32)],
        compiler_params=_cparams(("arbitrary",)),
        name="outproj_ln_router",
    )(hmix, hmem, w1, w2, x, g, b, wr_t, br_col)


INVERT_UNROLL = 16


def _invert_kernel(trips_ref, dest_ref, tok_ref, *, n):
    def clear(i, carry):
        for u in range(INVERT_UNROLL):
            tok_ref[i * INVERT_UNROLL + u] = 0
        return carry

    lax.fori_loop(0, trips_ref[0], clear, 0)
    for k in range(TOP_K):
        def put(i, carry, k=k):
            for u in range(INVERT_UNROLL):
                t = i * INVERT_UNROLL + u
                tok_ref[dest_ref[k * n + t]] = t
            return carry

        lax.fori_loop(0, trips_ref[1], put, 0)


def _invert(dest_flat, rows, n):
    assert rows % INVERT_UNROLL == 0 and n % INVERT_UNROLL == 0
    trips = jnp.array([rows // INVERT_UNROLL, n // INVERT_UNROLL], I32)
    return pl.pallas_call(
        functools.partial(_invert_kernel, n=n),
        in_specs=[pl.BlockSpec(memory_space=pltpu.SMEM), pl.BlockSpec(memory_space=pltpu.SMEM)],
        out_specs=pl.BlockSpec(memory_space=pltpu.SMEM),
        out_shape=jax.ShapeDtypeStruct((rows,), I32),
        name="moe_invert",
    )(trips, dest_flat)


ROW_DMA_UNROLL = 8


def _moe_ffn_kernel(sbe_ref, sbrow_ref, sbnsub_ref, meta_ref, tok_ref,
                    x_hbm, wg_ref, wu_ref, wd_ref, bg_ref, bu_ref, bd_ref, ys_hbm,
                    xraw, acc, wg_bf, wu_bf, wd_bf, in_sem, out_sem, *, nblk):
    s = pl.program_id(0)
    f = pl.program_id(1)
    ns = pl.num_programs(0)
    nf = pl.num_programs(1)
    r = MOE_ROW_BLOCK
    half = xraw.shape[2]
    nsub = sbnsub_ref[s]
    slot = s % 2

    def row_copy(src_row, sl, dst_row):
        return pltpu.make_async_copy(x_hbm.at[pl.ds(src_row, 1), :],
                                     xraw.at[sl, pl.ds(dst_row, 1), :], in_sem.at[sl])

    def out_copy(sub):
        row0 = pl.multiple_of(sub * r, r)
        dst = pl.multiple_of(sbrow_ref[s] + sub * r, r)
        return pltpu.make_async_copy(acc.at[pl.ds(row0, r), :], ys_hbm.at[pl.ds(dst, r), :], out_sem)

    def start_load(ss, sl):
        base = sbrow_ref[ss]

        def go(i, carry):
            for u in range(ROW_DMA_UNROLL):
                j = i * ROW_DMA_UNROLL + u
                row_copy(tok_ref[base + j], sl, j).start()
            return carry
        lax.fori_loop(0, sbnsub_ref[ss] * (r // ROW_DMA_UNROLL), go, 0)

    @pl.when(f == 0)
    def _():
        @pl.when(s == 0)
        def _():
            start_load(0, 0)

        @pl.when(s + 1 < ns)
        def _():
            start_load(s + 1, 1 - slot)

        def land(i, carry):
            for u in range(ROW_DMA_UNROLL):
                row_copy(0, slot, i * ROW_DMA_UNROLL + u).wait()
            return carry
        lax.fori_loop(0, nsub * (r // ROW_DMA_UNROLL), land, 0)

    @pl.when(nsub > 0)
    def _():
        wg_bf[...] = wg_ref[...].astype(BF16)
        wu_bf[...] = wu_ref[...].astype(BF16)
        wd_bf[...] = wd_ref[...].astype(BF16)

        def sub_block(sub, carry):
            row0 = pl.multiple_of(sub * r, r)
            x_lo, x_hi = _unpack_bf16_pairs(xraw[slot, pl.ds(row0, r), :])
            hg = _dot(x_lo, wg_bf[0:half, :]) + _dot(x_hi, wg_bf[half:, :]) + bg_ref[...]
            hu = _dot(x_lo, wu_bf[0:half, :]) + _dot(x_hi, wu_bf[half:, :]) + bu_ref[...]
            hg = jnp.minimum(hg, SWIGLU_LIMIT)
            hu = jnp.clip(hu, -SWIGLU_LIMIT, SWIGLU_LIMIT)
            hid = (hu + 1.0) * (hg * jax.nn.sigmoid(SWIGLU_ALPHA * hg))
            part = _dot(hid.astype(BF16), wd_bf[...])

            @pl.when(f == 0)
            def _():
                acc[pl.ds(row0, r), :] = part

            @pl.when((f > 0) & (f < nf - 1))
            def _():
                acc[pl.ds(row0, r), :] += part

            @pl.when(f == nf - 1)
            def _():
                acc[pl.ds(row0, r), :] += part + bd_ref[...]
                out_copy(sub).start()

            return carry

        lax.fori_loop(0, nsub, sub_block, 0)

        @pl.when(f == nf - 1)
        def _():
            def flush(sub, carry):
                out_copy(sub).wait()
                return carry
            lax.fori_loop(0, nsub, flush, 0)

    @pl.when((s == ns - 1) & (f == nf - 1))
    def _():
        acc[0:r, :] = jnp.zeros((r, acc.shape[1]), F32)

        def tail_copy(blk):
            return pltpu.make_async_copy(acc.at[pl.ds(0, r), :],
                                         ys_hbm.at[pl.ds(pl.multiple_of(blk * r, r), r), :], out_sem)

        def go(blk, carry):
            tail_copy(blk).start()
            return carry
        lax.fori_loop(meta_ref[1], nblk, go, 0)

        def done(blk, carry):
            tail_copy(blk).wait()
            return carry
        lax.fori_loop(meta_ref[1], nblk, done, 0)


def _moe_ffn(xp, row_tok, w_gu, b_gu, w_down, b_down, sbe, sbrow, sbnsub, meta, layer, nblk):
    half = xp.shape[1]
    rows = row_tok.shape[0]
    d = 2 * half
    dff = w_gu.shape[-1] // 2
    tf = MOE_FF_TILE
    nf = dff // tf
    ns = sbe.shape[0]
    bgu = b_gu.reshape(b_gu.shape[0], b_gu.shape[1], 1, 2 * dff)
    bdn = b_down.reshape(b_down.shape[0], b_down.shape[1], 1, d)

    def feff(s, f, meta):
        return jnp.where(s < meta[0], f, nf - 1)

    kern = functools.partial(_moe_ffn_kernel, nblk=nblk)
    return pl.pallas_call(
        kern,
        grid_spec=pltpu.PrefetchScalarGridSpec(
            num_scalar_prefetch=5,
            grid=(ns, nf),
            in_specs=[
                pl.BlockSpec(memory_space=pl.ANY),
                pl.BlockSpec((None, None, d, tf),
                             lambda s, f, se, sr, sn, mt, tk: (layer, se[s], 0, feff(s, f, mt))),
                pl.BlockSpec((None, None, d, tf),
                             lambda s, f, se, sr, sn, mt, tk: (layer, se[s], 0, nf + feff(s, f, mt))),
                pl.BlockSpec((None, None, tf, d),
                             lambda s, f, se, sr, sn, mt, tk: (layer, se[s], feff(s, f, mt), 0)),
                pl.BlockSpec((None, None, 1, tf),
                             lambda s, f, se, sr, sn, mt, tk: (layer, se[s], 0, feff(s, f, mt))),
                pl.BlockSpec((None, None, 1, tf),
                             lambda s, f, se, sr, sn, mt, tk: (layer, se[s], 0, nf + feff(s, f, mt))),
                pl.BlockSpec((None, None, 1, d),
                             lambda s, f, se, sr, sn, mt, tk: (layer, se[s], 0, 0)),
            ],
            out_specs=pl.BlockSpec(memory_space=pl.ANY),
            scratch_shapes=[pltpu.VMEM((2, MOE_SUPER_ROWS, half), U32),
                            pltpu.VMEM((MOE_SUPER_ROWS, d), F32),
                            pltpu.VMEM((d, tf), BF16), pltpu.VMEM((d, tf), BF16),
                            pltpu.VMEM((tf, d), BF16),
                            pltpu.SemaphoreType.DMA((2,)), pltpu.SemaphoreType.DMA(())]),
        out_shape=jax.ShapeDtypeStruct((rows, d), F32),
        compiler_params=_cparams(("arbitrary", "arbitrary")),
        name="moe_ffn",
    )(sbe, sbrow, sbnsub, meta, row_tok, xp, w_gu, w_gu, w_down, bgu, bgu, bdn)


def _combine_kernel(pos_ref, ys_hbm, x_ref, gate_ref, g_ref, b_ref, o_ref, buf, sem):
    i = pl.program_id(0)
    tb = x_ref.shape[0]
    slot = i % 2

    def row_copy(src_row, sl, k, dst_row):
        return pltpu.make_async_copy(ys_hbm.at[pl.ds(src_row, 1), :],
                                     buf.at[sl, k, pl.ds(dst_row, 1), :], sem.at[sl])

    def request(blk, sl):
        base = blk * (tb * TOP_K)

        def go(r, carry):
            for k in range(TOP_K):
                row_copy(pos_ref[base + r * TOP_K + k], sl, k, r).start()
            return carry
        lax.fori_loop(0, tb, go, 0, unroll=4)

    @pl.when(i == 0)
    def _():
        request(0, 0)

    @pl.when(i + 1 < pl.num_programs(0))
    def _():
        request(i + 1, 1 - slot)

    def land(r, carry):
        for k in range(TOP_K):
            row_copy(0, slot, k, r).wait()
        return carry

    lax.fori_loop(0, tb, land, 0, unroll=4)

    gate = gate_ref[...]
    f = gate[:, 0:1] * buf[slot, 0]
    for k in range(1, TOP_K):
        f = f + gate[:, k:k + 1] * buf[slot, k]
    o_ref[...] = _layer_norm_rows(DEEPNORM_ALPHA * x_ref[...] + f, g_ref[...], b_ref[...])


def _combine(ys, pos, gate, x, g, b, tb):
    n, d = x.shape
    return pl.pallas_call(
        _combine_kernel,
        grid_spec=pltpu.PrefetchScalarGridSpec(
            num_scalar_prefetch=1,
            grid=(n // tb,),
            in_specs=[pl.BlockSpec(memory_space=pl.ANY),
                      pl.BlockSpec((tb, d), lambda i, ps: (i, 0)),
                      pl.BlockSpec((tb, TOP_K), lambda i, ps: (i, 0)),
                      pl.BlockSpec((1, d), lambda i, ps: (0, 0)),
                      pl.BlockSpec((1, d), lambda i, ps: (0, 0))],
            out_specs=pl.BlockSpec((tb, d), lambda i, ps: (i, 0)),
            scratch_shapes=[pltpu.VMEM((2, TOP_K, tb, d), F32), pltpu.SemaphoreType.DMA((2,))]),
        out_shape=jax.ShapeDtypeStruct((n, d), F32),
        compiler_params=_cparams(("arbitrary",)),
        name="moe_combine",
    )(pos.reshape(-1), ys, x, gate, g, b)


def _dispatch_plan(top_e_t, rank_t, cnt):
    n = top_e_t.shape[1]
    r = MOE_ROW_BLOCK
    sb = MOE_SUPER_ROWS
    nblk = -(-(n * TOP_K) // r) + N_EXPERTS
    counts = cnt[:, 0].astype(I32)
    padded = (counts + r - 1) // r * r
    pend = jnp.cumsum(padded)
    pstart = pend - padded
    eids = jnp.arange(N_EXPERTS, dtype=I32)
    base = jnp.sum(jnp.where(top_e_t[:, :, None] == eids[None, None, :],
                             pstart[None, None, :], 0), axis=-1)
    dest_t = base + rank_t
    nused = pend[-1] // r
    sbb = sb // r
    nb_e = padded // r
    nsb = (nb_e + sbb - 1) // sbb
    per = (nb_e + jnp.maximum(nsb, 1) - 1) // jnp.maximum(nsb, 1)
    sb_end = jnp.cumsum(nsb)
    sb_start = sb_end - nsb
    total = sb_end[-1]
    n_sb_max = nblk // sbb + N_EXPERTS
    sid = jnp.arange(n_sb_max, dtype=I32)
    used = sid < total
    e_of = jnp.minimum(jnp.sum(sb_end[None, :] <= jnp.minimum(sid, total - 1)[:, None], axis=1),
                       N_EXPERTS - 1).astype(I32)
    idx = jnp.minimum(sid, total - 1) - sb_start[e_of]
    row0 = jnp.where(used, pstart[e_of] + idx * per[e_of] * r, 0).astype(I32)
    nsub = jnp.where(used, jnp.clip(nb_e[e_of] - idx * per[e_of], 0, per[e_of]), 0).astype(I32)
    meta = jnp.stack([total, nused]).astype(I32)
    return dest_t.astype(I32), meta, e_of, row0, nsub, nblk


def _moe_and_norm(x1, x1p, top_e_t, gate_t, rank_t, cnt, w_gu, b_gu, w_down, b_down, g, b, layer):
    dest_t, meta, sbe, sbrow, sbnsub, nblk = _dispatch_plan(top_e_t, rank_t, cnt)
    row_tok = _invert(dest_t.reshape(-1), nblk * MOE_ROW_BLOCK, dest_t.shape[1])
    ys = _moe_ffn(x1p, row_tok, w_gu, b_gu, w_down, b_down, sbe, sbrow, sbnsub, meta, layer, nblk)
    return _combine(ys, dest_t.T, gate_t.T, x1, g, b, tb=128)


def _rotary_tables(s):
    half = ROPE_DIM // 2
    inv_freq = ROPE_THETA ** (-jnp.arange(half, dtype=F32) * 2.0 / ROPE_DIM)
    ang = jnp.arange(s, dtype=F32)[:, None] * inv_freq[None, :]
    cos, sin = jnp.cos(ang), jnp.sin(ang)
    rest = HEAD_DIM - ROPE_DIM
    cos_t = jnp.concatenate([cos, cos, jnp.ones((s, rest), F32)], axis=1)
    s1_t = jnp.concatenate([-sin, jnp.zeros((s, half + rest), F32)], axis=1)
    s2_t = jnp.concatenate([jnp.zeros((s, half), F32), sin, jnp.zeros((s, rest), F32)], axis=1)
    return cos_t, s1_t, s2_t


def kernel(x, mem, mlstm_w_in, mlstm_conv_w, mlstm_conv_b, mlstm_b_igate, mlstm_b_fgate, moba_w_in, w_mem_kv, w_out, ln1_g, ln1_b, w_router, b_router, w_gu, b_gu, w_down, b_down, ln2_g, ln2_b):
    bsz, s, d = x.shape
    n = bsz * s
    n_mem = mem.shape[1]
    mix_w = d - MEM_WIDTH
    nh_ml = MLSTM_HEADS
    dv = mix_w // nh_ml
    dqk = dv // 2
    qkw2 = 2 * nh_ml * dqk
    nh_mb = mix_w // HEAD_DIM
    qm_block = 3 * mix_w // MEM_WIDTH
    assert qkw2 == mix_w and 3 * mix_w % MEM_WIDTH == 0

    xf = x.reshape(n, d)
    memf = mem.reshape(bsz * n_mem, d)
    for i in range(DEPTH):
        j = i // 2
        if i % 2 == 0:
            w_in = mlstm_w_in[j]
            gate_lo = qkw2 + 2 * mix_w
            gate_hi = gate_lo + 2 * nh_ml
            w_main = jnp.concatenate([w_in[:, :gate_lo], w_in[:, gate_hi:]], axis=1).astype(BF16)
            w_gate = jnp.pad(w_in[:, gate_lo:gate_hi], ((0, 0), (0, LANES - 2 * nh_ml)))
            z = _matmul(xf, w_main, F32, tm=1024, tn=512).reshape(bsz, s, -1)
            gates = _matmul3(xf, w_gate, tm=1024).reshape(bsz, s, LANES)
            gates_t = jnp.swapaxes(gates[:, :, :2 * SUBLANES], 1, 2)
            bias = jnp.concatenate([mlstm_b_igate[j], mlstm_b_fgate[j]])
            bias_row = jnp.pad(bias, (0, LANES - 2 * nh_ml)).reshape(1, LANES)
            bias_col = jnp.pad(bias, (0, 2 * SUBLANES - 2 * nh_ml)).reshape(2 * SUBLANES, 1)
            h_mix = _mlstm(z, gates, gates_t, mlstm_conv_w[j], mlstm_conv_b[j].reshape(1, -1),
                           bias_row, bias_col, nh=nh_ml, dqk=dqk, dv=dv, chunk=MLSTM_CHUNK)
        else:
            z = _matmul(xf, moba_w_in[j].astype(BF16), F32, tm=1024, tn=512).reshape(bsz, s, -1)
            cos_t, s1_t, s2_t = _rotary_tables(s)
            q_r, k_r, v_b, kbar = _moba_prep(z, cos_t, s1_t, s2_t, nh_mb)
            h_mix = _moba_attn(q_r, k_r, v_b, kbar.reshape(bsz, s // MOBA_BLOCK, mix_w), nh_mb,
                               hg=MOBA_HEAD_GROUP)
        kv = _matmul(memf, w_mem_kv[i].astype(BF16), BF16, tm=1024, tn=512)
        h_mem = _mem_attn(z, kv.reshape(bsz, n_mem, 2 * MEM_WIDTH), qm_block, tq=512)
        w_o = w_out[i].astype(BF16)
        x1, x1p, top_e_t, gate_t, rank_t, cnt = _outproj(
            h_mix.reshape(n, mix_w), h_mem.reshape(n, MEM_WIDTH), w_o[:mix_w], w_o[mix_w:], xf,
            ln1_g[i].reshape(1, d), ln1_b[i].reshape(1, d),
            w_router[i].T, b_router[i].reshape(N_EXPERTS, 1), tm=512)
        xf = _moe_and_norm(x1, x1p, top_e_t, gate_t, rank_t, cnt, w_gu, b_gu, w_down, b_down,
                           ln2_g[i].reshape(1, d), ln2_b[i].reshape(1, d), i)
    return xf.reshape(bsz, s, d)
```

```python
import functools

import jax
import jax.numpy as jnp
from jax import lax
from jax.experimental import pallas as pl
from jax.experimental.pallas import tpu as pltpu

F32 = jnp.float32
BF16 = jnp.bfloat16
U32 = jnp.uint32
I32 = jnp.int32

HEAD_DIM = 128
MEM_HEADS = 4
MEM_WIDTH = MEM_HEADS * HEAD_DIM
MLSTM_HEADS = 6
CONV_WIDTH = 4
MOBA_BLOCK = 256
MOBA_TOPK = 3
ROPE_THETA = 500000.0
ROPE_DIM = HEAD_DIM // 4
N_EXPERTS = 32
TOP_K = 4
SWIGLU_LIMIT = 7.0
SWIGLU_ALPHA = 1.702
LN_EPS = 1e-5
DEPTH = 2
DEEPNORM_ALPHA = (2 * DEPTH) ** 0.25

LANES = 128
SUBLANES = 8
VMEM_LIMIT_BYTES = 56 * 1024 * 1024
MLSTM_CHUNK = 128
MOE_ROW_BLOCK = 256
MOBA_HEAD_GROUP = 4
MOE_SUPER_ROWS = 1280
MOE_FF_TILE = 512
NEG_BIG = -1e30


def _cparams(sem):
    return pltpu.CompilerParams(dimension_semantics=sem, vmem_limit_bytes=VMEM_LIMIT_BYTES)


def _dot(a, b):
    return jnp.dot(a, b, preferred_element_type=F32)


def _dot_nt(a, b):
    return lax.dot_general(a, b, (((1,), (1,)), ((), ())), preferred_element_type=F32)


def _dot_tn(a, b):
    return lax.dot_general(a, b, (((0,), (0,)), ((), ())), preferred_element_type=F32)


def _split_bf16(x):
    hi = x.astype(BF16)
    lo = (x - hi.astype(F32)).astype(BF16)
    return hi, lo


def _layer_norm_rows(r, g, b):
    mu = jnp.mean(r, axis=-1, keepdims=True)
    d = r - mu
    var = jnp.mean(d * d, axis=-1, keepdims=True)
    return d * lax.rsqrt(var + LN_EPS) * g + b


def _pack_bf16_pairs(x):
    w = x.shape[1] // 2
    lo = pltpu.bitcast(x[:, :w].astype(BF16).astype(F32), U32)
    hi = pltpu.bitcast(x[:, w:].astype(BF16).astype(F32), U32)
    return (hi & jnp.uint32(0xFFFF0000)) | (lo >> 16)


def _unpack_bf16_pairs(p):
    lo = pltpu.bitcast(p << 16, F32).astype(BF16)
    hi = pltpu.bitcast(p & jnp.uint32(0xFFFF0000), F32).astype(BF16)
    return lo, hi


def _matmul_kernel(a_ref, b_ref, o_ref, a_bf_ref):
    @pl.when(pl.program_id(1) == 0)
    def _():
        a_bf_ref[...] = a_ref[...].astype(BF16)

    o_ref[...] = _dot(a_bf_ref[...], b_ref[...]).astype(o_ref.dtype)


def _matmul(a, b, out_dtype, tm, tn):
    m, k = a.shape
    n = b.shape[1]
    return pl.pallas_call(
        _matmul_kernel,
        grid=(m // tm, n // tn),
        in_specs=[pl.BlockSpec((tm, k), lambda i, j: (i, 0)),
                  pl.BlockSpec((k, tn), lambda i, j: (0, j))],
        out_specs=pl.BlockSpec((tm, tn), lambda i, j: (i, j)),
        out_shape=jax.ShapeDtypeStruct((m, n), out_dtype),
        scratch_shapes=[pltpu.VMEM((tm, k), BF16)],
        compiler_params=_cparams(("parallel", "arbitrary")),
        name="matmul",
    )(a, b)


def _matmul3_kernel(a_ref, b_ref, o_ref):
    a_hi, a_lo = _split_bf16(a_ref[...])
    b_hi, b_lo = _split_bf16(b_ref[...])
    o_ref[...] = _dot(a_hi, b_hi) + _dot(a_hi, b_lo) + _dot(a_lo, b_hi)


def _matmul3(a, b, tm):
    m, k = a.shape
    n = b.shape[1]
    return pl.pallas_call(
        _matmul3_kernel,
        grid=(m // tm,),
        in_specs=[pl.BlockSpec((tm, k), lambda i: (i, 0)),
                  pl.BlockSpec((k, n), lambda i: (0, 0))],
        out_specs=pl.BlockSpec((tm, n), lambda i: (i, 0)),
        out_shape=jax.ShapeDtypeStruct((m, n), F32),
        compiler_params=_cparams(("parallel",)),
        name="matmul3",
    )(a, b)


def _log_sigmoid(x):
    return jnp.minimum(x, 0.0) - jnp.log1p(jnp.exp(-jnp.abs(x)))


def _mlstm_kernel(qk_ref, v_ref, og_ref, g_ref, gt_ref, cw_ref, cb_ref, bias_ref, biast_ref,
                  out_ref, ext_ref, c_ref, n_ref, m_ref, *, chunk, nh, dqk, dv):
    L = chunk
    qkw = nh * dqk

    @pl.when(pl.program_id(1) == 0)
    def _():
        ext_ref[0:SUBLANES, :] = jnp.zeros((SUBLANES, 2 * qkw), F32)
        c_ref[...] = jnp.zeros_like(c_ref)
        n_ref[...] = jnp.zeros_like(n_ref)
        m_ref[...] = jnp.zeros_like(m_ref)

    ext_ref[SUBLANES:SUBLANES + L, :] = qk_ref[...]
    cw = cw_ref[...]
    y = jnp.broadcast_to(cb_ref[...], (L, 2 * qkw))
    for w in range(CONV_WIDTH):
        y = y + ext_ref[pl.ds(SUBLANES - (CONV_WIDTH - 1) + w, L), :] * cw[w:w + 1, :]
    ext_ref[0:SUBLANES, :] = ext_ref[L:L + SUBLANES, :]
    qk = y * jax.nn.sigmoid(y)

    gates = g_ref[...] + bias_ref[...]
    gates_t = gt_ref[...] + biast_ref[...]
    row = lax.broadcasted_iota(jnp.int32, (L, L), 0)
    col = lax.broadcasted_iota(jnp.int32, (L, L), 1)
    causal = col <= row

    for h in range(nh):
        q = qk[:, h * dqk:(h + 1) * dqk]
        k = qk[:, qkw + h * dqk:qkw + (h + 1) * dqk] * (dqk ** -0.5)
        v_bf = v_ref[:, h * dv:(h + 1) * dv].astype(BF16)
        q_bf = q.astype(BF16)
        ig_c = gates[:, h:h + 1]
        ig_r = gates_t[h:h + 1, :]
        lf_c = _log_sigmoid(gates[:, nh + h:nh + h + 1])
        lf_r = _log_sigmoid(gates_t[nh + h:nh + h + 1, :])
        bcum_c = jnp.sum(jnp.where(causal, lf_r, 0.0), axis=1, keepdims=True)
        bcum_r = jnp.sum(jnp.where(row <= col, lf_c, 0.0), axis=0, keepdims=True)
        g_tot = jnp.sum(lf_r, axis=1, keepdims=True)
        m_prev = m_ref[h:h + 1, 0:1]

        dmat = jnp.where(causal, bcum_c - bcum_r + ig_r, -jnp.inf)
        inter_log = bcum_c + m_prev
        m_t = jnp.maximum(inter_log, jnp.max(dmat, axis=1, keepdims=True))
        s_qk = _dot_nt(q_bf, k.astype(BF16)) * jnp.exp(dmat - m_t)
        inter_w = jnp.exp(inter_log - m_t)
        c_prev = c_ref[h]
        n_prev = n_ref[h:h + 1, :]
        num = inter_w * _dot(q_bf, c_prev.astype(BF16)) + _dot(s_qk.astype(BF16), v_bf)
        den = inter_w * jnp.sum(q * n_prev, axis=1, keepdims=True) \
            + jnp.sum(s_qk, axis=1, keepdims=True)
        hval = num / jnp.maximum(jnp.abs(den), jnp.exp(-m_t))
        o_gate = jax.nn.sigmoid(og_ref[:, h * dv:(h + 1) * dv])
        out_ref[:, h * dv:(h + 1) * dv] = (o_gate * hval).astype(out_ref.dtype)

        a_c = g_tot - bcum_c + ig_c
        m_new = jnp.maximum(g_tot + m_prev, jnp.max(a_c, axis=0, keepdims=True))
        decay = jnp.exp(g_tot + m_prev - m_new)
        kw = k * jnp.exp(a_c - m_new)
        c_ref[h] = decay * c_prev + _dot_tn(kw.astype(BF16), v_bf)
        n_ref[h:h + 1, :] = decay * n_prev + jnp.sum(kw, axis=0, keepdims=True)
        m_ref[h:h + 1, :] = jnp.broadcast_to(m_new, (1, LANES))


def _mlstm(z, gates, gates_t, conv_w, conv_b, bias_row, bias_col, *, nh, dqk, dv, chunk):
    bsz, s, _ = z.shape
    qkw2 = 2 * nh * dqk
    vw = nh * dv
    assert qkw2 == vw, "q|k, v and output-gate column blocks must share one block width"
    nc = s // chunk
    kern = functools.partial(_mlstm_kernel, chunk=chunk, nh=nh, dqk=dqk, dv=dv)
    return pl.pallas_call(
        kern,
        grid=(bsz, nc),
        in_specs=[
            pl.BlockSpec((None, chunk, vw), lambda b, c: (b, c, 0)),
            pl.BlockSpec((None, chunk, vw), lambda b, c: (b, c, 1)),
            pl.BlockSpec((None, chunk, vw), lambda b, c: (b, c, 2)),
            pl.BlockSpec((None, chunk, LANES), lambda b, c: (b, c, 0)),
            pl.BlockSpec((None, 2 * SUBLANES, chunk), lambda b, c: (b, 0, c)),
            pl.BlockSpec((CONV_WIDTH, qkw2), lambda b, c: (0, 0)),
            pl.BlockSpec((1, qkw2), lambda b, c: (0, 0)),
            pl.BlockSpec((1, LANES), lambda b, c: (0, 0)),
            pl.BlockSpec((2 * SUBLANES, 1), lambda b, c: (0, 0)),
        ],
        out_specs=pl.BlockSpec((None, chunk, vw), lambda b, c: (b, c, 0)),
        out_shape=jax.ShapeDtypeStruct((bsz, s, vw), BF16),
        scratch_shapes=[
            pltpu.VMEM((chunk + SUBLANES, qkw2), F32),
            pltpu.VMEM((nh, dqk, dv), F32),
            pltpu.VMEM((SUBLANES, dqk), F32),
            pltpu.VMEM((SUBLANES, LANES), F32),
        ],
        compiler_params=_cparams(("parallel", "arbitrary")),
        name="mlstm",
    )(z, z, z, gates, gates_t, conv_w, conv_b, bias_row, bias_col)


def _mem_attn_kernel(q_ref, k_ref, v_ref, o_ref):
    scale = HEAD_DIM ** -0.5
    for h in range(MEM_HEADS):
        sl = slice(h * HEAD_DIM, (h + 1) * HEAD_DIM)
        q = q_ref[:, sl].astype(BF16)
        sc = _dot_nt(q, k_ref[:, sl]) * scale
        sc = sc - jnp.max(sc, axis=1, keepdims=True)
        p = jnp.exp(sc)
        p = p / jnp.sum(p, axis=1, keepdims=True)
        o_ref[:, sl] = _dot(p.astype(BF16), v_ref[:, sl]).astype(o_ref.dtype)


def _mem_attn(z, kv, qm_block, tq):
    bsz, s, _ = z.shape
    n_mem = kv.shape[1]
    return pl.pallas_call(
        _mem_attn_kernel,
        grid=(bsz, s // tq),
        in_specs=[
            pl.BlockSpec((None, tq, MEM_WIDTH), lambda b, i: (b, i, qm_block)),
            pl.BlockSpec((None, n_mem, MEM_WIDTH), lambda b, i: (b, 0, 0)),
            pl.BlockSpec((None, n_mem, MEM_WIDTH), lambda b, i: (b, 0, 1)),
        ],
        out_specs=pl.BlockSpec((None, tq, MEM_WIDTH), lambda b, i: (b, i, 0)),
        out_shape=jax.ShapeDtypeStruct((bsz, s, MEM_WIDTH), BF16),
        compiler_params=_cparams(("parallel", "parallel")),
        name="mem_attn",
    )(z, kv, kv)


def _moba_prep_kernel(q_ref, k_ref, v_ref, cos_ref, s1_ref, s2_ref,
                      qo_ref, ko_ref, vo_ref, kbar_ref, *, nh):
    cosf = cos_ref[...]
    s1 = s1_ref[...]
    s2 = s2_ref[...]
    half = ROPE_DIM // 2
    for h in range(nh):
        sl = slice(h * HEAD_DIM, (h + 1) * HEAD_DIM)
        for src, dst, is_k in ((q_ref, qo_ref, False), (k_ref, ko_ref, True)):
            x = src[:, sl]
            xr = (x * cosf + pltpu.roll(x, HEAD_DIM - half, 1) * s1
                  + pltpu.roll(x, half, 1) * s2)
            if is_k:
                dst[:, sl] = xr.astype(dst.dtype)
                kbar_ref[:, sl] = jnp.mean(xr, axis=0, keepdims=True)
            else:
                dst[:, sl] = (xr * (HEAD_DIM ** -0.5)).astype(dst.dtype)
    vo_ref[...] = v_ref[...].astype(vo_ref.dtype)


def _moba_prep(z, cos_t, s1_t, s2_t, nh):
    bsz, s, _ = z.shape
    w = nh * HEAD_DIM
    nb = s // MOBA_BLOCK
    t = MOBA_BLOCK
    kern = functools.partial(_moba_prep_kernel, nh=nh)
    tab = pl.BlockSpec((t, HEAD_DIM), lambda b, i: (i, 0))
    big = jax.ShapeDtypeStruct((bsz, s, w), BF16)
    return pl.pallas_call(
        kern,
        grid=(bsz, nb),
        in_specs=[pl.BlockSpec((None, t, w), lambda b, i: (b, i, 0)),
                  pl.BlockSpec((None, t, w), lambda b, i: (b, i, 1)),
                  pl.BlockSpec((None, t, w), lambda b, i: (b, i, 2)),
                  tab, tab, tab],
        out_specs=[pl.BlockSpec((None, t, w), lambda b, i: (b, i, 0)),
                   pl.BlockSpec((None, t, w), lambda b, i: (b, i, 0)),
                   pl.BlockSpec((None, t, w), lambda b, i: (b, i, 0)),
                   pl.BlockSpec((None, None, 1, w), lambda b, i: (b, i, 0, 0))],
        out_shape=[big, big, big, jax.ShapeDtypeStruct((bsz, nb, 1, w), F32)],
        compiler_params=_cparams(("parallel", "parallel")),
        name="moba_prep",
    )(z, z, z, cos_t, s1_t, s2_t)


def _moba_attn_kernel(q_ref, k_ref, v_ref, kbar_ref, o_ref, *, nb, hg):
    t = MOBA_BLOCK
    qb = pl.program_id(2)
    blk = lax.broadcasted_iota(jnp.int32, (nb, t), 0)
    valid = blk < qb
    pad_rows = 2 * SUBLANES - nb
    eye = jnp.where(lax.broadcasted_iota(jnp.int32, (nb + pad_rows, LANES), 0)
                    == lax.broadcasted_iota(jnp.int32, (nb + pad_rows, LANES), 1),
                    1.0, 0.0).astype(BF16)

    qs = []
    sel_biases = []
    for hh in range(hg):
        sl = slice(hh * HEAD_DIM, (hh + 1) * HEAD_DIM)
        q = q_ref[:, sl]
        kb_hi, kb_lo = _split_bf16(kbar_ref[:, sl])
        gate = _dot_nt(kb_hi, q) + _dot_nt(kb_lo, q)
        gm = jnp.where(valid, gate, -jnp.inf)
        rank = jnp.zeros((nb, t), jnp.int32)
        for n2 in range(nb):
            gc = gm[n2:n2 + 1, :]
            beats = (gc > gm) | ((gc == gm) & (n2 < blk))
            rank = rank + beats.astype(jnp.int32)
        bias_t = jnp.where(valid & (rank < MOBA_TOPK), 0.0, NEG_BIG)
        bias_t = jnp.concatenate([bias_t, jnp.zeros((pad_rows, t), F32)], axis=0).astype(BF16)
        sel_biases.append(_dot_tn(bias_t, eye))
        qs.append(q)

    row = lax.broadcasted_iota(jnp.int32, (t, t), 0)
    col = lax.broadcasted_iota(jnp.int32, (t, t), 1)
    causal = col <= row

    for c in range(nb):
        @pl.when(qb == c)
        def _(c=c):
            for hh in range(hg):
                sl = slice(hh * HEAD_DIM, (hh + 1) * HEAD_DIM)
                pieces = []
                for n in range(c + 1):
                    s = _dot_nt(qs[hh], k_ref[n * t:(n + 1) * t, sl])
                    if n == c:
                        pieces.append(jnp.where(causal, s, NEG_BIG))
                    else:
                        pieces.append(s + sel_biases[hh][:, n:n + 1])
                mx = pieces[0]
                for piece in pieces[1:]:
                    mx = jnp.maximum(mx, piece)
                m = jnp.max(mx, axis=1, keepdims=True)
                psum = None
                acc = None
                for n in range(c + 1):
                    p = jnp.exp(pieces[n] - m)
                    pv = _dot(p.astype(BF16), v_ref[n * t:(n + 1) * t, sl])
                    psum = p if psum is None else psum + p
                    acc = pv if acc is None else acc + pv
                l = jnp.sum(psum, axis=1, keepdims=True)
                o_ref[:, sl] = (acc / l).astype(o_ref.dtype)


def _moba_attn(q, k, v, kbar, nh, hg):
    bsz, s, _ = q.shape
    nb = s // MOBA_BLOCK
    t = MOBA_BLOCK
    w = hg * HEAD_DIM
    kern = functools.partial(_moba_attn_kernel, nb=nb, hg=hg)
    return pl.pallas_call(
        kern,
        grid=(bsz, nh // hg, nb),
        in_specs=[pl.BlockSpec((None, t, w), lambda b, h, i: (b, i, h)),
                  pl.BlockSpec((None, s, w), lambda b, h, i: (b, 0, h)),
                  pl.BlockSpec((None, s, w), lambda b, h, i: (b, 0, h)),
                  pl.BlockSpec((None, nb, w), lambda b, h, i: (b, 0, h))],
        out_specs=pl.BlockSpec((None, t, w), lambda b, h, i: (b, i, h)),
        out_shape=jax.ShapeDtypeStruct((bsz, s, nh * HEAD_DIM), BF16),
        compiler_params=_cparams(("parallel", "parallel", "arbitrary")),
        name="moba_attn",
    )(q, k, v, kbar)


def _outproj_kernel(hmix_ref, hmem_ref, w1_ref, w2_ref, x_ref, g_ref, b_ref, wr_ref, br_ref,
                    x1_ref, x1p_ref, tope_ref, gate_ref, rank_ref, cnt_ref, tri_ref, run_ref):
    i = pl.program_id(0)
    y = _dot(hmix_ref[...], w1_ref[...]) + _dot(hmem_ref[...], w2_ref[...])
    x1 = _layer_norm_rows(DEEPNORM_ALPHA * x_ref[...] + y, g_ref[...], b_ref[...])
    x1_ref[...] = x1
    packed = _pack_bf16_pairs(x1)
    for jj in range(packed.shape[1] // LANES):
        x1p_ref[pl.ds(jj, packed.shape[0], stride=SUBLANES), :] = packed[:, jj * LANES:(jj + 1) * LANES]

    x_hi, x_lo = _split_bf16(x1)
    w_hi, w_lo = _split_bf16(wr_ref[...])
    logits = _dot_nt(w_hi, x_hi) + _dot_nt(w_hi, x_lo) + _dot_nt(w_lo, x_hi) + br_ref[...]
    ne, tm = logits.shape

    @pl.when(i == 0)
    def _():
        r = lax.broadcasted_iota(jnp.int32, (tm, tm), 0)
        c = lax.broadcasted_iota(jnp.int32, (tm, tm), 1)
        tri_ref[...] = jnp.where(r <= c, 1.0, 0.0).astype(BF16)
        run_ref[...] = jnp.zeros_like(run_ref)

    eid = lax.broadcasted_iota(jnp.int32, (ne, tm), 0)
    run = run_ref[...]
    vals = []
    for k in range(TOP_K):
        mx = jnp.max(logits, axis=0, keepdims=True)
        idx = jnp.min(jnp.where(logits == mx, eid, ne), axis=0, keepdims=True)
        tope_ref[k:k + 1, :] = idx
        vals.append(mx)
        hit = eid == idx
        logits = jnp.where(hit, -jnp.inf, logits)
        incl = _dot(jnp.where(hit, 1.0, 0.0).astype(BF16), tri_ref[...])
        rank = jnp.sum(jnp.where(hit, run + incl - 1.0, 0.0), axis=0, keepdims=True)
        rank_ref[k:k + 1, :] = rank.astype(jnp.int32)
        run = run + incl[:, tm - 1:tm]
    run_ref[...] = run
    cnt_ref[...] = jnp.broadcast_to(run, cnt_ref.shape)
    ex = [jnp.exp(vk - vals[0]) for vk in vals]
    tot = ex[0] + ex[1] + ex[2] + ex[3]
    for k in range(TOP_K):
        gate_ref[k:k + 1, :] = ex[k] / tot


def _outproj(hmix, hmem, w1, w2, x, g, b, wr_t, br_col, tm):
    m, d = x.shape
    ne = wr_t.shape[0]
    return pl.pallas_call(
        _outproj_kernel,
        grid=(m // tm,),
        in_specs=[pl.BlockSpec((tm, hmix.shape[1]), lambda i: (i, 0)),
                  pl.BlockSpec((tm, hmem.shape[1]), lambda i: (i, 0)),
                  pl.BlockSpec(w1.shape, lambda i: (0, 0)),
                  pl.BlockSpec(w2.shape, lambda i: (0, 0)),
                  pl.BlockSpec((tm, d), lambda i: (i, 0)),
                  pl.BlockSpec((1, d), lambda i: (0, 0)),
                  pl.BlockSpec((1, d), lambda i: (0, 0)),
                  pl.BlockSpec(wr_t.shape, lambda i: (0, 0)),
                  pl.BlockSpec(br_col.shape, lambda i: (0, 0))],
        out_specs=[pl.BlockSpec((tm, d), lambda i: (i, 0)),
                   pl.BlockSpec((tm * SUBLANES, LANES), lambda i: (i, 0)),
                   pl.BlockSpec((TOP_K, tm), lambda i: (0, i)),
                   pl.BlockSpec((TOP_K, tm), lambda i: (0, i)),
                   pl.BlockSpec((TOP_K, tm), lambda i: (0, i)),
                   pl.BlockSpec((ne, LANES), lambda i: (0, 0))],
        out_shape=[jax.ShapeDtypeStruct((m, d), F32),
                   jax.ShapeDtypeStruct((m * SUBLANES, LANES), U32),
                   jax.ShapeDtypeStruct((TOP_K, m), jnp.int32),
                   jax.ShapeDtypeStruct((TOP_K, m), F32),
                   jax.ShapeDtypeStruct((TOP_K, m), jnp.int32),
                   jax.ShapeDtypeStruct((ne, LANES), F32)],
        scratch_shapes=[pltpu.VMEM((tm, tm), BF16), pltpu.VMEM((ne, 1), F32)],
        compiler_params=_cparams(("arbitrary",)),
        name="outproj_ln_router",
    )(hmix, hmem, w1, w2, x, g, b, wr_t, br_col)


INVERT_UNROLL = 16


def _invert_kernel(trips_ref, dest_ref, tok_ref, *, n):
    def clear(i, carry):
        for u in range(INVERT_UNROLL):
            tok_ref[i * INVERT_UNROLL + u] = 0
        return carry

    lax.fori_loop(0, trips_ref[0], clear, 0)
    for k in range(TOP_K):
        def put(i, carry, k=k):
            for u in range(INVERT_UNROLL):
                t = i * INVERT_UNROLL + u
                tok_ref[dest_ref[k * n + t]] = t
            return carry

        lax.fori_loop(0, trips_ref[1], put, 0)


def _invert(dest_flat, rows, n):
    assert rows % INVERT_UNROLL == 0 and n % INVERT_UNROLL == 0
    trips = jnp.array([rows // INVERT_UNROLL, n // INVERT_UNROLL], I32)
    return pl.pallas_call(
        functools.partial(_invert_kernel, n=n),
        in_specs=[pl.BlockSpec(memory_space=pltpu.SMEM), pl.BlockSpec(memory_space=pltpu.SMEM)],
        out_specs=pl.BlockSpec(memory_space=pltpu.SMEM),
        out_shape=jax.ShapeDtypeStruct((rows,), I32),
        name="moe_invert",
    )(trips, dest_flat)


ROW_DMA_UNROLL = 8


def _moe_ffn_kernel(sbe_ref, sbrow_ref, sbnsub_ref, meta_ref, tok_ref,
                    x_hbm, wg_ref, wu_ref, wd_ref, bg_ref, bu_ref, bd_ref, ys_hbm,
                    xraw, acc, wg_bf, wu_bf, wd_bf, in_sem, out_sem, *, nblk):
    s = pl.program_id(0)
    f = pl.program_id(1)
    ns = pl.num_programs(0)
    nf = pl.num_programs(1)
    r = MOE_ROW_BLOCK
    half = wg_bf.shape[0] // 2
    nxl = half // LANES
    nsub = sbnsub_ref[s]
    nsub_prev = jnp.where(s > 0, sbnsub_ref[jnp.maximum(s - 1, 0)], 0)
    slot = s % 2

    def row_copy(src_row, sl, dst_row):
        return pltpu.make_async_copy(
            x_hbm.at[pl.ds(pl.multiple_of(src_row * nxl, nxl), nxl), :],
            xraw.at[sl, pl.ds(pl.multiple_of(dst_row * nxl, nxl), nxl), :], in_sem.at[sl])

    def out_copy(ss, sub):
        row0 = sub * r if isinstance(sub, int) else pl.multiple_of(sub * r, r)
        dst = pl.multiple_of(sbrow_ref[ss] + sub * r, r)
        return pltpu.make_async_copy(acc.at[pl.ds(row0, r), :], ys_hbm.at[pl.ds(dst, r), :], out_sem)

    def await_rows(sl, n_rows):
        def land(i, carry):
            for u in range(ROW_DMA_UNROLL):
                row_copy(0, sl, i * ROW_DMA_UNROLL + u).wait()
            return carry
        lax.fori_loop(0, n_rows // ROW_DMA_UNROLL, land, 0)

    def await_outputs(ss, count):
        def flush(sub, carry):
            out_copy(ss, sub).wait()
            return carry
        lax.fori_loop(0, count, flush, 0)

    def request_rows(ss, sl):
        base = sbrow_ref[ss]

        def go(i, carry):
            for u in range(ROW_DMA_UNROLL):
                j = i * ROW_DMA_UNROLL + u
                row_copy(tok_ref[base + j], sl, j).start()
            return carry
        lax.fori_loop(0, sbnsub_ref[ss] * (r // ROW_DMA_UNROLL), go, 0)

    @pl.when(f == 0)
    def _():
        @pl.when(s == 0)
        def _():
            request_rows(0, 0)

        @pl.when(s + 1 < ns)
        def _():
            request_rows(s + 1, 1 - slot)

        await_rows(slot, nsub * r)
        await_outputs(jnp.maximum(s - 1, 0), nsub_prev)

    @pl.when(nsub > 0)
    def _():
        wg_bf[...] = wg_ref[...].astype(BF16)
        wu_bf[...] = wu_ref[...].astype(BF16)
        wd_bf[...] = wd_ref[...].astype(BF16)

        def sub_block(sub, carry):
            row0 = sub * r if isinstance(sub, int) else pl.multiple_of(sub * r, r)
            xu = jnp.concatenate([xraw[slot, pl.ds(row0 * nxl + jj, r, stride=nxl), :]
                                  for jj in range(nxl)], axis=1)
            x_lo, x_hi = _unpack_bf16_pairs(xu)
            hg = _dot(x_lo, wg_bf[0:half, :]) + _dot(x_hi, wg_bf[half:, :]) + bg_ref[...]
            hu = _dot(x_lo, wu_bf[0:half, :]) + _dot(x_hi, wu_bf[half:, :]) + bu_ref[...]
            hg = jnp.minimum(hg, SWIGLU_LIMIT)
            hu = jnp.clip(hu, -SWIGLU_LIMIT, SWIGLU_LIMIT)
            hid = (hu + 1.0) * (hg * jax.nn.sigmoid(SWIGLU_ALPHA * hg))
            part = _dot(hid.astype(BF16), wd_bf[...])

            @pl.when(f == 0)
            def _():
                acc[pl.ds(row0, r), :] = part

            @pl.when((f > 0) & (f < nf - 1))
            def _():
                acc[pl.ds(row0, r), :] += part

            @pl.when(f == nf - 1)
            def _():
                acc[pl.ds(row0, r), :] += part + bd_ref[...]
                out_copy(s, sub).start()

            return carry

        sub_block(0, 0)
        lax.fori_loop(1, nsub, sub_block, 0)

    @pl.when((s == ns - 1) & (f == nf - 1))
    def _():
        await_outputs(s, nsub)
        acc[0:r, :] = jnp.zeros((r, acc.shape[1]), F32)

        def tail_copy(blk):
            return pltpu.make_async_copy(acc.at[pl.ds(0, r), :],
                                         ys_hbm.at[pl.ds(pl.multiple_of(blk * r, r), r), :], out_sem)

        def go(blk, carry):
            tail_copy(blk).start()
            return carry
        lax.fori_loop(meta_ref[1], nblk, go, 0)

        def done(blk, carry):
            tail_copy(blk).wait()
            return carry
        lax.fori_loop(meta_ref[1], nblk, done, 0)


def _moe_ffn(xp, n_tok, row_tok, w_gu, b_gu, w_down, b_down, sbe, sbrow, sbnsub, meta, layer, nblk):
    half = xp.shape[0] * xp.shape[1] // n_tok
    assert half == SUBLANES * LANES, "a packed token must be exactly one (8, 128) tile"
    rows = row_tok.shape[0]
    d = 2 * half
    dff = w_gu.shape[-1] // 2
    tf = MOE_FF_TILE
    nf = dff // tf
    ns = sbe.shape[0]
    bgu = b_gu.reshape(b_gu.shape[0], b_gu.shape[1], 1, 2 * dff)
    bdn = b_down.reshape(b_down.shape[0], b_down.shape[1], 1, d)

    def feff(s, f, meta):
        return jnp.where(s < meta[0], f, nf - 1)

    kern = functools.partial(_moe_ffn_kernel, nblk=nblk)
    return pl.pallas_call(
        kern,
        grid_spec=pltpu.PrefetchScalarGridSpec(
            num_scalar_prefetch=5,
            grid=(ns, nf),
            in_specs=[
                pl.BlockSpec(memory_space=pl.ANY),
                pl.BlockSpec((None, None, d, tf),
                             lambda s, f, se, sr, sn, mt, tk: (layer, se[s], 0, feff(s, f, mt))),
                pl.BlockSpec((None, None, d, tf),
                             lambda s, f, se, sr, sn, mt, tk: (layer, se[s], 0, nf + feff(s, f, mt))),
                pl.BlockSpec((None, None, tf, d),
                             lambda s, f, se, sr, sn, mt, tk: (layer, se[s], feff(s, f, mt), 0)),
                pl.BlockSpec((None, None, 1, tf),
                             lambda s, f, se, sr, sn, mt, tk: (layer, se[s], 0, feff(s, f, mt))),
                pl.BlockSpec((None, None, 1, tf),
                             lambda s, f, se, sr, sn, mt, tk: (layer, se[s], 0, nf + feff(s, f, mt))),
                pl.BlockSpec((None, None, 1, d),
                             lambda s, f, se, sr, sn, mt, tk: (layer, se[s], 0, 0)),
            ],
            out_specs=pl.BlockSpec(memory_space=pl.ANY),
            scratch_shapes=[pltpu.VMEM((2, MOE_SUPER_ROWS * SUBLANES, LANES), U32),
                            pltpu.VMEM((MOE_SUPER_ROWS, d), F32),
                            pltpu.VMEM((d, tf), BF16), pltpu.VMEM((d, tf), BF16),
                            pltpu.VMEM((tf, d), BF16),
                            pltpu.SemaphoreType.DMA((2,)), pltpu.SemaphoreType.DMA(())]),
        out_shape=jax.ShapeDtypeStruct((rows, d), F32),
        compiler_params=_cparams(("arbitrary", "arbitrary")),
        name="moe_ffn",
    )(sbe, sbrow, sbnsub, meta, row_tok, xp, w_gu, w_gu, w_down, bgu, bgu, bdn)


def _combine_kernel(pos_ref, ys_hbm, x_ref, gate_ref, g_ref, b_ref, o_ref, buf, sem):
    i = pl.program_id(0)
    tb = x_ref.shape[0]
    slot = i % 2

    def row_copy(src_row, sl, k, dst_row):
        return pltpu.make_async_copy(ys_hbm.at[pl.ds(src_row, 1), :],
                                     buf.at[sl, k, pl.ds(dst_row, 1), :], sem.at[sl])

    def request(blk, sl):
        base = blk * (tb * TOP_K)

        def go(r, carry):
            for k in range(TOP_K):
                row_copy(pos_ref[base + r * TOP_K + k], sl, k, r).start()
            return carry
        lax.fori_loop(0, tb, go, 0, unroll=4)

    @pl.when(i == 0)
    def _():
        request(0, 0)

    @pl.when(i + 1 < pl.num_programs(0))
    def _():
        request(i + 1, 1 - slot)

    def land(r, carry):
        for k in range(TOP_K):
            row_copy(0, slot, k, r).wait()
        return carry

    lax.fori_loop(0, tb, land, 0, unroll=4)

    gate = gate_ref[...]
    f = gate[:, 0:1] * buf[slot, 0]
    for k in range(1, TOP_K):
        f = f + gate[:, k:k + 1] * buf[slot, k]
    o_ref[...] = _layer_norm_rows(DEEPNORM_ALPHA * x_ref[...] + f, g_ref[...], b_ref[...])


def _combine(ys, pos, gate, x, g, b, tb):
    n, d = x.shape
    return pl.pallas_call(
        _combine_kernel,
        grid_spec=pltpu.PrefetchScalarGridSpec(
            num_scalar_prefetch=1,
            grid=(n // tb,),
            in_specs=[pl.BlockSpec(memory_space=pl.ANY),
                      pl.BlockSpec((tb, d), lambda i, ps: (i, 0)),
                      pl.BlockSpec((tb, TOP_K), lambda i, ps: (i, 0)),
                      pl.BlockSpec((1, d), lambda i, ps: (0, 0)),
                      pl.BlockSpec((1, d), lambda i, ps: (0, 0))],
            out_specs=pl.BlockSpec((tb, d), lambda i, ps: (i, 0)),
            scratch_shapes=[pltpu.VMEM((2, TOP_K, tb, d), F32), pltpu.SemaphoreType.DMA((2,))]),
        out_shape=jax.ShapeDtypeStruct((n, d), F32),
        compiler_params=_cparams(("arbitrary",)),
        name="moe_combine",
    )(pos.reshape(-1), ys, x, gate, g, b)


def _dispatch_plan(top_e_t, rank_t, cnt):
    n = top_e_t.shape[1]
    r = MOE_ROW_BLOCK
    sb = MOE_SUPER_ROWS
    nblk = -(-(n * TOP_K) // r) + N_EXPERTS
    counts = cnt[:, 0].astype(I32)
    padded = (counts + r - 1) // r * r
    pend = jnp.cumsum(padded)
    pstart = pend - padded
    eids = jnp.arange(N_EXPERTS, dtype=I32)
    base = jnp.sum(jnp.where(top_e_t[:, :, None] == eids[None, None, :],
                             pstart[None, None, :], 0), axis=-1)
    dest_t = base + rank_t
    nused = pend[-1] // r
    sbb = sb // r
    nb_e = padded // r
    nsb = (nb_e + sbb - 1) // sbb
    per = (nb_e + jnp.maximum(nsb, 1) - 1) // jnp.maximum(nsb, 1)
    sb_end = jnp.cumsum(nsb)
    sb_start = sb_end - nsb
    total = sb_end[-1]
    n_sb_max = nblk // sbb + N_EXPERTS
    sid = jnp.arange(n_sb_max, dtype=I32)
    used = sid < total
    e_of = jnp.minimum(jnp.sum(sb_end[None, :] <= jnp.minimum(sid, total - 1)[:, None], axis=1),
                       N_EXPERTS - 1).astype(I32)
    idx = jnp.minimum(sid, total - 1) - sb_start[e_of]
    row0 = jnp.where(used, pstart[e_of] + idx * per[e_of] * r, 0).astype(I32)
    nsub = jnp.where(used, jnp.clip(nb_e[e_of] - idx * per[e_of], 0, per[e_of]), 0).astype(I32)
    meta = jnp.stack([total, nused]).astype(I32)
    return dest_t.astype(I32), meta, e_of, row0, nsub, nblk


def _moe_and_norm(x1, x1p, top_e_t, gate_t, rank_t, cnt, w_gu, b_gu, w_down, b_down, g, b, layer):
    dest_t, meta, sbe, sbrow, sbnsub, nblk = _dispatch_plan(top_e_t, rank_t, cnt)
    row_tok = _invert(dest_t.reshape(-1), nblk * MOE_ROW_BLOCK, dest_t.shape[1])
    ys = _moe_ffn(x1p, x1.shape[0], row_tok, w_gu, b_gu, w_down, b_down, sbe, sbrow, sbnsub, meta, layer, nblk)
    return _combine(ys, dest_t.T, gate_t.T, x1, g, b, tb=128)


def _rotary_tables(s):
    half = ROPE_DIM // 2
    inv_freq = ROPE_THETA ** (-jnp.arange(half, dtype=F32) * 2.0 / ROPE_DIM)
    ang = jnp.arange(s, dtype=F32)[:, None] * inv_freq[None, :]
    cos, sin = jnp.cos(ang), jnp.sin(ang)
    rest = HEAD_DIM - ROPE_DIM
    cos_t = jnp.concatenate([cos, cos, jnp.ones((s, rest), F32)], axis=1)
    s1_t = jnp.concatenate([-sin, jnp.zeros((s, half + rest), F32)], axis=1)
    s2_t = jnp.concatenate([jnp.zeros((s, half), F32), sin, jnp.zeros((s, rest), F32)], axis=1)
    return cos_t, s1_t, s2_t


def kernel(x, mem, mlstm_w_in, mlstm_conv_w, mlstm_conv_b, mlstm_b_igate, mlstm_b_fgate, moba_w_in, w_mem_kv, w_out, ln1_g, ln1_b, w_router, b_router, w_gu, b_gu, w_down, b_down, ln2_g, ln2_b):
    bsz, s, d = x.shape
    n = bsz * s
    n_mem = mem.shape[1]
    mix_w = d - MEM_WIDTH
    nh_ml = MLSTM_HEADS
    dv = mix_w // nh_ml
    dqk = dv // 2
    qkw2 = 2 * nh_ml * dqk
    nh_mb = mix_w // HEAD_DIM
    qm_block = 3 * mix_w // MEM_WIDTH
    assert qkw2 == mix_w and 3 * mix_w % MEM_WIDTH == 0

    xf = x.reshape(n, d)
    memf = mem.reshape(bsz * n_mem, d)
    for i in range(DEPTH):
        j = i // 2
        if i % 2 == 0:
            w_in = mlstm_w_in[j]
            gate_lo = qkw2 + 2 * mix_w
            gate_hi = gate_lo + 2 * nh_ml
            w_main = jnp.concatenate([w_in[:, :gate_lo], w_in[:, gate_hi:]], axis=1).astype(BF16)
            w_gate = jnp.pad(w_in[:, gate_lo:gate_hi], ((0, 0), (0, LANES - 2 * nh_ml)))
            z = _matmul(xf, w_main, F32, tm=1024, tn=512).reshape(bsz, s, -1)
            gates = _matmul3(xf, w_gate, tm=1024).reshape(bsz, s, LANES)
            gates_t = jnp.swapaxes(gates[:, :, :2 * SUBLANES], 1, 2)
            bias = jnp.concatenate([mlstm_b_igate[j], mlstm_b_fgate[j]])
            bias_row = jnp.pad(bias, (0, LANES - 2 * nh_ml)).reshape(1, LANES)
            bias_col = jnp.pad(bias, (0, 2 * SUBLANES - 2 * nh_ml)).reshape(2 * SUBLANES, 1)
            h_mix = _mlstm(z, gates, gates_t, mlstm_conv_w[j], mlstm_conv_b[j].reshape(1, -1),
                           bias_row, bias_col, nh=nh_ml, dqk=dqk, dv=dv, chunk=MLSTM_CHUNK)
        else:
            z = _matmul(xf, moba_w_in[j].astype(BF16), F32, tm=1024, tn=512).reshape(bsz, s, -1)
            cos_t, s1_t, s2_t = _rotary_tables(s)
            q_r, k_r, v_b, kbar = _moba_prep(z, cos_t, s1_t, s2_t, nh_mb)
            h_mix = _moba_attn(q_r, k_r, v_b, kbar.reshape(bsz, s // MOBA_BLOCK, mix_w), nh_mb,
                               hg=MOBA_HEAD_GROUP)
        kv = _matmul(memf, w_mem_kv[i].astype(BF16), BF16, tm=1024, tn=512)
        h_mem = _mem_attn(z, kv.reshape(bsz, n_mem, 2 * MEM_WIDTH), qm_block, tq=512)
        w_o = w_out[i].astype(BF16)
        x1, x1p, top_e_t, gate_t, rank_t, cnt = _outproj(
            h_mix.reshape(n, mix_w), h_mem.reshape(n, MEM_WIDTH), w_o[:mix_w], w_o[mix_w:], xf,
            ln1_g[i].reshape(1, d), ln1_b[i].reshape(1, d),
            w_router[i].T, b_router[i].reshape(N_EXPERTS, 1), tm=512)
        xf = _moe_and_norm(x1, x1p, top_e_t, gate_t, rank_t, cnt, w_gu, b_gu, w_down, b_down,
                           ln2_g[i].reshape(1, d), ln2_b[i].reshape(1, d), i)
    return xf.reshape(bsz, s, d)
```

```python
import functools

import jax
import jax.numpy as jnp
from jax import lax
from jax.experimental import pallas as pl
from jax.experimental.pallas import tpu as pltpu

F32 = jnp.float32
BF16 = jnp.bfloat16
U32 = jnp.uint32
I32 = jnp.int32

HEAD_DIM = 128
MEM_HEADS = 4
MEM_WIDTH = MEM_HEADS * HEAD_DIM
MLSTM_HEADS = 6
CONV_WIDTH = 4
MOBA_BLOCK = 256
MOBA_TOPK = 3
ROPE_THETA = 500000.0
ROPE_DIM = HEAD_DIM // 4
N_EXPERTS = 32
TOP_K = 4
SWIGLU_LIMIT = 7.0
SWIGLU_ALPHA = 1.702
LN_EPS = 1e-5
DEPTH = 2
DEEPNORM_ALPHA = (2 * DEPTH) ** 0.25

LANES = 128
SUBLANES = 8
VMEM_LIMIT_BYTES = 56 * 1024 * 1024
MLSTM_CHUNK = 128
MOE_ROW_BLOCK = 256
MOBA_HEAD_GROUP = 4
MOE_SUPER_ROWS = 1280
MOE_FF_TILE = 512
NEG_BIG = -1e30


def _cparams(sem):
    return pltpu.CompilerParams(dimension_semantics=sem, vmem_limit_bytes=VMEM_LIMIT_BYTES)


def _dot(a, b):
    return jnp.dot(a, b, preferred_element_type=F32)


def _dot_nt(a, b):
    return lax.dot_general(a, b, (((1,), (1,)), ((), ())), preferred_element_type=F32)


def _dot_tn(a, b):
    return lax.dot_general(a, b, (((0,), (0,)), ((), ())), preferred_element_type=F32)


def _split_bf16(x):
    hi = x.astype(BF16)
    lo = (x - hi.astype(F32)).astype(BF16)
    return hi, lo


def _layer_norm_rows(r, g, b):
    mu = jnp.mean(r, axis=-1, keepdims=True)
    d = r - mu
    var = jnp.mean(d * d, axis=-1, keepdims=True)
    return d * lax.rsqrt(var + LN_EPS) * g + b


def _pack_bf16_pairs(x):
    w = x.shape[1] // 2
    lo = pltpu.bitcast(x[:, :w].astype(BF16).astype(F32), U32)
    hi = pltpu.bitcast(x[:, w:].astype(BF16).astype(F32), U32)
    return (hi & jnp.uint32(0xFFFF0000)) | (lo >> 16)


def _unpack_bf16_pairs(p):
    lo = pltpu.bitcast(p << 16, F32).astype(BF16)
    hi = pltpu.bitcast(p & jnp.uint32(0xFFFF0000), F32).astype(BF16)
    return lo, hi


def _matmul_kernel(a_ref, b_ref, o_ref, a_bf_ref):
    @pl.when(pl.program_id(1) == 0)
    def _():
        a_bf_ref[...] = a_ref[...].astype(BF16)

    o_ref[...] = _dot(a_bf_ref[...], b_ref[...]).astype(o_ref.dtype)


def _matmul(a, b, out_dtype, tm, tn):
    m, k = a.shape
    n = b.shape[1]
    return pl.pallas_call(
        _matmul_kernel,
        grid=(m // tm, n // tn),
        in_specs=[pl.BlockSpec((tm, k), lambda i, j: (i, 0)),
                  pl.BlockSpec((k, tn), lambda i, j: (0, j))],
        out_specs=pl.BlockSpec((tm, tn), lambda i, j: (i, j)),
        out_shape=jax.ShapeDtypeStruct((m, n), out_dtype),
        scratch_shapes=[pltpu.VMEM((tm, k), BF16)],
        compiler_params=_cparams(("parallel", "arbitrary")),
        name="matmul",
    )(a, b)


def _matmul3_kernel(a_ref, b_ref, o_ref):
    a_hi, a_lo = _split_bf16(a_ref[...])
    b_hi, b_lo = _split_bf16(b_ref[...])
    o_ref[...] = _dot(a_hi, b_hi) + _dot(a_hi, b_lo) + _dot(a_lo, b_hi)


def _matmul3(a, b, tm):
    m, k = a.shape
    n = b.shape[1]
    return pl.pallas_call(
        _matmul3_kernel,
        grid=(m // tm,),
        in_specs=[pl.BlockSpec((tm, k), lambda i: (i, 0)),
                  pl.BlockSpec((k, n), lambda i: (0, 0))],
        out_specs=pl.BlockSpec((tm, n), lambda i: (i, 0)),
        out_shape=jax.ShapeDtypeStruct((m, n), F32),
        compiler_params=_cparams(("parallel",)),
        name="matmul3",
    )(a, b)


def _log_sigmoid(x):
    return jnp.minimum(x, 0.0) - jnp.log1p(jnp.exp(-jnp.abs(x)))


def _mlstm_kernel(qk_ref, v_ref, og_ref, g_ref, gt_ref, cw_ref, cb_ref, bias_ref, biast_ref,
                  out_ref, ext_ref, c_ref, n_ref, m_ref, *, chunk, nh, dqk, dv):
    L = chunk
    qkw = nh * dqk

    @pl.when(pl.program_id(1) == 0)
    def _():
        ext_ref[0:SUBLANES, :] = jnp.zeros((SUBLANES, 2 * qkw), F32)
        c_ref[...] = jnp.zeros_like(c_ref)
        n_ref[...] = jnp.zeros_like(n_ref)
        m_ref[...] = jnp.zeros_like(m_ref)

    ext_ref[SUBLANES:SUBLANES + L, :] = qk_ref[...].astype(F32)
    cw = cw_ref[...]
    y = jnp.broadcast_to(cb_ref[...], (L, 2 * qkw))
    for w in range(CONV_WIDTH):
        y = y + ext_ref[pl.ds(SUBLANES - (CONV_WIDTH - 1) + w, L), :] * cw[w:w + 1, :]
    ext_ref[0:SUBLANES, :] = ext_ref[L:L + SUBLANES, :]
    qk = y * jax.nn.sigmoid(y)

    gates = g_ref[...] + bias_ref[...]
    gates_t = gt_ref[...] + biast_ref[...]
    row = lax.broadcasted_iota(jnp.int32, (L, L), 0)
    col = lax.broadcasted_iota(jnp.int32, (L, L), 1)
    causal = col <= row

    for h in range(nh):
        q = qk[:, h * dqk:(h + 1) * dqk]
        k = qk[:, qkw + h * dqk:qkw + (h + 1) * dqk] * (dqk ** -0.5)
        v_bf = v_ref[:, h * dv:(h + 1) * dv].astype(BF16)
        q_bf = q.astype(BF16)
        ig_c = gates[:, h:h + 1]
        ig_r = gates_t[h:h + 1, :]
        lf_c = _log_sigmoid(gates[:, nh + h:nh + h + 1])
        lf_r = _log_sigmoid(gates_t[nh + h:nh + h + 1, :])
        bcum_c = jnp.sum(jnp.where(causal, lf_r, 0.0), axis=1, keepdims=True)
        bcum_r = jnp.sum(jnp.where(row <= col, lf_c, 0.0), axis=0, keepdims=True)
        g_tot = jnp.sum(lf_r, axis=1, keepdims=True)
        m_prev = m_ref[h:h + 1, 0:1]

        dmat = jnp.where(causal, bcum_c - bcum_r + ig_r, -jnp.inf)
        inter_log = bcum_c + m_prev
        m_t = jnp.maximum(inter_log, jnp.max(dmat, axis=1, keepdims=True))
        s_qk = _dot_nt(q_bf, k.astype(BF16)) * jnp.exp(dmat - m_t)
        inter_w = jnp.exp(inter_log - m_t)
        c_prev = c_ref[h]
        n_prev = n_ref[h:h + 1, :]
        num = inter_w * _dot(q_bf, c_prev.astype(BF16)) + _dot(s_qk.astype(BF16), v_bf)
        den = inter_w * jnp.sum(q * n_prev, axis=1, keepdims=True) \
            + jnp.sum(s_qk, axis=1, keepdims=True)
        hval = num / jnp.maximum(jnp.abs(den), jnp.exp(-m_t))
        o_gate = jax.nn.sigmoid(og_ref[:, h * dv:(h + 1) * dv].astype(F32))
        out_ref[:, h * dv:(h + 1) * dv] = (o_gate * hval).astype(out_ref.dtype)

        a_c = g_tot - bcum_c + ig_c
        m_new = jnp.maximum(g_tot + m_prev, jnp.max(a_c, axis=0, keepdims=True))
        decay = jnp.exp(g_tot + m_prev - m_new)
        kw = k * jnp.exp(a_c - m_new)
        c_ref[h] = decay * c_prev + _dot_tn(kw.astype(BF16), v_bf)
        n_ref[h:h + 1, :] = decay * n_prev + jnp.sum(kw, axis=0, keepdims=True)
        m_ref[h:h + 1, :] = jnp.broadcast_to(m_new, (1, LANES))


def _mlstm(z, gates, gates_t, conv_w, conv_b, bias_row, bias_col, *, nh, dqk, dv, chunk):
    bsz, s, _ = z.shape
    qkw2 = 2 * nh * dqk
    vw = nh * dv
    assert qkw2 == vw, "q|k, v and output-gate column blocks must share one block width"
    nc = s // chunk
    kern = functools.partial(_mlstm_kernel, chunk=chunk, nh=nh, dqk=dqk, dv=dv)
    return pl.pallas_call(
        kern,
        grid=(bsz, nc),
        in_specs=[
            pl.BlockSpec((None, chunk, vw), lambda b, c: (b, c, 0)),
            pl.BlockSpec((None, chunk, vw), lambda b, c: (b, c, 1)),
            pl.BlockSpec((None, chunk, vw), lambda b, c: (b, c, 2)),
            pl.BlockSpec((None, chunk, LANES), lambda b, c: (b, c, 0)),
            pl.BlockSpec((None, 2 * SUBLANES, chunk), lambda b, c: (b, 0, c)),
            pl.BlockSpec((CONV_WIDTH, qkw2), lambda b, c: (0, 0)),
            pl.BlockSpec((1, qkw2), lambda b, c: (0, 0)),
            pl.BlockSpec((1, LANES), lambda b, c: (0, 0)),
            pl.BlockSpec((2 * SUBLANES, 1), lambda b, c: (0, 0)),
        ],
        out_specs=pl.BlockSpec((None, chunk, vw), lambda b, c: (b, c, 0)),
        out_shape=jax.ShapeDtypeStruct((bsz, s, vw), BF16),
        scratch_shapes=[
            pltpu.VMEM((chunk + SUBLANES, qkw2), F32),
            pltpu.VMEM((nh, dqk, dv), F32),
            pltpu.VMEM((SUBLANES, dqk), F32),
            pltpu.VMEM((SUBLANES, LANES), F32),
        ],
        compiler_params=_cparams(("parallel", "arbitrary")),
        name="mlstm",
    )(z, z, z, gates, gates_t, conv_w, conv_b, bias_row, bias_col)


def _mem_attn_kernel(q_ref, k_ref, v_ref, o_ref):
    scale = HEAD_DIM ** -0.5
    for h in range(MEM_HEADS):
        sl = slice(h * HEAD_DIM, (h + 1) * HEAD_DIM)
        q = q_ref[:, sl].astype(BF16)
        sc = _dot_nt(q, k_ref[:, sl]) * scale
        sc = sc - jnp.max(sc, axis=1, keepdims=True)
        p = jnp.exp(sc)
        p = p / jnp.sum(p, axis=1, keepdims=True)
        o_ref[:, sl] = _dot(p.astype(BF16), v_ref[:, sl]).astype(o_ref.dtype)


def _mem_attn(z, kv, qm_block, tq):
    bsz, s, _ = z.shape
    n_mem = kv.shape[1]
    return pl.pallas_call(
        _mem_attn_kernel,
        grid=(bsz, s // tq),
        in_specs=[
            pl.BlockSpec((None, tq, MEM_WIDTH), lambda b, i: (b, i, qm_block)),
            pl.BlockSpec((None, n_mem, MEM_WIDTH), lambda b, i: (b, 0, 0)),
            pl.BlockSpec((None, n_mem, MEM_WIDTH), lambda b, i: (b, 0, 1)),
        ],
        out_specs=pl.BlockSpec((None, tq, MEM_WIDTH), lambda b, i: (b, i, 0)),
        out_shape=jax.ShapeDtypeStruct((bsz, s, MEM_WIDTH), BF16),
        compiler_params=_cparams(("parallel", "parallel")),
        name="mem_attn",
    )(z, kv, kv)


def _moba_prep_kernel(q_ref, k_ref, cos_ref, s1_ref, s2_ref, qo_ref, ko_ref, kbar_ref, *, nh):
    cosf = cos_ref[...]
    s1 = s1_ref[...]
    s2 = s2_ref[...]
    half = ROPE_DIM // 2
    for h in range(nh):
        sl = slice(h * HEAD_DIM, (h + 1) * HEAD_DIM)
        for src, dst, is_k in ((q_ref, qo_ref, False), (k_ref, ko_ref, True)):
            x = src[:, sl].astype(F32)
            xr = (x * cosf + pltpu.roll(x, HEAD_DIM - half, 1) * s1
                  + pltpu.roll(x, half, 1) * s2)
            if is_k:
                dst[:, sl] = xr.astype(dst.dtype)
                kbar_ref[:, sl] = jnp.mean(xr, axis=0, keepdims=True)
            else:
                dst[:, sl] = (xr * (HEAD_DIM ** -0.5)).astype(dst.dtype)


def _moba_prep(z, cos_t, s1_t, s2_t, nh):
    bsz, s, _ = z.shape
    w = nh * HEAD_DIM
    nb = s // MOBA_BLOCK
    t = MOBA_BLOCK
    kern = functools.partial(_moba_prep_kernel, nh=nh)
    tab = pl.BlockSpec((t, HEAD_DIM), lambda b, i: (i, 0))
    big = jax.ShapeDtypeStruct((bsz, s, w), BF16)
    return pl.pallas_call(
        kern,
        grid=(bsz, nb),
        in_specs=[pl.BlockSpec((None, t, w), lambda b, i: (b, i, 0)),
                  pl.BlockSpec((None, t, w), lambda b, i: (b, i, 1)),
                  tab, tab, tab],
        out_specs=[pl.BlockSpec((None, t, w), lambda b, i: (b, i, 0)),
                   pl.BlockSpec((None, t, w), lambda b, i: (b, i, 0)),
                   pl.BlockSpec((None, None, 1, w), lambda b, i: (b, i, 0, 0))],
        out_shape=[big, big, jax.ShapeDtypeStruct((bsz, nb, 1, w), F32)],
        compiler_params=_cparams(("parallel", "parallel")),
        name="moba_prep",
    )(z, z, cos_t, s1_t, s2_t)


def _moba_attn_kernel(q_ref, k_ref, v_ref, kbar_ref, o_ref, *, nb, hg):
    t = MOBA_BLOCK
    qb = pl.program_id(2)
    blk = lax.broadcasted_iota(jnp.int32, (nb, t), 0)
    valid = blk < qb
    pad_rows = 2 * SUBLANES - nb
    eye = jnp.where(lax.broadcasted_iota(jnp.int32, (nb + pad_rows, LANES), 0)
                    == lax.broadcasted_iota(jnp.int32, (nb + pad_rows, LANES), 1),
                    1.0, 0.0).astype(BF16)

    qs = []
    sel_biases = []
    for hh in range(hg):
        sl = slice(hh * HEAD_DIM, (hh + 1) * HEAD_DIM)
        q = q_ref[:, sl]
        kb_hi, kb_lo = _split_bf16(kbar_ref[:, sl])
        gate = _dot_nt(kb_hi, q) + _dot_nt(kb_lo, q)
        gm = jnp.where(valid, gate, -jnp.inf)
        rank = jnp.zeros((nb, t), jnp.int32)
        for n2 in range(nb):
            gc = gm[n2:n2 + 1, :]
            beats = (gc > gm) | ((gc == gm) & (n2 < blk))
            rank = rank + beats.astype(jnp.int32)
        bias_t = jnp.where(valid & (rank < MOBA_TOPK), 0.0, NEG_BIG)
        bias_t = jnp.concatenate([bias_t, jnp.zeros((pad_rows, t), F32)], axis=0).astype(BF16)
        sel_biases.append(_dot_tn(bias_t, eye))
        qs.append(q)

    row = lax.broadcasted_iota(jnp.int32, (t, t), 0)
    col = lax.broadcasted_iota(jnp.int32, (t, t), 1)
    causal = col <= row

    for c in range(nb):
        @pl.when(qb == c)
        def _(c=c):
            for hh in range(hg):
                sl = slice(hh * HEAD_DIM, (hh + 1) * HEAD_DIM)
                pieces = []
                for n in range(c + 1):
                    s = _dot_nt(qs[hh], k_ref[n * t:(n + 1) * t, sl])
                    if n == c:
                        pieces.append(jnp.where(causal, s, NEG_BIG))
                    else:
                        pieces.append(s + sel_biases[hh][:, n:n + 1])
                mx = pieces[0]
                for piece in pieces[1:]:
                    mx = jnp.maximum(mx, piece)
                m = jnp.max(mx, axis=1, keepdims=True)
                psum = None
                acc = None
                for n in range(c + 1):
                    p = jnp.exp(pieces[n] - m)
                    pv = _dot(p.astype(BF16), v_ref[n * t:(n + 1) * t, sl])
                    psum = p if psum is None else psum + p
                    acc = pv if acc is None else acc + pv
                l = jnp.sum(psum, axis=1, keepdims=True)
                o_ref[:, sl] = (acc / l).astype(o_ref.dtype)


def _moba_attn(q, k, z, v_col, kbar, nh, hg):
    bsz, s, _ = q.shape
    nb = s // MOBA_BLOCK
    t = MOBA_BLOCK
    w = hg * HEAD_DIM
    assert v_col % w == 0
    v_blk = v_col // w
    kern = functools.partial(_moba_attn_kernel, nb=nb, hg=hg)
    return pl.pallas_call(
        kern,
        grid=(bsz, nh // hg, nb),
        in_specs=[pl.BlockSpec((None, t, w), lambda b, h, i: (b, i, h)),
                  pl.BlockSpec((None, s, w), lambda b, h, i: (b, 0, h)),
                  pl.BlockSpec((None, s, w), lambda b, h, i: (b, 0, v_blk + h)),
                  pl.BlockSpec((None, nb, w), lambda b, h, i: (b, 0, h))],
        out_specs=pl.BlockSpec((None, t, w), lambda b, h, i: (b, i, h)),
        out_shape=jax.ShapeDtypeStruct((bsz, s, nh * HEAD_DIM), BF16),
        compiler_params=_cparams(("parallel", "parallel", "arbitrary")),
        name="moba_attn",
    )(q, k, z, kbar)


def _outproj_kernel(hmix_ref, hmem_ref, w1_ref, w2_ref, x_ref, g_ref, b_ref, wr_ref, br_ref,
                    x1_ref, x1p_ref, tope_ref, gate_ref, rank_ref, cnt_ref, tri_ref, run_ref):
    i = pl.program_id(0)
    y = _dot(hmix_ref[...], w1_ref[...]) + _dot(hmem_ref[...], w2_ref[...])
    x1 = _layer_norm_rows(DEEPNORM_ALPHA * x_ref[...] + y, g_ref[...], b_ref[...])
    x1_ref[...] = x1
    packed = _pack_bf16_pairs(x1)
    for jj in range(packed.shape[1] // LANES):
        x1p_ref[pl.ds(jj, packed.shape[0], stride=SUBLANES), :] = packed[:, jj * LANES:(jj + 1) * LANES]

    x_hi, x_lo = _split_bf16(x1)
    w_hi, w_lo = _split_bf16(wr_ref[...])
    logits = _dot_nt(w_hi, x_hi) + _dot_nt(w_hi, x_lo) + _dot_nt(w_lo, x_hi) + br_ref[...]
    ne, tm = logits.shape

    @pl.when(i == 0)
    def _():
        r = lax.broadcasted_iota(jnp.int32, (tm, tm), 0)
        c = lax.broadcasted_iota(jnp.int32, (tm, tm), 1)
        tri_ref[...] = jnp.where(r <= c, 1.0, 0.0).astype(BF16)
        run_ref[...] = jnp.zeros_like(run_ref)

    eid = lax.broadcasted_iota(jnp.int32, (ne, tm), 0)
    run = run_ref[...]
    vals = []
    for k in range(TOP_K):
        mx = jnp.max(logits, axis=0, keepdims=True)
        idx = jnp.min(jnp.where(logits == mx, eid, ne), axis=0, keepdims=True)
        tope_ref[k:k + 1, :] = idx
        vals.append(mx)
        hit = eid == idx
        logits = jnp.where(hit, -jnp.inf, logits)
        incl = _dot(jnp.where(hit, 1.0, 0.0).astype(BF16), tri_ref[...])
        rank = jnp.sum(jnp.where(hit, run + incl - 1.0, 0.0), axis=0, keepdims=True)
        rank_ref[k:k + 1, :] = rank.astype(jnp.int32)
        run = run + incl[:, tm - 1:tm]
    run_ref[...] = run
    cnt_ref[...] = jnp.broadcast_to(run, cnt_ref.shape)
    ex = [jnp.exp(vk - vals[0]) for vk in vals]
    tot = ex[0] + ex[1] + ex[2] + ex[3]
    for k in range(TOP_K):
        gate_ref[k:k + 1, :] = ex[k] / tot


def _outproj(hmix, hmem, w1, w2, x, g, b, wr_t, br_col, tm):
    m, d = x.shape
    ne = wr_t.shape[0]
    return pl.pallas_call(
        _outproj_kernel,
        grid=(m // tm,),
        in_specs=[pl.BlockSpec((tm, hmix.shape[1]), lambda i: (i, 0)),
                  pl.BlockSpec((tm, hmem.shape[1]), lambda i: (i, 0)),
                  pl.BlockSpec(w1.shape, lambda i: (0, 0)),
                  pl.BlockSpec(w2.shape, lambda i: (0, 0)),
                  pl.BlockSpec((tm, d), lambda i: (i, 0)),
                  pl.BlockSpec((1, d), lambda i: (0, 0)),
                  pl.BlockSpec((1, d), lambda i: (0, 0)),
                  pl.BlockSpec(wr_t.shape, lambda i: (0, 0)),
                  pl.BlockSpec(br_col.shape, lambda i: (0, 0))],
        out_specs=[pl.BlockSpec((tm, d), lambda i: (i, 0)),
                   pl.BlockSpec((tm * SUBLANES, LANES), lambda i: (i, 0)),
                   pl.BlockSpec((TOP_K, tm), lambda i: (0, i)),
                   pl.BlockSpec((TOP_K, tm), lambda i: (0, i)),
                   pl.BlockSpec((TOP_K, tm), lambda i: (0, i)),
                   pl.BlockSpec((ne, LANES), lambda i: (0, 0))],
        out_shape=[jax.ShapeDtypeStruct((m, d), F32),
                   jax.ShapeDtypeStruct((m * SUBLANES, LANES), U32),
                   jax.ShapeDtypeStruct((TOP_K, m), jnp.int32),
                   jax.ShapeDtypeStruct((TOP_K, m), F32),
                   jax.ShapeDtypeStruct((TOP_K, m), jnp.int32),
                   jax.ShapeDtypeStruct((ne, LANES), F32)],
        scratch_shapes=[pltpu.VMEM((tm, tm), BF16), pltpu.VMEM((ne, 1), F32)],
        compiler_params=_cparams(("arbitrary",)),
        name="outproj_ln_router",
    )(hmix, hmem, w1, w2, x, g, b, wr_t, br_col)


INVERT_UNROLL = 16


def _invert_kernel(trips_ref, dest_ref, tok_ref, *, n):
    def clear(i, carry):
        for u in range(INVERT_UNROLL):
            tok_ref[i * INVERT_UNROLL + u] = 0
        return carry

    lax.fori_loop(0, trips_ref[0], clear, 0)
    for k in range(TOP_K):
        def put(i, carry, k=k):
            for u in range(INVERT_UNROLL):
                t = i * INVERT_UNROLL + u
                tok_ref[dest_ref[k * n + t]] = t
            return carry

        lax.fori_loop(0, trips_ref[1], put, 0)


def _invert(dest_flat, rows, n):
    assert rows % INVERT_UNROLL == 0 and n % INVERT_UNROLL == 0
    trips = jnp.array([rows // INVERT_UNROLL, n // INVERT_UNROLL], I32)
    return pl.pallas_call(
        functools.partial(_invert_kernel, n=n),
        in_specs=[pl.BlockSpec(memory_space=pltpu.SMEM), pl.BlockSpec(memory_space=pltpu.SMEM)],
        out_specs=pl.BlockSpec(memory_space=pltpu.SMEM),
        out_shape=jax.ShapeDtypeStruct((rows,), I32),
        name="moe_invert",
    )(trips, dest_flat)


ROW_DMA_UNROLL = 8


def _moe_ffn_kernel(sbe_ref, sbrow_ref, sbnsub_ref, meta_ref, tok_ref,
                    x_hbm, wg_ref, wu_ref, wd_ref, bg_ref, bu_ref, bd_ref, ys_hbm,
                    xraw, acc, wgu_bf, wd_bf, in_sem, out_sem, *, nblk):
    s = pl.program_id(0)
    f = pl.program_id(1)
    ns = pl.num_programs(0)
    nf = pl.num_programs(1)
    r = MOE_ROW_BLOCK
    nxl = wgu_bf.shape[0] // (2 * LANES)
    nsub = sbnsub_ref[s]
    nsub_prev = jnp.where(s > 0, sbnsub_ref[jnp.maximum(s - 1, 0)], 0)
    slot = s % 2

    def row_copy(src_row, sl, dst_row):
        return pltpu.make_async_copy(
            x_hbm.at[pl.ds(pl.multiple_of(src_row * nxl, nxl), nxl), :],
            xraw.at[sl, pl.ds(pl.multiple_of(dst_row * nxl, nxl), nxl), :], in_sem.at[sl])

    def out_copy(ss, sub):
        row0 = sub * r if isinstance(sub, int) else pl.multiple_of(sub * r, r)
        dst = pl.multiple_of(sbrow_ref[ss] + sub * r, r)
        return pltpu.make_async_copy(acc.at[pl.ds(row0, r), :], ys_hbm.at[pl.ds(dst, r), :], out_sem)

    def await_rows(sl, n_rows):
        def land(i, carry):
            for u in range(ROW_DMA_UNROLL):
                row_copy(0, sl, i * ROW_DMA_UNROLL + u).wait()
            return carry
        lax.fori_loop(0, n_rows // ROW_DMA_UNROLL, land, 0)

    def await_outputs(ss, count):
        def flush(sub, carry):
            out_copy(ss, sub).wait()
            return carry
        lax.fori_loop(0, count, flush, 0)

    def request_rows(ss, sl):
        base = sbrow_ref[ss]

        def go(i, carry):
            for u in range(ROW_DMA_UNROLL):
                j = i * ROW_DMA_UNROLL + u
                row_copy(tok_ref[base + j], sl, j).start()
            return carry
        lax.fori_loop(0, sbnsub_ref[ss] * (r // ROW_DMA_UNROLL), go, 0)

    @pl.when(f == 0)
    def _():
        @pl.when(s == 0)
        def _():
            request_rows(0, 0)

        @pl.when(s + 1 < ns)
        def _():
            request_rows(s + 1, 1 - slot)

        await_rows(slot, nsub * r)
        await_outputs(jnp.maximum(s - 1, 0), nsub_prev)

    @pl.when(nsub > 0)
    def _():
        tf = wd_bf.shape[0]
        wgu_bf[:, 0:tf] = wg_ref[...].astype(BF16)
        wgu_bf[:, tf:] = wu_ref[...].astype(BF16)
        wd_bf[...] = wd_ref[...].astype(BF16)

        def sub_block(sub, carry):
            row0 = sub * r if isinstance(sub, int) else pl.multiple_of(sub * r, r)
            xu = jnp.concatenate([xraw[slot, pl.ds(row0 * nxl + jj, r, stride=nxl), :]
                                  for jj in range(nxl)], axis=1)
            x = jnp.concatenate(_unpack_bf16_pairs(xu), axis=1)
            h = _dot(x, wgu_bf[...])
            hg = h[:, 0:tf] + bg_ref[...]
            hu = h[:, tf:] + bu_ref[...]
            hg = jnp.minimum(hg, SWIGLU_LIMIT)
            hu = jnp.clip(hu, -SWIGLU_LIMIT, SWIGLU_LIMIT)
            hid = (hu + 1.0) * (hg * jax.nn.sigmoid(SWIGLU_ALPHA * hg))
            part = _dot(hid.astype(BF16), wd_bf[...])

            @pl.when(f == 0)
            def _():
                acc[pl.ds(row0, r), :] = part

            @pl.when((f > 0) & (f < nf - 1))
            def _():
                acc[pl.ds(row0, r), :] += part

            @pl.when(f == nf - 1)
            def _():
                acc[pl.ds(row0, r), :] += part + bd_ref[...]
                out_copy(s, sub).start()

            return carry

        sub_block(0, 0)
        lax.fori_loop(1, nsub, sub_block, 0)

    @pl.when((s == ns - 1) & (f == nf - 1))
    def _():
        await_outputs(s, nsub)
        acc[0:r, :] = jnp.zeros((r, acc.shape[1]), F32)

        def tail_copy(blk):
            return pltpu.make_async_copy(acc.at[pl.ds(0, r), :],
                                         ys_hbm.at[pl.ds(pl.multiple_of(blk * r, r), r), :], out_sem)

        def go(blk, carry):
            tail_copy(blk).start()
            return carry
        lax.fori_loop(meta_ref[1], nblk, go, 0)

        def done(blk, carry):
            tail_copy(blk).wait()
            return carry
        lax.fori_loop(meta_ref[1], nblk, done, 0)


def _moe_ffn(xp, n_tok, row_tok, w_gu, b_gu, w_down, b_down, sbe, sbrow, sbnsub, meta, layer, nblk):
    half = xp.shape[0] * xp.shape[1] // n_tok
    assert half == SUBLANES * LANES, "a packed token must be exactly one (8, 128) tile"
    rows = row_tok.shape[0]
    d = 2 * half
    dff = w_gu.shape[-1] // 2
    tf = MOE_FF_TILE
    nf = dff // tf
    ns = sbe.shape[0]
    bgu = b_gu.reshape(b_gu.shape[0], b_gu.shape[1], 1, 2 * dff)
    bdn = b_down.reshape(b_down.shape[0], b_down.shape[1], 1, d)

    def feff(s, f, meta):
        return jnp.where(s < meta[0], f, nf - 1)

    kern = functools.partial(_moe_ffn_kernel, nblk=nblk)
    return pl.pallas_call(
        kern,
        grid_spec=pltpu.PrefetchScalarGridSpec(
            num_scalar_prefetch=5,
            grid=(ns, nf),
            in_specs=[
                pl.BlockSpec(memory_space=pl.ANY),
                pl.BlockSpec((None, None, d, tf),
                             lambda s, f, se, sr, sn, mt, tk: (layer, se[s], 0, feff(s, f, mt))),
                pl.BlockSpec((None, None, d, tf),
                             lambda s, f, se, sr, sn, mt, tk: (layer, se[s], 0, nf + feff(s, f, mt))),
                pl.BlockSpec((None, None, tf, d),
                             lambda s, f, se, sr, sn, mt, tk: (layer, se[s], feff(s, f, mt), 0)),
                pl.BlockSpec((None, None, 1, tf),
                             lambda s, f, se, sr, sn, mt, tk: (layer, se[s], 0, feff(s, f, mt))),
                pl.BlockSpec((None, None, 1, tf),
                             lambda s, f, se, sr, sn, mt, tk: (layer, se[s], 0, nf + feff(s, f, mt))),
                pl.BlockSpec((None, None, 1, d),
                             lambda s, f, se, sr, sn, mt, tk: (layer, se[s], 0, 0)),
            ],
            out_specs=pl.BlockSpec(memory_space=pl.ANY),
            scratch_shapes=[pltpu.VMEM((2, MOE_SUPER_ROWS * SUBLANES, LANES), U32),
                            pltpu.VMEM((MOE_SUPER_ROWS, d), F32),
                            pltpu.VMEM((d, 2 * tf), BF16),
                            pltpu.VMEM((tf, d), BF16),
                            pltpu.SemaphoreType.DMA((2,)), pltpu.SemaphoreType.DMA(())]),
        out_shape=jax.ShapeDtypeStruct((rows, d), F32),
        compiler_params=_cparams(("arbitrary", "arbitrary")),
        name="moe_ffn",
    )(sbe, sbrow, sbnsub, meta, row_tok, xp, w_gu, w_gu, w_down, bgu, bgu, bdn)


def _combine_kernel(pos_ref, ys_hbm, x_ref, gate_ref, g_ref, b_ref, o_ref, buf, sem):
    i = pl.program_id(0)
    tb = x_ref.shape[0]
    slot = i % 2

    def row_copy(src_row, sl, k, dst_row):
        return pltpu.make_async_copy(ys_hbm.at[pl.ds(src_row, 1), :],
                                     buf.at[sl, k, pl.ds(dst_row, 1), :], sem.at[sl])

    def request(blk, sl):
        base = blk * (tb * TOP_K)

        def go(r, carry):
            for k in range(TOP_K):
                row_copy(pos_ref[base + r * TOP_K + k], sl, k, r).start()
            return carry
        lax.fori_loop(0, tb, go, 0, unroll=4)

    @pl.when(i == 0)
    def _():
        request(0, 0)

    @pl.when(i + 1 < pl.num_programs(0))
    def _():
        request(i + 1, 1 - slot)

    def land(r, carry):
        for k in range(TOP_K):
            row_copy(0, slot, k, r).wait()
        return carry

    lax.fori_loop(0, tb, land, 0, unroll=4)

    gate = gate_ref[...]
    f = gate[:, 0:1] * buf[slot, 0]
    for k in range(1, TOP_K):
        f = f + gate[:, k:k + 1] * buf[slot, k]
    o_ref[...] = _layer_norm_rows(DEEPNORM_ALPHA * x_ref[...] + f, g_ref[...], b_ref[...])


def _combine(ys, pos, gate, x, g, b, tb):
    n, d = x.shape
    return pl.pallas_call(
        _combine_kernel,
        grid_spec=pltpu.PrefetchScalarGridSpec(
            num_scalar_prefetch=1,
            grid=(n // tb,),
            in_specs=[pl.BlockSpec(memory_space=pl.ANY),
                      pl.BlockSpec((tb, d), lambda i, ps: (i, 0)),
                      pl.BlockSpec((tb, TOP_K), lambda i, ps: (i, 0)),
                      pl.BlockSpec((1, d), lambda i, ps: (0, 0)),
                      pl.BlockSpec((1, d), lambda i, ps: (0, 0))],
            out_specs=pl.BlockSpec((tb, d), lambda i, ps: (i, 0)),
            scratch_shapes=[pltpu.VMEM((2, TOP_K, tb, d), F32), pltpu.SemaphoreType.DMA((2,))]),
        out_shape=jax.ShapeDtypeStruct((n, d), F32),
        compiler_params=_cparams(("arbitrary",)),
        name="moe_combine",
    )(pos.reshape(-1), ys, x, gate, g, b)


def _dispatch_plan(top_e_t, rank_t, cnt):
    n = top_e_t.shape[1]
    r = MOE_ROW_BLOCK
    sb = MOE_SUPER_ROWS
    nblk = -(-(n * TOP_K) // r) + N_EXPERTS
    counts = cnt[:, 0].astype(I32)
    padded = (counts + r - 1) // r * r
    pend = jnp.cumsum(padded)
    pstart = pend - padded
    eids = jnp.arange(N_EXPERTS, dtype=I32)
    base = jnp.sum(jnp.where(top_e_t[:, :, None] == eids[None, None, :],
                             pstart[None, None, :], 0), axis=-1)
    dest_t = base + rank_t
    nused = pend[-1] // r
    sbb = sb // r
    nb_e = padded // r
    nsb = (nb_e + sbb - 1) // sbb
    per = (nb_e + jnp.maximum(nsb, 1) - 1) // jnp.maximum(nsb, 1)
    sb_end = jnp.cumsum(nsb)
    sb_start = sb_end - nsb
    total = sb_end[-1]
    n_sb_max = nblk // sbb + N_EXPERTS
    sid = jnp.arange(n_sb_max, dtype=I32)
    used = sid < total
    e_of = jnp.minimum(jnp.sum(sb_end[None, :] <= jnp.minimum(sid, total - 1)[:, None], axis=1),
                       N_EXPERTS - 1).astype(I32)
    idx = jnp.minimum(sid, total - 1) - sb_start[e_of]
    row0 = jnp.where(used, pstart[e_of] + idx * per[e_of] * r, 0).astype(I32)
    nsub = jnp.where(used, jnp.clip(nb_e[e_of] - idx * per[e_of], 0, per[e_of]), 0).astype(I32)
    meta = jnp.stack([total, nused]).astype(I32)
    return dest_t.astype(I32), meta, e_of, row0, nsub, nblk


def _moe_and_norm(x1, x1p, top_e_t, gate_t, rank_t, cnt, w_gu, b_gu, w_down, b_down, g, b, layer):
    dest_t, meta, sbe, sbrow, sbnsub, nblk = _dispatch_plan(top_e_t, rank_t, cnt)
    row_tok = _invert(dest_t.reshape(-1), nblk * MOE_ROW_BLOCK, dest_t.shape[1])
    ys = _moe_ffn(x1p, x1.shape[0], row_tok, w_gu, b_gu, w_down, b_down, sbe, sbrow, sbnsub, meta, layer, nblk)
    return _combine(ys, dest_t.T, gate_t.T, x1, g, b, tb=128)


def _rotary_tables(s):
    half = ROPE_DIM // 2
    inv_freq = ROPE_THETA ** (-jnp.arange(half, dtype=F32) * 2.0 / ROPE_DIM)
    ang = jnp.arange(s, dtype=F32)[:, None] * inv_freq[None, :]
    cos, sin = jnp.cos(ang), jnp.sin(ang)
    rest = HEAD_DIM - ROPE_DIM
    cos_t = jnp.concatenate([cos, cos, jnp.ones((s, rest), F32)], axis=1)
    s1_t = jnp.concatenate([-sin, jnp.zeros((s, half + rest), F32)], axis=1)
    s2_t = jnp.concatenate([jnp.zeros((s, half), F32), sin, jnp.zeros((s, rest), F32)], axis=1)
    return cos_t, s1_t, s2_t


def kernel(x, mem, mlstm_w_in, mlstm_conv_w, mlstm_conv_b, mlstm_b_igate, mlstm_b_fgate, moba_w_in, w_mem_kv, w_out, ln1_g, ln1_b, w_router, b_router, w_gu, b_gu, w_down, b_down, ln2_g, ln2_b):
    bsz, s, d = x.shape
    n = bsz * s
    n_mem = mem.shape[1]
    mix_w = d - MEM_WIDTH
    nh_ml = MLSTM_HEADS
    dv = mix_w // nh_ml
    dqk = dv // 2
    qkw2 = 2 * nh_ml * dqk
    nh_mb = mix_w // HEAD_DIM
    qm_block = 3 * mix_w // MEM_WIDTH
    assert qkw2 == mix_w and 3 * mix_w % MEM_WIDTH == 0

    xf = x.reshape(n, d)
    memf = mem.reshape(bsz * n_mem, d)
    for i in range(DEPTH):
        j = i // 2
        if i % 2 == 0:
            w_in = mlstm_w_in[j]
            gate_lo = qkw2 + 2 * mix_w
            gate_hi = gate_lo + 2 * nh_ml
            w_main = jnp.concatenate([w_in[:, :gate_lo], w_in[:, gate_hi:]], axis=1).astype(BF16)
            w_gate = jnp.pad(w_in[:, gate_lo:gate_hi], ((0, 0), (0, LANES - 2 * nh_ml)))
            z = _matmul(xf, w_main, BF16, tm=1024, tn=512).reshape(bsz, s, -1)
            gates = _matmul3(xf, w_gate, tm=1024).reshape(bsz, s, LANES)
            gates_t = jnp.swapaxes(gates[:, :, :2 * SUBLANES], 1, 2)
            bias = jnp.concatenate([mlstm_b_igate[j], mlstm_b_fgate[j]])
            bias_row = jnp.pad(bias, (0, LANES - 2 * nh_ml)).reshape(1, LANES)
            bias_col = jnp.pad(bias, (0, 2 * SUBLANES - 2 * nh_ml)).reshape(2 * SUBLANES, 1)
            h_mix = _mlstm(z, gates, gates_t, mlstm_conv_w[j], mlstm_conv_b[j].reshape(1, -1),
                           bias_row, bias_col, nh=nh_ml, dqk=dqk, dv=dv, chunk=MLSTM_CHUNK)
        else:
            z = _matmul(xf, moba_w_in[j].astype(BF16), BF16, tm=1024, tn=512).reshape(bsz, s, -1)
            cos_t, s1_t, s2_t = _rotary_tables(s)
            q_r, k_r, kbar = _moba_prep(z, cos_t, s1_t, s2_t, nh_mb)
            h_mix = _moba_attn(q_r, k_r, z, 2 * mix_w, kbar.reshape(bsz, s // MOBA_BLOCK, mix_w),
                               nh_mb, hg=MOBA_HEAD_GROUP)
        kv = _matmul(memf, w_mem_kv[i].astype(BF16), BF16, tm=1024, tn=512)
        h_mem = _mem_attn(z, kv.reshape(bsz, n_mem, 2 * MEM_WIDTH), qm_block, tq=512)
        w_o = w_out[i].astype(BF16)
        x1, x1p, top_e_t, gate_t, rank_t, cnt = _outproj(
            h_mix.reshape(n, mix_w), h_mem.reshape(n, MEM_WIDTH), w_o[:mix_w], w_o[mix_w:], xf,
            ln1_g[i].reshape(1, d), ln1_b[i].reshape(1, d),
            w_router[i].T, b_router[i].reshape(N_EXPERTS, 1), tm=512)
        xf = _moe_and_norm(x1, x1p, top_e_t, gate_t, rank_t, cnt, w_gu, b_gu, w_down, b_down,
                           ln2_g[i].reshape(1, d), ln2_b[i].reshape(1, d), i)
    return xf.reshape(bsz, s, d)
```

```python
import functools

import jax
import jax.numpy as jnp
from jax import lax
from jax.experimental import pallas as pl
from jax.experimental.pallas import tpu as pltpu

F32 = jnp.float32
BF16 = jnp.bfloat16
U32 = jnp.uint32
I32 = jnp.int32

HEAD_DIM = 128
MEM_HEADS = 4
MEM_WIDTH = MEM_HEADS * HEAD_DIM
MLSTM_HEADS = 6
CONV_WIDTH = 4
MOBA_BLOCK = 256
MOBA_TOPK = 3
ROPE_THETA = 500000.0
ROPE_DIM = HEAD_DIM // 4
N_EXPERTS = 32
TOP_K = 4
SWIGLU_LIMIT = 7.0
SWIGLU_ALPHA = 1.702
LN_EPS = 1e-5
DEPTH = 2
DEEPNORM_ALPHA = (2 * DEPTH) ** 0.25

LANES = 128
SUBLANES = 8
VMEM_LIMIT_BYTES = 56 * 1024 * 1024
MLSTM_CHUNK = 128
MOE_ROW_BLOCK = 256
MOBA_HEAD_GROUP = 4
MOE_SUPER_ROWS = 1280
MOE_FF_TILE = 512
NEG_BIG = -1e30


def _cparams(sem):
    return pltpu.CompilerParams(dimension_semantics=sem, vmem_limit_bytes=VMEM_LIMIT_BYTES)


def _dot(a, b):
    return jnp.dot(a, b, preferred_element_type=F32)


def _dot_nt(a, b):
    return lax.dot_general(a, b, (((1,), (1,)), ((), ())), preferred_element_type=F32)


def _dot_tn(a, b):
    return lax.dot_general(a, b, (((0,), (0,)), ((), ())), preferred_element_type=F32)


def _split_bf16(x):
    hi = x.astype(BF16)
    lo = (x - hi.astype(F32)).astype(BF16)
    return hi, lo


def _layer_norm_rows(r, g, b):
    mu = jnp.mean(r, axis=-1, keepdims=True)
    d = r - mu
    var = jnp.mean(d * d, axis=-1, keepdims=True)
    return d * lax.rsqrt(var + LN_EPS) * g + b


def _pack_bf16_pairs(x):
    w = x.shape[1] // 2
    lo = pltpu.bitcast(x[:, :w].astype(BF16).astype(F32), U32)
    hi = pltpu.bitcast(x[:, w:].astype(BF16).astype(F32), U32)
    return (hi & jnp.uint32(0xFFFF0000)) | (lo >> 16)


def _unpack_bf16_pairs(p):
    lo = pltpu.bitcast(p << 16, F32).astype(BF16)
    hi = pltpu.bitcast(p & jnp.uint32(0xFFFF0000), F32).astype(BF16)
    return lo, hi


def _matmul_kernel(a_ref, b_ref, o_ref, a_bf_ref):
    @pl.when(pl.program_id(1) == 0)
    def _():
        a_bf_ref[...] = a_ref[...].astype(BF16)

    o_ref[...] = _dot(a_bf_ref[...], b_ref[...]).astype(o_ref.dtype)


def _matmul(a, b, out_dtype, tm, tn):
    m, k = a.shape
    n = b.shape[1]
    return pl.pallas_call(
        _matmul_kernel,
        grid=(m // tm, n // tn),
        in_specs=[pl.BlockSpec((tm, k), lambda i, j: (i, 0)),
                  pl.BlockSpec((k, tn), lambda i, j: (0, j))],
        out_specs=pl.BlockSpec((tm, tn), lambda i, j: (i, j)),
        out_shape=jax.ShapeDtypeStruct((m, n), out_dtype),
        scratch_shapes=[pltpu.VMEM((tm, k), BF16)],
        compiler_params=_cparams(("parallel", "arbitrary")),
        name="matmul",
    )(a, b)


def _matmul3_kernel(a_ref, b_ref, o_ref):
    a_hi, a_lo = _split_bf16(a_ref[...])
    b_hi, b_lo = _split_bf16(b_ref[...])
    o_ref[...] = _dot(a_hi, b_hi) + _dot(a_hi, b_lo) + _dot(a_lo, b_hi)


def _matmul3(a, b, tm):
    m, k = a.shape
    n = b.shape[1]
    return pl.pallas_call(
        _matmul3_kernel,
        grid=(m // tm,),
        in_specs=[pl.BlockSpec((tm, k), lambda i: (i, 0)),
                  pl.BlockSpec((k, n), lambda i: (0, 0))],
        out_specs=pl.BlockSpec((tm, n), lambda i: (i, 0)),
        out_shape=jax.ShapeDtypeStruct((m, n), F32),
        compiler_params=_cparams(("parallel",)),
        name="matmul3",
    )(a, b)


def _log_sigmoid(x):
    return jnp.minimum(x, 0.0) - jnp.log1p(jnp.exp(-jnp.abs(x)))


def _mlstm_kernel(qk_ref, v_ref, og_ref, g_ref, gt_ref, cw_ref, cb_ref, bias_ref, biast_ref,
                  out_ref, ext_ref, c_ref, n_ref, m_ref, *, chunk, nh, dqk, dv):
    L = chunk
    qkw = nh * dqk

    @pl.when(pl.program_id(1) == 0)
    def _():
        ext_ref[0:SUBLANES, :] = jnp.zeros((SUBLANES, 2 * qkw), F32)
        c_ref[...] = jnp.zeros_like(c_ref)
        n_ref[...] = jnp.zeros_like(n_ref)
        m_ref[...] = jnp.zeros_like(m_ref)

    ext_ref[SUBLANES:SUBLANES + L, :] = qk_ref[...].astype(F32)
    cw = cw_ref[...]
    y = jnp.broadcast_to(cb_ref[...], (L, 2 * qkw))
    for w in range(CONV_WIDTH):
        y = y + ext_ref[pl.ds(SUBLANES - (CONV_WIDTH - 1) + w, L), :] * cw[w:w + 1, :]
    ext_ref[0:SUBLANES, :] = ext_ref[L:L + SUBLANES, :]
    qk = y * jax.nn.sigmoid(y)

    gates = g_ref[...] + bias_ref[...]
    gates_t = gt_ref[...] + biast_ref[...]
    row = lax.broadcasted_iota(jnp.int32, (L, L), 0)
    col = lax.broadcasted_iota(jnp.int32, (L, L), 1)
    causal = col <= row

    for h in range(nh):
        q = qk[:, h * dqk:(h + 1) * dqk]
        k = qk[:, qkw + h * dqk:qkw + (h + 1) * dqk] * (dqk ** -0.5)
        v_bf = v_ref[:, h * dv:(h + 1) * dv].astype(BF16)
        q_bf = q.astype(BF16)
        ig_c = gates[:, h:h + 1]
        ig_r = gates_t[h:h + 1, :]
        lf_c = _log_sigmoid(gates[:, nh + h:nh + h + 1])
        lf_r = _log_sigmoid(gates_t[nh + h:nh + h + 1, :])
        bcum_c = jnp.sum(jnp.where(causal, lf_r, 0.0), axis=1, keepdims=True)
        bcum_r = jnp.sum(jnp.where(row <= col, lf_c, 0.0), axis=0, keepdims=True)
        g_tot = jnp.sum(lf_r, axis=1, keepdims=True)
        m_prev = m_ref[h:h + 1, 0:1]

        dmat = jnp.where(causal, bcum_c - bcum_r + ig_r, -jnp.inf)
        inter_log = bcum_c + m_prev
        m_t = jnp.maximum(inter_log, jnp.max(dmat, axis=1, keepdims=True))
        s_qk = _dot_nt(q_bf, k.astype(BF16)) * jnp.exp(dmat - m_t)
        inter_w = jnp.exp(inter_log - m_t)
        c_prev = c_ref[h]
        n_prev = n_ref[h:h + 1, :]
        num = inter_w * _dot(q_bf, c_prev.astype(BF16)) + _dot(s_qk.astype(BF16), v_bf)
        den = inter_w * jnp.sum(q * n_prev, axis=1, keepdims=True) \
            + jnp.sum(s_qk, axis=1, keepdims=True)
        hval = num / jnp.maximum(jnp.abs(den), jnp.exp(-m_t))
        o_gate = jax.nn.sigmoid(og_ref[:, h * dv:(h + 1) * dv].astype(F32))
        out_ref[:, h * dv:(h + 1) * dv] = (o_gate * hval).astype(out_ref.dtype)

        a_c = g_tot - bcum_c + ig_c
        m_new = jnp.maximum(g_tot + m_prev, jnp.max(a_c, axis=0, keepdims=True))
        decay = jnp.exp(g_tot + m_prev - m_new)
        kw = k * jnp.exp(a_c - m_new)
        c_ref[h] = decay * c_prev + _dot_tn(kw.astype(BF16), v_bf)
        n_ref[h:h + 1, :] = decay * n_prev + jnp.sum(kw, axis=0, keepdims=True)
        m_ref[h:h + 1, :] = jnp.broadcast_to(m_new, (1, LANES))


def _mlstm(z, gates, gates_t, conv_w, conv_b, bias_row, bias_col, *, nh, dqk, dv, chunk):
    bsz, s, _ = z.shape
    qkw2 = 2 * nh * dqk
    vw = nh * dv
    assert qkw2 == vw, "q|k, v and output-gate column blocks must share one block width"
    nc = s // chunk
    kern = functools.partial(_mlstm_kernel, chunk=chunk, nh=nh, dqk=dqk, dv=dv)
    return pl.pallas_call(
        kern,
        grid=(bsz, nc),
        in_specs=[
            pl.BlockSpec((None, chunk, vw), lambda b, c: (b, c, 0)),
            pl.BlockSpec((None, chunk, vw), lambda b, c: (b, c, 1)),
            pl.BlockSpec((None, chunk, vw), lambda b, c: (b, c, 2)),
            pl.BlockSpec((None, chunk, LANES), lambda b, c: (b, c, 0)),
            pl.BlockSpec((None, 2 * SUBLANES, chunk), lambda b, c: (b, 0, c)),
            pl.BlockSpec((CONV_WIDTH, qkw2), lambda b, c: (0, 0)),
            pl.BlockSpec((1, qkw2), lambda b, c: (0, 0)),
            pl.BlockSpec((1, LANES), lambda b, c: (0, 0)),
            pl.BlockSpec((2 * SUBLANES, 1), lambda b, c: (0, 0)),
        ],
        out_specs=pl.BlockSpec((None, chunk, vw), lambda b, c: (b, c, 0)),
        out_shape=jax.ShapeDtypeStruct((bsz, s, vw), BF16),
        scratch_shapes=[
            pltpu.VMEM((chunk + SUBLANES, qkw2), F32),
            pltpu.VMEM((nh, dqk, dv), F32),
            pltpu.VMEM((SUBLANES, dqk), F32),
            pltpu.VMEM((SUBLANES, LANES), F32),
        ],
        compiler_params=_cparams(("parallel", "arbitrary")),
        name="mlstm",
    )(z, z, z, gates, gates_t, conv_w, conv_b, bias_row, bias_col)


def _mem_attn_kernel(q_ref, k_ref, v_ref, o_ref):
    scale = HEAD_DIM ** -0.5
    for h in range(MEM_HEADS):
        sl = slice(h * HEAD_DIM, (h + 1) * HEAD_DIM)
        q = q_ref[:, sl].astype(BF16)
        sc = _dot_nt(q, k_ref[:, sl]) * scale
        sc = sc - jnp.max(sc, axis=1, keepdims=True)
        p = jnp.exp(sc)
        p = p / jnp.sum(p, axis=1, keepdims=True)
        o_ref[:, sl] = _dot(p.astype(BF16), v_ref[:, sl]).astype(o_ref.dtype)


def _mem_attn(z, kv, qm_block, tq):
    bsz, s, _ = z.shape
    n_mem = kv.shape[1]
    return pl.pallas_call(
        _mem_attn_kernel,
        grid=(bsz, s // tq),
        in_specs=[
            pl.BlockSpec((None, tq, MEM_WIDTH), lambda b, i: (b, i, qm_block)),
            pl.BlockSpec((None, n_mem, MEM_WIDTH), lambda b, i: (b, 0, 0)),
            pl.BlockSpec((None, n_mem, MEM_WIDTH), lambda b, i: (b, 0, 1)),
        ],
        out_specs=pl.BlockSpec((None, tq, MEM_WIDTH), lambda b, i: (b, i, 0)),
        out_shape=jax.ShapeDtypeStruct((bsz, s, MEM_WIDTH), BF16),
        compiler_params=_cparams(("parallel", "parallel")),
        name="mem_attn",
    )(z, kv, kv)


def _moba_prep_kernel(q_ref, k_ref, cos_ref, s1_ref, s2_ref, qo_ref, ko_ref, kbar_ref, *, nh):
    cosf = cos_ref[...]
    s1 = s1_ref[...]
    s2 = s2_ref[...]
    half = ROPE_DIM // 2
    for h in range(nh):
        sl = slice(h * HEAD_DIM, (h + 1) * HEAD_DIM)
        for src, dst, is_k in ((q_ref, qo_ref, False), (k_ref, ko_ref, True)):
            x = src[:, sl].astype(F32)
            xr = (x * cosf + pltpu.roll(x, HEAD_DIM - half, 1) * s1
                  + pltpu.roll(x, half, 1) * s2)
            if is_k:
                dst[:, sl] = xr.astype(dst.dtype)
                kbar_ref[:, sl] = jnp.mean(xr, axis=0, keepdims=True)
            else:
                dst[:, sl] = (xr * (HEAD_DIM ** -0.5)).astype(dst.dtype)


def _moba_prep(z, cos_t, s1_t, s2_t, nh):
    bsz, s, _ = z.shape
    w = nh * HEAD_DIM
    nb = s // MOBA_BLOCK
    t = MOBA_BLOCK
    kern = functools.partial(_moba_prep_kernel, nh=nh)
    tab = pl.BlockSpec((t, HEAD_DIM), lambda b, i: (i, 0))
    big = jax.ShapeDtypeStruct((bsz, s, w), BF16)
    return pl.pallas_call(
        kern,
        grid=(bsz, nb),
        in_specs=[pl.BlockSpec((None, t, w), lambda b, i: (b, i, 0)),
                  pl.BlockSpec((None, t, w), lambda b, i: (b, i, 1)),
                  tab, tab, tab],
        out_specs=[pl.BlockSpec((None, t, w), lambda b, i: (b, i, 0)),
                   pl.BlockSpec((None, t, w), lambda b, i: (b, i, 0)),
                   pl.BlockSpec((None, None, 1, w), lambda b, i: (b, i, 0, 0))],
        out_shape=[big, big, jax.ShapeDtypeStruct((bsz, nb, 1, w), F32)],
        compiler_params=_cparams(("parallel", "parallel")),
        name="moba_prep",
    )(z, z, cos_t, s1_t, s2_t)


def _moba_attn_kernel(q_ref, k_ref, v_ref, kbar_ref, o_ref, *, nb, hg):
    t = MOBA_BLOCK
    qb = pl.program_id(2)
    blk = lax.broadcasted_iota(jnp.int32, (nb, t), 0)
    valid = blk < qb
    pad_rows = 2 * SUBLANES - nb
    eye = jnp.where(lax.broadcasted_iota(jnp.int32, (nb + pad_rows, LANES), 0)
                    == lax.broadcasted_iota(jnp.int32, (nb + pad_rows, LANES), 1),
                    1.0, 0.0).astype(BF16)

    qs = []
    sel_biases = []
    for hh in range(hg):
        sl = slice(hh * HEAD_DIM, (hh + 1) * HEAD_DIM)
        q = q_ref[:, sl]
        kb_hi, kb_lo = _split_bf16(kbar_ref[:, sl])
        gate = _dot_nt(kb_hi, q) + _dot_nt(kb_lo, q)
        gm = jnp.where(valid, gate, -jnp.inf)
        rank = jnp.zeros((nb, t), jnp.int32)
        for n2 in range(nb):
            gc = gm[n2:n2 + 1, :]
            beats = (gc > gm) | ((gc == gm) & (n2 < blk))
            rank = rank + beats.astype(jnp.int32)
        bias_t = jnp.where(valid & (rank < MOBA_TOPK), 0.0, NEG_BIG)
        bias_t = jnp.concatenate([bias_t, jnp.zeros((pad_rows, t), F32)], axis=0).astype(BF16)
        sel_biases.append(_dot_tn(bias_t, eye))
        qs.append(q)

    row = lax.broadcasted_iota(jnp.int32, (t, t), 0)
    col = lax.broadcasted_iota(jnp.int32, (t, t), 1)
    causal = col <= row

    for c in range(nb):
        @pl.when(qb == c)
        def _(c=c):
            for hh in range(hg):
                sl = slice(hh * HEAD_DIM, (hh + 1) * HEAD_DIM)
                pieces = []
                for n in range(c + 1):
                    s = _dot_nt(qs[hh], k_ref[n * t:(n + 1) * t, sl])
                    if n == c:
                        pieces.append(jnp.where(causal, s, NEG_BIG))
                    else:
                        pieces.append(s + sel_biases[hh][:, n:n + 1])
                mx = pieces[0]
                for piece in pieces[1:]:
                    mx = jnp.maximum(mx, piece)
                m = jnp.max(mx, axis=1, keepdims=True)
                psum = None
                acc = None
                for n in range(c + 1):
                    p = jnp.exp(pieces[n] - m)
                    pv = _dot(p.astype(BF16), v_ref[n * t:(n + 1) * t, sl])
                    psum = p if psum is None else psum + p
                    acc = pv if acc is None else acc + pv
                l = jnp.sum(psum, axis=1, keepdims=True)
                o_ref[:, sl] = (acc / l).astype(o_ref.dtype)


def _moba_attn(q, k, z, v_col, kbar, nh, hg):
    bsz, s, _ = q.shape
    nb = s // MOBA_BLOCK
    t = MOBA_BLOCK
    w = hg * HEAD_DIM
    assert v_col % w == 0
    v_blk = v_col // w
    kern = functools.partial(_moba_attn_kernel, nb=nb, hg=hg)
    return pl.pallas_call(
        kern,
        grid=(bsz, nh // hg, nb),
        in_specs=[pl.BlockSpec((None, t, w), lambda b, h, i: (b, i, h)),
                  pl.BlockSpec((None, s, w), lambda b, h, i: (b, 0, h)),
                  pl.BlockSpec((None, s, w), lambda b, h, i: (b, 0, v_blk + h)),
                  pl.BlockSpec((None, nb, w), lambda b, h, i: (b, 0, h))],
        out_specs=pl.BlockSpec((None, t, w), lambda b, h, i: (b, i, h)),
        out_shape=jax.ShapeDtypeStruct((bsz, s, nh * HEAD_DIM), BF16),
        compiler_params=_cparams(("parallel", "parallel", "arbitrary")),
        name="moba_attn",
    )(q, k, z, kbar)


def _outproj_kernel(hmix_ref, hmem_ref, w1_ref, w2_ref, x_ref, g_ref, b_ref, wr_ref, br_ref,
                    x1_ref, x1p_ref, tope_ref, gate_ref, rank_ref, cnt_ref, tri_ref, run_ref):
    i = pl.program_id(0)
    y = _dot(hmix_ref[...], w1_ref[...]) + _dot(hmem_ref[...], w2_ref[...])
    x1 = _layer_norm_rows(DEEPNORM_ALPHA * x_ref[...] + y, g_ref[...], b_ref[...])
    x1_ref[...] = x1
    packed = _pack_bf16_pairs(x1)
    for jj in range(packed.shape[1] // LANES):
        x1p_ref[pl.ds(jj, packed.shape[0], stride=SUBLANES), :] = packed[:, jj * LANES:(jj + 1) * LANES]

    x_hi, x_lo = _split_bf16(x1)
    w_hi, w_lo = _split_bf16(wr_ref[...])
    logits = _dot_nt(w_hi, x_hi) + _dot_nt(w_hi, x_lo) + _dot_nt(w_lo, x_hi) + br_ref[...]
    ne, tm = logits.shape

    @pl.when(i == 0)
    def _():
        r = lax.broadcasted_iota(jnp.int32, (tm, tm), 0)
        c = lax.broadcasted_iota(jnp.int32, (tm, tm), 1)
        tri_ref[...] = jnp.where(r <= c, 1.0, 0.0).astype(BF16)
        run_ref[...] = jnp.zeros_like(run_ref)

    eid = lax.broadcasted_iota(jnp.int32, (ne, tm), 0)
    run = run_ref[...]
    vals = []
    for k in range(TOP_K):
        mx = jnp.max(logits, axis=0, keepdims=True)
        idx = jnp.min(jnp.where(logits == mx, eid, ne), axis=0, keepdims=True)
        tope_ref[k:k + 1, :] = idx
        vals.append(mx)
        hit = eid == idx
        logits = jnp.where(hit, -jnp.inf, logits)
        incl = _dot(jnp.where(hit, 1.0, 0.0).astype(BF16), tri_ref[...])
        rank = jnp.sum(jnp.where(hit, run + incl - 1.0, 0.0), axis=0, keepdims=True)
        rank_ref[k:k + 1, :] = rank.astype(jnp.int32)
        run = run + incl[:, tm - 1:tm]
    run_ref[...] = run
    cnt_ref[...] = jnp.broadcast_to(run, cnt_ref.shape)
    ex = [jnp.exp(vk - vals[0]) for vk in vals]
    tot = ex[0] + ex[1] + ex[2] + ex[3]
    for k in range(TOP_K):
        gate_ref[k:k + 1, :] = ex[k] / tot


def _outproj(hmix, hmem, w1, w2, x, g, b, wr_t, br_col, tm):
    m, d = x.shape
    ne = wr_t.shape[0]
    return pl.pallas_call(
        _outproj_kernel,
        grid=(m // tm,),
        in_specs=[pl.BlockSpec((tm, hmix.shape[1]), lambda i: (i, 0)),
                  pl.BlockSpec((tm, hmem.shape[1]), lambda i: (i, 0)),
                  pl.BlockSpec(w1.shape, lambda i: (0, 0)),
                  pl.BlockSpec(w2.shape, lambda i: (0, 0)),
                  pl.BlockSpec((tm, d), lambda i: (i, 0)),
                  pl.BlockSpec((1, d), lambda i: (0, 0)),
                  pl.BlockSpec((1, d), lambda i: (0, 0)),
                  pl.BlockSpec(wr_t.shape, lambda i: (0, 0)),
                  pl.BlockSpec(br_col.shape, lambda i: (0, 0))],
        out_specs=[pl.BlockSpec((tm, d), lambda i: (i, 0)),
                   pl.BlockSpec((tm * SUBLANES, LANES), lambda i: (i, 0)),
                   pl.BlockSpec((TOP_K, tm), lambda i: (0, i)),
                   pl.BlockSpec((TOP_K, tm), lambda i: (0, i)),
                   pl.BlockSpec((TOP_K, tm), lambda i: (0, i)),
                   pl.BlockSpec((ne, LANES), lambda i: (0, 0))],
        out_shape=[jax.ShapeDtypeStruct((m, d), F32),
                   jax.ShapeDtypeStruct((m * SUBLANES, LANES), U32),
                   jax.ShapeDtypeStruct((TOP_K, m), jnp.int32),
                   jax.ShapeDtypeStruct((TOP_K, m), F32),
                   jax.ShapeDtypeStruct((TOP_K, m), jnp.int32),
                   jax.ShapeDtypeStruct((ne, LANES), F32)],
        scratch_shapes=[pltpu.VMEM((tm, tm), BF16), pltpu.VMEM((ne, 1), F32)],
        compiler_params=_cparams(("arbitrary",)),
        name="outproj_ln_router",
    )(hmix, hmem, w1, w2, x, g, b, wr_t, br_col)


INVERT_UNROLL = 16


def _invert_kernel(trips_ref, dest_ref, tok_ref, *, n):
    def clear(i, carry):
        for u in range(INVERT_UNROLL):
            tok_ref[i * INVERT_UNROLL + u] = 0
        return carry

    lax.fori_loop(0, trips_ref[0], clear, 0)
    for k in range(TOP_K):
        def put(i, carry, k=k):
            for u in range(INVERT_UNROLL):
                t = i * INVERT_UNROLL + u
                tok_ref[dest_ref[k * n + t]] = t
            return carry

        lax.fori_loop(0, trips_ref[1], put, 0)


def _invert(dest_flat, rows, n):
    assert rows % INVERT_UNROLL == 0 and n % INVERT_UNROLL == 0
    trips = jnp.array([rows // INVERT_UNROLL, n // INVERT_UNROLL], I32)
    return pl.pallas_call(
        functools.partial(_invert_kernel, n=n),
        in_specs=[pl.BlockSpec(memory_space=pltpu.SMEM), pl.BlockSpec(memory_space=pltpu.SMEM)],
        out_specs=pl.BlockSpec(memory_space=pltpu.SMEM),
        out_shape=jax.ShapeDtypeStruct((rows,), I32),
        name="moe_invert",
    )(trips, dest_flat)


ROW_DMA_UNROLL = 8


def _moe_ffn_kernel(sbe_ref, sbrow_ref, sbnsub_ref, meta_ref, tok_ref,
                    x_hbm, wg_ref, wu_ref, wd_ref, bg_ref, bu_ref, bd_ref, ys_hbm,
                    xraw, acc, wgu_bf, wd_bf, hid_ref, in_sem, out_sem, *, nblk):
    s = pl.program_id(0)
    f = pl.program_id(1)
    ns = pl.num_programs(0)
    nf = pl.num_programs(1)
    r = MOE_ROW_BLOCK
    nxl = wgu_bf.shape[0] // (2 * LANES)
    nsub = sbnsub_ref[s]
    nsub_prev = jnp.where(s > 0, sbnsub_ref[jnp.maximum(s - 1, 0)], 0)
    slot = s % 2

    def row_copy(src_row, sl, dst_row):
        return pltpu.make_async_copy(
            x_hbm.at[pl.ds(pl.multiple_of(src_row * nxl, nxl), nxl), :],
            xraw.at[sl, pl.ds(pl.multiple_of(dst_row * nxl, nxl), nxl), :], in_sem.at[sl])

    def out_copy(ss, sub):
        row0 = sub * r if isinstance(sub, int) else pl.multiple_of(sub * r, r)
        dst = pl.multiple_of(sbrow_ref[ss] + sub * r, r)
        return pltpu.make_async_copy(acc.at[pl.ds(row0, r), :], ys_hbm.at[pl.ds(dst, r), :], out_sem)

    def await_rows(sl, n_rows):
        def land(i, carry):
            for u in range(ROW_DMA_UNROLL):
                row_copy(0, sl, i * ROW_DMA_UNROLL + u).wait()
            return carry
        lax.fori_loop(0, n_rows // ROW_DMA_UNROLL, land, 0)

    def await_outputs(ss, count):
        def flush(sub, carry):
            out_copy(ss, sub).wait()
            return carry
        lax.fori_loop(0, count, flush, 0)

    def request_rows(ss, sl):
        base = sbrow_ref[ss]

        def go(i, carry):
            for u in range(ROW_DMA_UNROLL):
                j = i * ROW_DMA_UNROLL + u
                row_copy(tok_ref[base + j], sl, j).start()
            return carry
        lax.fori_loop(0, sbnsub_ref[ss] * (r // ROW_DMA_UNROLL), go, 0)

    @pl.when(f == 0)
    def _():
        @pl.when(s == 0)
        def _():
            request_rows(0, 0)

        @pl.when(s + 1 < ns)
        def _():
            request_rows(s + 1, 1 - slot)

        await_rows(slot, nsub * r)
        await_outputs(jnp.maximum(s - 1, 0), nsub_prev)

    @pl.when(nsub > 0)
    def _():
        @pl.when(f == 0)
        def _():
            def init(sub, carry):
                acc[pl.ds(pl.multiple_of(sub * r, r), r), :] = jnp.broadcast_to(
                    bd_ref[...], (r, acc.shape[1]))
                return carry
            lax.fori_loop(0, nsub, init, 0)

        tf = wd_bf.shape[0]
        wgu_bf[:, 0:tf] = wg_ref[...].astype(BF16)
        wgu_bf[:, tf:] = wu_ref[...].astype(BF16)
        wd_bf[...] = wd_ref[...].astype(BF16)

        def hidden(sub):
            row0 = sub * r if isinstance(sub, int) else pl.multiple_of(sub * r, r)
            xu = jnp.concatenate([xraw[slot, pl.ds(row0 * nxl + jj, r, stride=nxl), :]
                                  for jj in range(nxl)], axis=1)
            x = jnp.concatenate(_unpack_bf16_pairs(xu), axis=1)
            h = _dot(x, wgu_bf[...])
            hg = h[:, 0:tf] + bg_ref[...]
            hu = h[:, tf:] + bu_ref[...]
            hg = jnp.minimum(hg, SWIGLU_LIMIT)
            hu = jnp.clip(hu, -SWIGLU_LIMIT, SWIGLU_LIMIT)
            hid = (hu + 1.0) * (hg * jax.nn.sigmoid(SWIGLU_ALPHA * hg))
            hid_ref[sub % 2] = hid.astype(BF16)

        def accumulate(sub, part):
            acc[pl.ds(pl.multiple_of(sub * r, r), r), :] += part

            @pl.when(f == nf - 1)
            def _():
                out_copy(s, sub).start()

        hidden(0)

        def step(i, carry):
            part = _dot(hid_ref[i % 2], wd_bf[...])
            hidden(i + 1)
            accumulate(i, part)
            return carry

        lax.fori_loop(0, nsub - 1, step, 0)
        last = nsub - 1
        accumulate(last, _dot(hid_ref[last % 2], wd_bf[...]))

    @pl.when((s == ns - 1) & (f == nf - 1))
    def _():
        await_outputs(s, nsub)
        acc[0:r, :] = jnp.zeros((r, acc.shape[1]), F32)

        def tail_copy(blk):
            return pltpu.make_async_copy(acc.at[pl.ds(0, r), :],
                                         ys_hbm.at[pl.ds(pl.multiple_of(blk * r, r), r), :], out_sem)

        def go(blk, carry):
            tail_copy(blk).start()
            return carry
        lax.fori_loop(meta_ref[1], nblk, go, 0)

        def done(blk, carry):
            tail_copy(blk).wait()
            return carry
        lax.fori_loop(meta_ref[1], nblk, done, 0)


def _moe_ffn(xp, n_tok, row_tok, w_gu, b_gu, w_down, b_down, sbe, sbrow, sbnsub, meta, layer, nblk):
    half = xp.shape[0] * xp.shape[1] // n_tok
    assert half == SUBLANES * LANES, "a packed token must be exactly one (8, 128) tile"
    rows = row_tok.shape[0]
    d = 2 * half
    dff = w_gu.shape[-1] // 2
    tf = MOE_FF_TILE
    nf = dff // tf
    ns = sbe.shape[0]
    bgu = b_gu.reshape(b_gu.shape[0], b_gu.shape[1], 1, 2 * dff)
    bdn = b_down.reshape(b_down.shape[0], b_down.shape[1], 1, d)

    def feff(s, f, meta):
        return jnp.where(s < meta[0], f, nf - 1)

    kern = functools.partial(_moe_ffn_kernel, nblk=nblk)
    return pl.pallas_call(
        kern,
        grid_spec=pltpu.PrefetchScalarGridSpec(
            num_scalar_prefetch=5,
            grid=(ns, nf),
            in_specs=[
                pl.BlockSpec(memory_space=pl.ANY),
                pl.BlockSpec((None, None, d, tf),
                             lambda s, f, se, sr, sn, mt, tk: (layer, se[s], 0, feff(s, f, mt))),
                pl.BlockSpec((None, None, d, tf),
                             lambda s, f, se, sr, sn, mt, tk: (layer, se[s], 0, nf + feff(s, f, mt))),
                pl.BlockSpec((None, None, tf, d),
                             lambda s, f, se, sr, sn, mt, tk: (layer, se[s], feff(s, f, mt), 0)),
                pl.BlockSpec((None, None, 1, tf),
                             lambda s, f, se, sr, sn, mt, tk: (layer, se[s], 0, feff(s, f, mt))),
                pl.BlockSpec((None, None, 1, tf),
                             lambda s, f, se, sr, sn, mt, tk: (layer, se[s], 0, nf + feff(s, f, mt))),
                pl.BlockSpec((None, None, 1, d),
                             lambda s, f, se, sr, sn, mt, tk: (layer, se[s], 0, 0)),
            ],
            out_specs=pl.BlockSpec(memory_space=pl.ANY),
            scratch_shapes=[pltpu.VMEM((2, MOE_SUPER_ROWS * SUBLANES, LANES), U32),
                            pltpu.VMEM((MOE_SUPER_ROWS, d), F32),
                            pltpu.VMEM((d, 2 * tf), BF16),
                            pltpu.VMEM((tf, d), BF16),
                            pltpu.VMEM((2, MOE_ROW_BLOCK, tf), BF16),
                            pltpu.SemaphoreType.DMA((2,)), pltpu.SemaphoreType.DMA(())]),
        out_shape=jax.ShapeDtypeStruct((rows, d), F32),
        compiler_params=_cparams(("arbitrary", "arbitrary")),
        name="moe_ffn",
    )(sbe, sbrow, sbnsub, meta, row_tok, xp, w_gu, w_gu, w_down, bgu, bgu, bdn)


def _combine_kernel(pos_ref, ys_hbm, x_ref, gate_ref, g_ref, b_ref, o_ref, buf, sem):
    i = pl.program_id(0)
    tb = x_ref.shape[0]
    slot = i % 2

    def row_copy(src_row, sl, k, dst_row):
        return pltpu.make_async_copy(ys_hbm.at[pl.ds(src_row, 1), :],
                                     buf.at[sl, k, pl.ds(dst_row, 1), :], sem.at[sl])

    def request(blk, sl):
        base = blk * (tb * TOP_K)

        def go(r, carry):
            for k in range(TOP_K):
                row_copy(pos_ref[base + r * TOP_K + k], sl, k, r).start()
            return carry
        lax.fori_loop(0, tb, go, 0, unroll=4)

    @pl.when(i == 0)
    def _():
        request(0, 0)

    @pl.when(i + 1 < pl.num_programs(0))
    def _():
        request(i + 1, 1 - slot)

    def land(r, carry):
        for k in range(TOP_K):
            row_copy(0, slot, k, r).wait()
        return carry

    lax.fori_loop(0, tb, land, 0, unroll=4)

    gate = gate_ref[...]
    f = gate[:, 0:1] * buf[slot, 0]
    for k in range(1, TOP_K):
        f = f + gate[:, k:k + 1] * buf[slot, k]
    o_ref[...] = _layer_norm_rows(DEEPNORM_ALPHA * x_ref[...] + f, g_ref[...], b_ref[...])


def _combine(ys, pos, gate, x, g, b, tb):
    n, d = x.shape
    return pl.pallas_call(
        _combine_kernel,
        grid_spec=pltpu.PrefetchScalarGridSpec(
            num_scalar_prefetch=1,
            grid=(n // tb,),
            in_specs=[pl.BlockSpec(memory_space=pl.ANY),
                      pl.BlockSpec((tb, d), lambda i, ps: (i, 0)),
                      pl.BlockSpec((tb, TOP_K), lambda i, ps: (i, 0)),
                      pl.BlockSpec((1, d), lambda i, ps: (0, 0)),
                      pl.BlockSpec((1, d), lambda i, ps: (0, 0))],
            out_specs=pl.BlockSpec((tb, d), lambda i, ps: (i, 0)),
            scratch_shapes=[pltpu.VMEM((2, TOP_K, tb, d), F32), pltpu.SemaphoreType.DMA((2,))]),
        out_shape=jax.ShapeDtypeStruct((n, d), F32),
        compiler_params=_cparams(("arbitrary",)),
        name="moe_combine",
    )(pos.reshape(-1), ys, x, gate, g, b)


def _dispatch_plan(top_e_t, rank_t, cnt):
    n = top_e_t.shape[1]
    r = MOE_ROW_BLOCK
    sb = MOE_SUPER_ROWS
    nblk = -(-(n * TOP_K) // r) + N_EXPERTS
    counts = cnt[:, 0].astype(I32)
    padded = (counts + r - 1) // r * r
    pend = jnp.cumsum(padded)
    pstart = pend - padded
    eids = jnp.arange(N_EXPERTS, dtype=I32)
    base = jnp.sum(jnp.where(top_e_t[:, :, None] == eids[None, None, :],
                             pstart[None, None, :], 0), axis=-1)
    dest_t = base + rank_t
    nused = pend[-1] // r
    sbb = sb // r
    nb_e = padded // r
    nsb = (nb_e + sbb - 1) // sbb
    per = (nb_e + jnp.maximum(nsb, 1) - 1) // jnp.maximum(nsb, 1)
    sb_end = jnp.cumsum(nsb)
    sb_start = sb_end - nsb
    total = sb_end[-1]
    n_sb_max = nblk // sbb + N_EXPERTS
    sid = jnp.arange(n_sb_max, dtype=I32)
    used = sid < total
    e_of = jnp.minimum(jnp.sum(sb_end[None, :] <= jnp.minimum(sid, total - 1)[:, None], axis=1),
                       N_EXPERTS - 1).astype(I32)
    idx = jnp.minimum(sid, total - 1) - sb_start[e_of]
    row0 = jnp.where(used, pstart[e_of] + idx * per[e_of] * r, 0).astype(I32)
    nsub = jnp.where(used, jnp.clip(nb_e[e_of] - idx * per[e_of], 0, per[e_of]), 0).astype(I32)
    meta = jnp.stack([total, nused]).astype(I32)
    return dest_t.astype(I32), meta, e_of, row0, nsub, nblk


def _moe_and_norm(x1, x1p, top_e_t, gate_t, rank_t, cnt, w_gu, b_gu, w_down, b_down, g, b, layer):
    dest_t, meta, sbe, sbrow, sbnsub, nblk = _dispatch_plan(top_e_t, rank_t, cnt)
    row_tok = _invert(dest_t.reshape(-1), nblk * MOE_ROW_BLOCK, dest_t.shape[1])
    ys = _moe_ffn(x1p, x1.shape[0], row_tok, w_gu, b_gu, w_down, b_down, sbe, sbrow, sbnsub, meta, layer, nblk)
    return _combine(ys, dest_t.T, gate_t.T, x1, g, b, tb=128)


def _rotary_tables(s):
    half = ROPE_DIM // 2
    inv_freq = ROPE_THETA ** (-jnp.arange(half, dtype=F32) * 2.0 / ROPE_DIM)
    ang = jnp.arange(s, dtype=F32)[:, None] * inv_freq[None, :]
    cos, sin = jnp.cos(ang), jnp.sin(ang)
    rest = HEAD_DIM - ROPE_DIM
    cos_t = jnp.concatenate([cos, cos, jnp.ones((s, rest), F32)], axis=1)
    s1_t = jnp.concatenate([-sin, jnp.zeros((s, half + rest), F32)], axis=1)
    s2_t = jnp.concatenate([jnp.zeros((s, half), F32), sin, jnp.zeros((s, rest), F32)], axis=1)
    return cos_t, s1_t, s2_t


def kernel(x, mem, mlstm_w_in, mlstm_conv_w, mlstm_conv_b, mlstm_b_igate, mlstm_b_fgate, moba_w_in, w_mem_kv, w_out, ln1_g, ln1_b, w_router, b_router, w_gu, b_gu, w_down, b_down, ln2_g, ln2_b):
    bsz, s, d = x.shape
    n = bsz * s
    n_mem = mem.shape[1]
    mix_w = d - MEM_WIDTH
    nh_ml = MLSTM_HEADS
    dv = mix_w // nh_ml
    dqk = dv // 2
    qkw2 = 2 * nh_ml * dqk
    nh_mb = mix_w // HEAD_DIM
    qm_block = 3 * mix_w // MEM_WIDTH
    assert qkw2 == mix_w and 3 * mix_w % MEM_WIDTH == 0

    xf = x.reshape(n, d)
    memf = mem.reshape(bsz * n_mem, d)
    for i in range(DEPTH):
        j = i // 2
        if i % 2 == 0:
            w_in = mlstm_w_in[j]
            gate_lo = qkw2 + 2 * mix_w
            gate_hi = gate_lo + 2 * nh_ml
            w_main = jnp.concatenate([w_in[:, :gate_lo], w_in[:, gate_hi:]], axis=1).astype(BF16)
            w_gate = jnp.pad(w_in[:, gate_lo:gate_hi], ((0, 0), (0, LANES - 2 * nh_ml)))
            z = _matmul(xf, w_main, BF16, tm=1024, tn=512).reshape(bsz, s, -1)
            gates = _matmul3(xf, w_gate, tm=1024).reshape(bsz, s, LANES)
            gates_t = jnp.swapaxes(gates[:, :, :2 * SUBLANES], 1, 2)
            bias = jnp.concatenate([mlstm_b_igate[j], mlstm_b_fgate[j]])
            bias_row = jnp.pad(bias, (0, LANES - 2 * nh_ml)).reshape(1, LANES)
            bias_col = jnp.pad(bias, (0, 2 * SUBLANES - 2 * nh_ml)).reshape(2 * SUBLANES, 1)
            h_mix = _mlstm(z, gates, gates_t, mlstm_conv_w[j], mlstm_conv_b[j].reshape(1, -1),
                           bias_row, bias_col, nh=nh_ml, dqk=dqk, dv=dv, chunk=MLSTM_CHUNK)
        else:
            z = _matmul(xf, moba_w_in[j].astype(BF16), BF16, tm=1024, tn=512).reshape(bsz, s, -1)
            cos_t, s1_t, s2_t = _rotary_tables(s)
            q_r, k_r, kbar = _moba_prep(z, cos_t, s1_t, s2_t, nh_mb)
            h_mix = _moba_attn(q_r, k_r, z, 2 * mix_w, kbar.reshape(bsz, s // MOBA_BLOCK, mix_w),
                               nh_mb, hg=MOBA_HEAD_GROUP)
        kv = _matmul(memf, w_mem_kv[i].astype(BF16), BF16, tm=1024, tn=512)
        h_mem = _mem_attn(z, kv.reshape(bsz, n_mem, 2 * MEM_WIDTH), qm_block, tq=512)
        w_o = w_out[i].astype(BF16)
        x1, x1p, top_e_t, gate_t, rank_t, cnt = _outproj(
            h_mix.reshape(n, mix_w), h_mem.reshape(n, MEM_WIDTH), w_o[:mix_w], w_o[mix_w:], xf,
            ln1_g[i].reshape(1, d), ln1_b[i].reshape(1, d),
            w_router[i].T, b_router[i].reshape(N_EXPERTS, 1), tm=512)
        xf = _moe_and_norm(x1, x1p, top_e_t, gate_t, rank_t, cnt, w_gu, b_gu, w_down, b_down,
                           ln2_g[i].reshape(1, d), ln2_b[i].reshape(1, d), i)
    return xf.reshape(bsz, s, d)
```

```python
import functools

import jax
import jax.numpy as jnp
from jax import lax
from jax.experimental import pallas as pl
from jax.experimental.pallas import tpu as pltpu

F32 = jnp.float32
BF16 = jnp.bfloat16
U32 = jnp.uint32
I32 = jnp.int32

HEAD_DIM = 128
MEM_HEADS = 4
MEM_WIDTH = MEM_HEADS * HEAD_DIM
MLSTM_HEADS = 6
CONV_WIDTH = 4
MOBA_BLOCK = 256
MOBA_TOPK = 3
ROPE_THETA = 500000.0
ROPE_DIM = HEAD_DIM // 4
N_EXPERTS = 32
TOP_K = 4
SWIGLU_LIMIT = 7.0
SWIGLU_ALPHA = 1.702
LN_EPS = 1e-5
DEPTH = 2
DEEPNORM_ALPHA = (2 * DEPTH) ** 0.25

LANES = 128
SUBLANES = 8
VMEM_LIMIT_BYTES = 56 * 1024 * 1024
MLSTM_CHUNK = 128
MOE_ROW_BLOCK = 256
MOBA_HEAD_GROUP = 4
MOE_SUPER_ROWS = 2560
MOE_FF_TILE = 512
NEG_BIG = -1e30


def _cparams(sem):
    return pltpu.CompilerParams(dimension_semantics=sem, vmem_limit_bytes=VMEM_LIMIT_BYTES)


def _dot(a, b):
    return jnp.dot(a, b, preferred_element_type=F32)


def _dot_nt(a, b):
    return lax.dot_general(a, b, (((1,), (1,)), ((), ())), preferred_element_type=F32)


def _dot_tn(a, b):
    return lax.dot_general(a, b, (((0,), (0,)), ((), ())), preferred_element_type=F32)


def _split_bf16(x):
    hi = x.astype(BF16)
    lo = (x - hi.astype(F32)).astype(BF16)
    return hi, lo


def _layer_norm_rows(r, g, b):
    mu = jnp.mean(r, axis=-1, keepdims=True)
    d = r - mu
    var = jnp.mean(d * d, axis=-1, keepdims=True)
    return d * lax.rsqrt(var + LN_EPS) * g + b


def _pack_bf16_pairs(x):
    w = x.shape[1] // 2
    lo = pltpu.bitcast(x[:, :w].astype(BF16).astype(F32), U32)
    hi = pltpu.bitcast(x[:, w:].astype(BF16).astype(F32), U32)
    return (hi & jnp.uint32(0xFFFF0000)) | (lo >> 16)


def _unpack_bf16_pairs(p):
    lo = pltpu.bitcast(p << 16, F32).astype(BF16)
    hi = pltpu.bitcast(p & jnp.uint32(0xFFFF0000), F32).astype(BF16)
    return lo, hi


def _matmul_kernel(a_ref, b_ref, o_ref, a_bf_ref):
    @pl.when(pl.program_id(1) == 0)
    def _():
        a_bf_ref[...] = a_ref[...].astype(BF16)

    o_ref[...] = _dot(a_bf_ref[...], b_ref[...]).astype(o_ref.dtype)


def _matmul(a, b, out_dtype, tm, tn):
    m, k = a.shape
    n = b.shape[1]
    return pl.pallas_call(
        _matmul_kernel,
        grid=(m // tm, n // tn),
        in_specs=[pl.BlockSpec((tm, k), lambda i, j: (i, 0)),
                  pl.BlockSpec((k, tn), lambda i, j: (0, j))],
        out_specs=pl.BlockSpec((tm, tn), lambda i, j: (i, j)),
        out_shape=jax.ShapeDtypeStruct((m, n), out_dtype),
        scratch_shapes=[pltpu.VMEM((tm, k), BF16)],
        compiler_params=_cparams(("parallel", "arbitrary")),
        name="matmul",
    )(a, b)


def _matmul3_kernel(a_ref, b_ref, o_ref):
    a_hi, a_lo = _split_bf16(a_ref[...])
    b_hi, b_lo = _split_bf16(b_ref[...])
    o_ref[...] = _dot(a_hi, b_hi) + _dot(a_hi, b_lo) + _dot(a_lo, b_hi)


def _matmul3(a, b, tm):
    m, k = a.shape
    n = b.shape[1]
    return pl.pallas_call(
        _matmul3_kernel,
        grid=(m // tm,),
        in_specs=[pl.BlockSpec((tm, k), lambda i: (i, 0)),
                  pl.BlockSpec((k, n), lambda i: (0, 0))],
        out_specs=pl.BlockSpec((tm, n), lambda i: (i, 0)),
        out_shape=jax.ShapeDtypeStruct((m, n), F32),
        compiler_params=_cparams(("parallel",)),
        name="matmul3",
    )(a, b)


def _log_sigmoid(x):
    return jnp.minimum(x, 0.0) - jnp.log1p(jnp.exp(-jnp.abs(x)))


def _mlstm_kernel(qk_ref, v_ref, og_ref, g_ref, gt_ref, cw_ref, cb_ref, bias_ref, biast_ref,
                  out_ref, ext_ref, c_ref, n_ref, m_ref, *, chunk, nh, dqk, dv):
    L = chunk
    qkw = nh * dqk

    @pl.when(pl.program_id(1) == 0)
    def _():
        ext_ref[0:SUBLANES, :] = jnp.zeros((SUBLANES, 2 * qkw), F32)
        c_ref[...] = jnp.zeros_like(c_ref)
        n_ref[...] = jnp.zeros_like(n_ref)
        m_ref[...] = jnp.zeros_like(m_ref)

    ext_ref[SUBLANES:SUBLANES + L, :] = qk_ref[...].astype(F32)
    cw = cw_ref[...]
    y = jnp.broadcast_to(cb_ref[...], (L, 2 * qkw))
    for w in range(CONV_WIDTH):
        y = y + ext_ref[pl.ds(SUBLANES - (CONV_WIDTH - 1) + w, L), :] * cw[w:w + 1, :]
    ext_ref[0:SUBLANES, :] = ext_ref[L:L + SUBLANES, :]
    qk = y * jax.nn.sigmoid(y)

    gates = g_ref[...] + bias_ref[...]
    gates_t = gt_ref[...] + biast_ref[...]
    row = lax.broadcasted_iota(jnp.int32, (L, L), 0)
    col = lax.broadcasted_iota(jnp.int32, (L, L), 1)
    causal = col <= row

    for h in range(nh):
        q = qk[:, h * dqk:(h + 1) * dqk]
        k = qk[:, qkw + h * dqk:qkw + (h + 1) * dqk] * (dqk ** -0.5)
        v_bf = v_ref[:, h * dv:(h + 1) * dv].astype(BF16)
        q_bf = q.astype(BF16)
        ig_c = gates[:, h:h + 1]
        ig_r = gates_t[h:h + 1, :]
        lf_c = _log_sigmoid(gates[:, nh + h:nh + h + 1])
        lf_r = _log_sigmoid(gates_t[nh + h:nh + h + 1, :])
        bcum_c = jnp.sum(jnp.where(causal, lf_r, 0.0), axis=1, keepdims=True)
        bcum_r = jnp.sum(jnp.where(row <= col, lf_c, 0.0), axis=0, keepdims=True)
        g_tot = jnp.sum(lf_r, axis=1, keepdims=True)
        m_prev = m_ref[h:h + 1, 0:1]

        dmat = jnp.where(causal, bcum_c - bcum_r + ig_r, -jnp.inf)
        inter_log = bcum_c + m_prev
        m_t = jnp.maximum(inter_log, jnp.max(dmat, axis=1, keepdims=True))
        s_qk = _dot_nt(q_bf, k.astype(BF16)) * jnp.exp(dmat - m_t)
        inter_w = jnp.exp(inter_log - m_t)
        c_prev = c_ref[h]
        n_prev = n_ref[h:h + 1, :]
        num = inter_w * _dot(q_bf, c_prev.astype(BF16)) + _dot(s_qk.astype(BF16), v_bf)
        den = inter_w * jnp.sum(q * n_prev, axis=1, keepdims=True) \
            + jnp.sum(s_qk, axis=1, keepdims=True)
        hval = num / jnp.maximum(jnp.abs(den), jnp.exp(-m_t))
        o_gate = jax.nn.sigmoid(og_ref[:, h * dv:(h + 1) * dv].astype(F32))
        out_ref[:, h * dv:(h + 1) * dv] = (o_gate * hval).astype(out_ref.dtype)

        a_c = g_tot - bcum_c + ig_c
        m_new = jnp.maximum(g_tot + m_prev, jnp.max(a_c, axis=0, keepdims=True))
        decay = jnp.exp(g_tot + m_prev - m_new)
        kw = k * jnp.exp(a_c - m_new)
        c_ref[h] = decay * c_prev + _dot_tn(kw.astype(BF16), v_bf)
        n_ref[h:h + 1, :] = decay * n_prev + jnp.sum(kw, axis=0, keepdims=True)
        m_ref[h:h + 1, :] = jnp.broadcast_to(m_new, (1, LANES))


def _mlstm(z, gates, gates_t, conv_w, conv_b, bias_row, bias_col, *, nh, dqk, dv, chunk):
    bsz, s, _ = z.shape
    qkw2 = 2 * nh * dqk
    vw = nh * dv
    assert qkw2 == vw, "q|k, v and output-gate column blocks must share one block width"
    nc = s // chunk
    kern = functools.partial(_mlstm_kernel, chunk=chunk, nh=nh, dqk=dqk, dv=dv)
    return pl.pallas_call(
        kern,
        grid=(bsz, nc),
        in_specs=[
            pl.BlockSpec((None, chunk, vw), lambda b, c: (b, c, 0)),
            pl.BlockSpec((None, chunk, vw), lambda b, c: (b, c, 1)),
            pl.BlockSpec((None, chunk, vw), lambda b, c: (b, c, 2)),
            pl.BlockSpec((None, chunk, LANES), lambda b, c: (b, c, 0)),
            pl.BlockSpec((None, 2 * SUBLANES, chunk), lambda b, c: (b, 0, c)),
            pl.BlockSpec((CONV_WIDTH, qkw2), lambda b, c: (0, 0)),
            pl.BlockSpec((1, qkw2), lambda b, c: (0, 0)),
            pl.BlockSpec((1, LANES), lambda b, c: (0, 0)),
            pl.BlockSpec((2 * SUBLANES, 1), lambda b, c: (0, 0)),
        ],
        out_specs=pl.BlockSpec((None, chunk, vw), lambda b, c: (b, c, 0)),
        out_shape=jax.ShapeDtypeStruct((bsz, s, vw), BF16),
        scratch_shapes=[
            pltpu.VMEM((chunk + SUBLANES, qkw2), F32),
            pltpu.VMEM((nh, dqk, dv), F32),
            pltpu.VMEM((SUBLANES, dqk), F32),
            pltpu.VMEM((SUBLANES, LANES), F32),
        ],
        compiler_params=_cparams(("parallel", "arbitrary")),
        name="mlstm",
    )(z, z, z, gates, gates_t, conv_w, conv_b, bias_row, bias_col)


def _mem_attn_kernel(q_ref, k_ref, v_ref, o_ref):
    scale = HEAD_DIM ** -0.5
    for h in range(MEM_HEADS):
        sl = slice(h * HEAD_DIM, (h + 1) * HEAD_DIM)
        q = q_ref[:, sl].astype(BF16)
        sc = _dot_nt(q, k_ref[:, sl]) * scale
        sc = sc - jnp.max(sc, axis=1, keepdims=True)
        p = jnp.exp(sc)
        p = p / jnp.sum(p, axis=1, keepdims=True)
        o_ref[:, sl] = _dot(p.astype(BF16), v_ref[:, sl]).astype(o_ref.dtype)


def _mem_attn(z, kv, qm_block, tq):
    bsz, s, _ = z.shape
    n_mem = kv.shape[1]
    return pl.pallas_call(
        _mem_attn_kernel,
        grid=(bsz, s // tq),
        in_specs=[
            pl.BlockSpec((None, tq, MEM_WIDTH), lambda b, i: (b, i, qm_block)),
            pl.BlockSpec((None, n_mem, MEM_WIDTH), lambda b, i: (b, 0, 0)),
            pl.BlockSpec((None, n_mem, MEM_WIDTH), lambda b, i: (b, 0, 1)),
        ],
        out_specs=pl.BlockSpec((None, tq, MEM_WIDTH), lambda b, i: (b, i, 0)),
        out_shape=jax.ShapeDtypeStruct((bsz, s, MEM_WIDTH), BF16),
        compiler_params=_cparams(("parallel", "parallel")),
        name="mem_attn",
    )(z, kv, kv)


def _moba_prep_kernel(q_ref, k_ref, cos_ref, s1_ref, s2_ref, qo_ref, ko_ref, kbar_ref, *, nh):
    cosf = cos_ref[...]
    s1 = s1_ref[...]
    s2 = s2_ref[...]
    half = ROPE_DIM // 2
    for h in range(nh):
        sl = slice(h * HEAD_DIM, (h + 1) * HEAD_DIM)
        for src, dst, is_k in ((q_ref, qo_ref, False), (k_ref, ko_ref, True)):
            x = src[:, sl].astype(F32)
            xr = (x * cosf + pltpu.roll(x, HEAD_DIM - half, 1) * s1
                  + pltpu.roll(x, half, 1) * s2)
            if is_k:
                dst[:, sl] = xr.astype(dst.dtype)
                kbar_ref[:, sl] = jnp.mean(xr, axis=0, keepdims=True)
            else:
                dst[:, sl] = (xr * (HEAD_DIM ** -0.5)).astype(dst.dtype)


def _moba_prep(z, cos_t, s1_t, s2_t, nh):
    bsz, s, _ = z.shape
    w = nh * HEAD_DIM
    nb = s // MOBA_BLOCK
    t = MOBA_BLOCK
    kern = functools.partial(_moba_prep_kernel, nh=nh)
    tab = pl.BlockSpec((t, HEAD_DIM), lambda b, i: (i, 0))
    big = jax.ShapeDtypeStruct((bsz, s, w), BF16)
    return pl.pallas_call(
        kern,
        grid=(bsz, nb),
        in_specs=[pl.BlockSpec((None, t, w), lambda b, i: (b, i, 0)),
                  pl.BlockSpec((None, t, w), lambda b, i: (b, i, 1)),
                  tab, tab, tab],
        out_specs=[pl.BlockSpec((None, t, w), lambda b, i: (b, i, 0)),
                   pl.BlockSpec((None, t, w), lambda b, i: (b, i, 0)),
                   pl.BlockSpec((None, None, 1, w), lambda b, i: (b, i, 0, 0))],
        out_shape=[big, big, jax.ShapeDtypeStruct((bsz, nb, 1, w), F32)],
        compiler_params=_cparams(("parallel", "parallel")),
        name="moba_prep",
    )(z, z, cos_t, s1_t, s2_t)


def _moba_attn_kernel(q_ref, k_ref, v_ref, kbar_ref, o_ref, *, nb, hg):
    t = MOBA_BLOCK
    qb = pl.program_id(2)
    blk = lax.broadcasted_iota(jnp.int32, (nb, t), 0)
    valid = blk < qb
    pad_rows = 2 * SUBLANES - nb
    eye = jnp.where(lax.broadcasted_iota(jnp.int32, (nb + pad_rows, LANES), 0)
                    == lax.broadcasted_iota(jnp.int32, (nb + pad_rows, LANES), 1),
                    1.0, 0.0).astype(BF16)

    qs = []
    sel_biases = []
    for hh in range(hg):
        sl = slice(hh * HEAD_DIM, (hh + 1) * HEAD_DIM)
        q = q_ref[:, sl]
        kb_hi, kb_lo = _split_bf16(kbar_ref[:, sl])
        gate = _dot_nt(kb_hi, q) + _dot_nt(kb_lo, q)
        gm = jnp.where(valid, gate, -jnp.inf)
        rank = jnp.zeros((nb, t), jnp.int32)
        for n2 in range(nb):
            gc = gm[n2:n2 + 1, :]
            beats = (gc > gm) | ((gc == gm) & (n2 < blk))
            rank = rank + beats.astype(jnp.int32)
        bias_t = jnp.where(valid & (rank < MOBA_TOPK), 0.0, NEG_BIG)
        bias_t = jnp.concatenate([bias_t, jnp.zeros((pad_rows, t), F32)], axis=0).astype(BF16)
        sel_biases.append(_dot_tn(bias_t, eye))
        qs.append(q)

    row = lax.broadcasted_iota(jnp.int32, (t, t), 0)
    col = lax.broadcasted_iota(jnp.int32, (t, t), 1)
    causal = col <= row

    for c in range(nb):
        @pl.when(qb == c)
        def _(c=c):
            for hh in range(hg):
                sl = slice(hh * HEAD_DIM, (hh + 1) * HEAD_DIM)
                pieces = []
                for n in range(c + 1):
                    s = _dot_nt(qs[hh], k_ref[n * t:(n + 1) * t, sl])
                    if n == c:
                        pieces.append(jnp.where(causal, s, NEG_BIG))
                    else:
                        pieces.append(s + sel_biases[hh][:, n:n + 1])
                mx = pieces[0]
                for piece in pieces[1:]:
                    mx = jnp.maximum(mx, piece)
                m = jnp.max(mx, axis=1, keepdims=True)
                psum = None
                acc = None
                for n in range(c + 1):
                    p = jnp.exp(pieces[n] - m)
                    pv = _dot(p.astype(BF16), v_ref[n * t:(n + 1) * t, sl])
                    psum = p if psum is None else psum + p
                    acc = pv if acc is None else acc + pv
                l = jnp.sum(psum, axis=1, keepdims=True)
                o_ref[:, sl] = (acc / l).astype(o_ref.dtype)


def _moba_attn(q, k, z, v_col, kbar, nh, hg):
    bsz, s, _ = q.shape
    nb = s // MOBA_BLOCK
    t = MOBA_BLOCK
    w = hg * HEAD_DIM
    assert v_col % w == 0
    v_blk = v_col // w
    kern = functools.partial(_moba_attn_kernel, nb=nb, hg=hg)
    return pl.pallas_call(
        kern,
        grid=(bsz, nh // hg, nb),
        in_specs=[pl.BlockSpec((None, t, w), lambda b, h, i: (b, i, h)),
                  pl.BlockSpec((None, s, w), lambda b, h, i: (b, 0, h)),
                  pl.BlockSpec((None, s, w), lambda b, h, i: (b, 0, v_blk + h)),
                  pl.BlockSpec((None, nb, w), lambda b, h, i: (b, 0, h))],
        out_specs=pl.BlockSpec((None, t, w), lambda b, h, i: (b, i, h)),
        out_shape=jax.ShapeDtypeStruct((bsz, s, nh * HEAD_DIM), BF16),
        compiler_params=_cparams(("parallel", "parallel", "arbitrary")),
        name="moba_attn",
    )(q, k, z, kbar)


def _outproj_kernel(hmix_ref, hmem_ref, w1_ref, w2_ref, x_ref, g_ref, b_ref, wr_ref, br_ref,
                    x1_ref, x1p_ref, tope_ref, gate_ref, rank_ref, cnt_ref, tri_ref, run_ref):
    i = pl.program_id(0)
    y = _dot(hmix_ref[...], w1_ref[...]) + _dot(hmem_ref[...], w2_ref[...])
    x1 = _layer_norm_rows(DEEPNORM_ALPHA * x_ref[...] + y, g_ref[...], b_ref[...])
    x1_ref[...] = x1
    packed = _pack_bf16_pairs(x1)
    for jj in range(packed.shape[1] // LANES):
        x1p_ref[pl.ds(jj, packed.shape[0], stride=SUBLANES), :] = packed[:, jj * LANES:(jj + 1) * LANES]

    x_hi, x_lo = _split_bf16(x1)
    w_hi, w_lo = _split_bf16(wr_ref[...])
    logits = _dot_nt(w_hi, x_hi) + _dot_nt(w_hi, x_lo) + _dot_nt(w_lo, x_hi) + br_ref[...]
    ne, tm = logits.shape

    @pl.when(i == 0)
    def _():
        r = lax.broadcasted_iota(jnp.int32, (tm, tm), 0)
        c = lax.broadcasted_iota(jnp.int32, (tm, tm), 1)
        tri_ref[...] = jnp.where(r <= c, 1.0, 0.0).astype(BF16)
        run_ref[...] = jnp.zeros_like(run_ref)

    eid = lax.broadcasted_iota(jnp.int32, (ne, tm), 0)
    run = run_ref[...]
    vals = []
    for k in range(TOP_K):
        mx = jnp.max(logits, axis=0, keepdims=True)
        idx = jnp.min(jnp.where(logits == mx, eid, ne), axis=0, keepdims=True)
        tope_ref[k:k + 1, :] = idx
        vals.append(mx)
        hit = eid == idx
        logits = jnp.where(hit, -jnp.inf, logits)
        incl = _dot(jnp.where(hit, 1.0, 0.0).astype(BF16), tri_ref[...])
        rank = jnp.sum(jnp.where(hit, run + incl - 1.0, 0.0), axis=0, keepdims=True)
        rank_ref[k:k + 1, :] = rank.astype(jnp.int32)
        run = run + incl[:, tm - 1:tm]
    run_ref[...] = run
    cnt_ref[...] = jnp.broadcast_to(run, cnt_ref.shape)
    ex = [jnp.exp(vk - vals[0]) for vk in vals]
    tot = ex[0] + ex[1] + ex[2] + ex[3]
    for k in range(TOP_K):
        gate_ref[k:k + 1, :] = ex[k] / tot


def _outproj(hmix, hmem, w1, w2, x, g, b, wr_t, br_col, tm):
    m, d = x.shape
    ne = wr_t.shape[0]
    return pl.pallas_call(
        _outproj_kernel,
        grid=(m // tm,),
        in_specs=[pl.BlockSpec((tm, hmix.shape[1]), lambda i: (i, 0)),
                  pl.BlockSpec((tm, hmem.shape[1]), lambda i: (i, 0)),
                  pl.BlockSpec(w1.shape, lambda i: (0, 0)),
                  pl.BlockSpec(w2.shape, lambda i: (0, 0)),
                  pl.BlockSpec((tm, d), lambda i: (i, 0)),
                  pl.BlockSpec((1, d), lambda i: (0, 0)),
                  pl.BlockSpec((1, d), lambda i: (0, 0)),
                  pl.BlockSpec(wr_t.shape, lambda i: (0, 0)),
                  pl.BlockSpec(br_col.shape, lambda i: (0, 0))],
        out_specs=[pl.BlockSpec((tm, d), lambda i: (i, 0)),
                   pl.BlockSpec((tm * SUBLANES, LANES), lambda i: (i, 0)),
                   pl.BlockSpec((TOP_K, tm), lambda i: (0, i)),
                   pl.BlockSpec((TOP_K, tm), lambda i: (0, i)),
                   pl.BlockSpec((TOP_K, tm), lambda i: (0, i)),
                   pl.BlockSpec((ne, LANES), lambda i: (0, 0))],
        out_shape=[jax.ShapeDtypeStruct((m, d), F32),
                   jax.ShapeDtypeStruct((m * SUBLANES, LANES), U32),
                   jax.ShapeDtypeStruct((TOP_K, m), jnp.int32),
                   jax.ShapeDtypeStruct((TOP_K, m), F32),
                   jax.ShapeDtypeStruct((TOP_K, m), jnp.int32),
                   jax.ShapeDtypeStruct((ne, LANES), F32)],
        scratch_shapes=[pltpu.VMEM((tm, tm), BF16), pltpu.VMEM((ne, 1), F32)],
        compiler_params=_cparams(("arbitrary",)),
        name="outproj_ln_router",
    )(hmix, hmem, w1, w2, x, g, b, wr_t, br_col)


INVERT_UNROLL = 16


def _invert_kernel(trips_ref, dest_ref, tok_ref, *, n):
    def clear(i, carry):
        for u in range(INVERT_UNROLL):
            tok_ref[i * INVERT_UNROLL + u] = 0
        return carry

    lax.fori_loop(0, trips_ref[0], clear, 0)
    for k in range(TOP_K):
        def put(i, carry, k=k):
            for u in range(INVERT_UNROLL):
                t = i * INVERT_UNROLL + u
                tok_ref[dest_ref[k * n + t]] = t
            return carry

        lax.fori_loop(0, trips_ref[1], put, 0)


def _invert(dest_flat, rows, n):
    assert rows % INVERT_UNROLL == 0 and n % INVERT_UNROLL == 0
    trips = jnp.array([rows // INVERT_UNROLL, n // INVERT_UNROLL], I32)
    return pl.pallas_call(
        functools.partial(_invert_kernel, n=n),
        in_specs=[pl.BlockSpec(memory_space=pltpu.SMEM), pl.BlockSpec(memory_space=pltpu.SMEM)],
        out_specs=pl.BlockSpec(memory_space=pltpu.SMEM),
        out_shape=jax.ShapeDtypeStruct((rows,), I32),
        name="moe_invert",
    )(trips, dest_flat)


ROW_DMA_UNROLL = 8


def _moe_ffn_kernel(sbe_ref, sbrow_ref, sbnsub_ref, meta_ref, tok_ref,
                    x_hbm, wgu_hbm, wdn_hbm, bg_ref, bu_ref, bd_ref, ys_hbm,
                    xraw, acc, stg_g, stg_u, stg_d, wgu_bf, wd_bf, hid_ref,
                    in_sem, w_sem, out_sem, *, nblk, layer):
    s = pl.program_id(0)
    f = pl.program_id(1)
    ns = pl.num_programs(0)
    nf = pl.num_programs(1)
    r = MOE_ROW_BLOCK
    tf = wd_bf.shape[0]
    dff = nf * tf
    nxl = wgu_bf.shape[0] // (2 * LANES)
    nsub = sbnsub_ref[s]
    nsub_prev = jnp.where(s > 0, sbnsub_ref[jnp.maximum(s - 1, 0)], 0)

    def row_copy(src_row, dst_row):
        return pltpu.make_async_copy(
            x_hbm.at[pl.ds(pl.multiple_of(src_row * nxl, nxl), nxl), :],
            xraw.at[pl.ds(pl.multiple_of(dst_row * nxl, nxl), nxl), :], in_sem)

    def out_copy(ss, sub):
        row0 = pl.multiple_of(sub * r, r)
        dst = pl.multiple_of(sbrow_ref[ss] + sub * r, r)
        return pltpu.make_async_copy(acc.at[pl.ds(row0, r), :], ys_hbm.at[pl.ds(dst, r), :], out_sem)

    def weight_copies(ss, ff):
        e = sbe_ref[ss]
        col = pl.multiple_of(ff * tf, tf)
        return (
            pltpu.make_async_copy(wgu_hbm.at[layer, e, :, pl.ds(col, tf)], stg_g, w_sem),
            pltpu.make_async_copy(wgu_hbm.at[layer, e, :, pl.ds(pl.multiple_of(dff + col, tf), tf)],
                                  stg_u, w_sem),
            pltpu.make_async_copy(wdn_hbm.at[layer, e, pl.ds(col, tf), :], stg_d, w_sem))

    def await_outputs(ss, count):
        def flush(sub, carry):
            out_copy(ss, sub).wait()
            return carry
        lax.fori_loop(0, count, flush, 0)

    @pl.when(f == 0)
    def _():
        await_outputs(jnp.maximum(s - 1, 0), nsub_prev)

    @pl.when(nsub > 0)
    def _():
        @pl.when((s == 0) & (f == 0))
        def _():
            for cp in weight_copies(0, 0):
                cp.start()

        @pl.when(f == 0)
        def _():
            base = sbrow_ref[s]
            trips = nsub * (r // ROW_DMA_UNROLL)

            def request(i, carry):
                for u in range(ROW_DMA_UNROLL):
                    j = i * ROW_DMA_UNROLL + u
                    row_copy(tok_ref[base + j], j).start()
                return carry
            lax.fori_loop(0, trips, request, 0)

            def init(sub, carry):
                acc[pl.ds(pl.multiple_of(sub * r, r), r), :] = jnp.broadcast_to(
                    bd_ref[...], (r, acc.shape[1]))
                return carry
            lax.fori_loop(0, nsub, init, 0)

            def land(i, carry):
                for u in range(ROW_DMA_UNROLL):
                    row_copy(0, i * ROW_DMA_UNROLL + u).wait()
                return carry
            lax.fori_loop(0, trips, land, 0)

        for cp in weight_copies(s, f):
            cp.wait()
        wgu_bf[:, 0:tf] = stg_g[...].astype(BF16)
        wgu_bf[:, tf:] = stg_u[...].astype(BF16)
        wd_bf[...] = stg_d[...].astype(BF16)

        def hidden(sub):
            row0 = sub * r if isinstance(sub, int) else pl.multiple_of(sub * r, r)
            xu = jnp.concatenate([xraw[pl.ds(row0 * nxl + jj, r, stride=nxl), :]
                                  for jj in range(nxl)], axis=1)
            x = jnp.concatenate(_unpack_bf16_pairs(xu), axis=1)
            h = _dot(x, wgu_bf[...])
            hg = h[:, 0:tf] + bg_ref[...]
            hu = h[:, tf:] + bu_ref[...]
            hg = jnp.minimum(hg, SWIGLU_LIMIT)
            hu = jnp.clip(hu, -SWIGLU_LIMIT, SWIGLU_LIMIT)
            hid = (hu + 1.0) * (hg * jax.nn.sigmoid(SWIGLU_ALPHA * hg))
            hid_ref[sub % 2] = hid.astype(BF16)

        def accumulate(sub, part):
            acc[pl.ds(pl.multiple_of(sub * r, r), r), :] += part

            @pl.when(f == nf - 1)
            def _():
                out_copy(s, sub).start()

        hidden(0)

        last_f = f == nf - 1
        s_next = jnp.where(last_f, jnp.minimum(s + 1, ns - 1), s)
        f_next = jnp.where(last_f, 0, f + 1)

        @pl.when(jnp.logical_not(last_f) | (s + 1 < meta_ref[0]))
        def _():
            for cp in weight_copies(s_next, f_next):
                cp.start()

        def step(i, carry):
            part = _dot(hid_ref[i % 2], wd_bf[...])
            hidden(i + 1)
            accumulate(i, part)
            return carry

        lax.fori_loop(0, nsub - 1, step, 0)
        last = nsub - 1
        accumulate(last, _dot(hid_ref[last % 2], wd_bf[...]))

    @pl.when((s == ns - 1) & (f == nf - 1))
    def _():
        await_outputs(s, nsub)
        acc[0:r, :] = jnp.zeros((r, acc.shape[1]), F32)

        def tail_copy(blk):
            return pltpu.make_async_copy(acc.at[pl.ds(0, r), :],
                                         ys_hbm.at[pl.ds(pl.multiple_of(blk * r, r), r), :], out_sem)

        def go(blk, carry):
            tail_copy(blk).start()
            return carry
        lax.fori_loop(meta_ref[1], nblk, go, 0)

        def done(blk, carry):
            tail_copy(blk).wait()
            return carry
        lax.fori_loop(meta_ref[1], nblk, done, 0)


def _moe_ffn(xp, n_tok, row_tok, w_gu, b_gu, w_down, b_down, sbe, sbrow, sbnsub, meta, layer, nblk):
    half = xp.shape[0] * xp.shape[1] // n_tok
    assert half == SUBLANES * LANES, "a packed token must be exactly one (8, 128) tile"
    rows = row_tok.shape[0]
    d = 2 * half
    dff = w_gu.shape[-1] // 2
    tf = MOE_FF_TILE
    nf = dff // tf
    ns = sbe.shape[0]
    bgu = b_gu.reshape(b_gu.shape[0], b_gu.shape[1], 1, 2 * dff)
    bdn = b_down.reshape(b_down.shape[0], b_down.shape[1], 1, d)

    def feff(s, f, meta):
        return jnp.where(s < meta[0], f, nf - 1)

    kern = functools.partial(_moe_ffn_kernel, nblk=nblk, layer=layer)
    return pl.pallas_call(
        kern,
        grid_spec=pltpu.PrefetchScalarGridSpec(
            num_scalar_prefetch=5,
            grid=(ns, nf),
            in_specs=[
                pl.BlockSpec(memory_space=pl.ANY),
                pl.BlockSpec(memory_space=pl.ANY),
                pl.BlockSpec(memory_space=pl.ANY),
                pl.BlockSpec((None, None, 1, tf),
                             lambda s, f, se, sr, sn, mt, tk: (layer, se[s], 0, feff(s, f, mt))),
                pl.BlockSpec((None, None, 1, tf),
                             lambda s, f, se, sr, sn, mt, tk: (layer, se[s], 0, nf + feff(s, f, mt))),
                pl.BlockSpec((None, None, 1, d),
                             lambda s, f, se, sr, sn, mt, tk: (layer, se[s], 0, 0)),
            ],
            out_specs=pl.BlockSpec(memory_space=pl.ANY),
            scratch_shapes=[pltpu.VMEM((MOE_SUPER_ROWS * SUBLANES, LANES), U32),
                            pltpu.VMEM((MOE_SUPER_ROWS, d), F32),
                            pltpu.VMEM((d, tf), F32), pltpu.VMEM((d, tf), F32),
                            pltpu.VMEM((tf, d), F32),
                            pltpu.VMEM((d, 2 * tf), BF16),
                            pltpu.VMEM((tf, d), BF16),
                            pltpu.VMEM((2, MOE_ROW_BLOCK, tf), BF16),
                            pltpu.SemaphoreType.DMA(()), pltpu.SemaphoreType.DMA(()),
                            pltpu.SemaphoreType.DMA(())]),
        out_shape=jax.ShapeDtypeStruct((rows, d), F32),
        compiler_params=_cparams(("arbitrary", "arbitrary")),
        name="moe_ffn",
    )(sbe, sbrow, sbnsub, meta, row_tok, xp, w_gu, w_down, bgu, bgu, bdn)


def _combine_kernel(pos_ref, ys_hbm, x_ref, gate_ref, g_ref, b_ref, o_ref, buf, sem):
    i = pl.program_id(0)
    tb = x_ref.shape[0]
    slot = i % 2

    def row_copy(src_row, sl, k, dst_row):
        return pltpu.make_async_copy(ys_hbm.at[pl.ds(src_row, 1), :],
                                     buf.at[sl, k, pl.ds(dst_row, 1), :], sem.at[sl])

    def request(blk, sl):
        base = blk * (tb * TOP_K)

        def go(r, carry):
            for k in range(TOP_K):
                row_copy(pos_ref[base + r * TOP_K + k], sl, k, r).start()
            return carry
        lax.fori_loop(0, tb, go, 0, unroll=4)

    @pl.when(i == 0)
    def _():
        request(0, 0)

    @pl.when(i + 1 < pl.num_programs(0))
    def _():
        request(i + 1, 1 - slot)

    def land(r, carry):
        for k in range(TOP_K):
            row_copy(0, slot, k, r).wait()
        return carry

    lax.fori_loop(0, tb, land, 0, unroll=4)

    gate = gate_ref[...]
    f = gate[:, 0:1] * buf[slot, 0]
    for k in range(1, TOP_K):
        f = f + gate[:, k:k + 1] * buf[slot, k]
    o_ref[...] = _layer_norm_rows(DEEPNORM_ALPHA * x_ref[...] + f, g_ref[...], b_ref[...])


def _combine(ys, pos, gate, x, g, b, tb):
    n, d = x.shape
    return pl.pallas_call(
        _combine_kernel,
        grid_spec=pltpu.PrefetchScalarGridSpec(
            num_scalar_prefetch=1,
            grid=(n // tb,),
            in_specs=[pl.BlockSpec(memory_space=pl.ANY),
                      pl.BlockSpec((tb, d), lambda i, ps: (i, 0)),
                      pl.BlockSpec((tb, TOP_K), lambda i, ps: (i, 0)),
                      pl.BlockSpec((1, d), lambda i, ps: (0, 0)),
                      pl.BlockSpec((1, d), lambda i, ps: (0, 0))],
            out_specs=pl.BlockSpec((tb, d), lambda i, ps: (i, 0)),
            scratch_shapes=[pltpu.VMEM((2, TOP_K, tb, d), F32), pltpu.SemaphoreType.DMA((2,))]),
        out_shape=jax.ShapeDtypeStruct((n, d), F32),
        compiler_params=_cparams(("arbitrary",)),
        name="moe_combine",
    )(pos.reshape(-1), ys, x, gate, g, b)


def _dispatch_plan(top_e_t, rank_t, cnt):
    n = top_e_t.shape[1]
    r = MOE_ROW_BLOCK
    sb = MOE_SUPER_ROWS
    nblk = -(-(n * TOP_K) // r) + N_EXPERTS
    counts = cnt[:, 0].astype(I32)
    padded = (counts + r - 1) // r * r
    pend = jnp.cumsum(padded)
    pstart = pend - padded
    eids = jnp.arange(N_EXPERTS, dtype=I32)
    base = jnp.sum(jnp.where(top_e_t[:, :, None] == eids[None, None, :],
                             pstart[None, None, :], 0), axis=-1)
    dest_t = base + rank_t
    nused = pend[-1] // r
    sbb = sb // r
    nb_e = padded // r
    nsb = (nb_e + sbb - 1) // sbb
    per = (nb_e + jnp.maximum(nsb, 1) - 1) // jnp.maximum(nsb, 1)
    sb_end = jnp.cumsum(nsb)
    sb_start = sb_end - nsb
    total = sb_end[-1]
    n_sb_max = nblk // sbb + N_EXPERTS
    sid = jnp.arange(n_sb_max, dtype=I32)
    used = sid < total
    e_of = jnp.minimum(jnp.sum(sb_end[None, :] <= jnp.minimum(sid, total - 1)[:, None], axis=1),
                       N_EXPERTS - 1).astype(I32)
    idx = jnp.minimum(sid, total - 1) - sb_start[e_of]
    row0 = jnp.where(used, pstart[e_of] + idx * per[e_of] * r, 0).astype(I32)
    nsub = jnp.where(used, jnp.clip(nb_e[e_of] - idx * per[e_of], 0, per[e_of]), 0).astype(I32)
    meta = jnp.stack([total, nused]).astype(I32)
    return dest_t.astype(I32), meta, e_of, row0, nsub, nblk


def _moe_and_norm(x1, x1p, top_e_t, gate_t, rank_t, cnt, w_gu, b_gu, w_down, b_down, g, b, layer):
    dest_t, meta, sbe, sbrow, sbnsub, nblk = _dispatch_plan(top_e_t, rank_t, cnt)
    row_tok = _invert(dest_t.reshape(-1), nblk * MOE_ROW_BLOCK, dest_t.shape[1])
    ys = _moe_ffn(x1p, x1.shape[0], row_tok, w_gu, b_gu, w_down, b_down, sbe, sbrow, sbnsub, meta, layer, nblk)
    return _combine(ys, dest_t.T, gate_t.T, x1, g, b, tb=128)


def _rotary_tables(s):
    half = ROPE_DIM // 2
    inv_freq = ROPE_THETA ** (-jnp.arange(half, dtype=F32) * 2.0 / ROPE_DIM)
    ang = jnp.arange(s, dtype=F32)[:, None] * inv_freq[None, :]
    cos, sin = jnp.cos(ang), jnp.sin(ang)
    rest = HEAD_DIM - ROPE_DIM
    cos_t = jnp.concatenate([cos, cos, jnp.ones((s, rest), F32)], axis=1)
    s1_t = jnp.concatenate([-sin, jnp.zeros((s, half + rest), F32)], axis=1)
    s2_t = jnp.concatenate([jnp.zeros((s, half), F32), sin, jnp.zeros((s, rest), F32)], axis=1)
    return cos_t, s1_t, s2_t


def kernel(x, mem, mlstm_w_in, mlstm_conv_w, mlstm_conv_b, mlstm_b_igate, mlstm_b_fgate, moba_w_in, w_mem_kv, w_out, ln1_g, ln1_b, w_router, b_router, w_gu, b_gu, w_down, b_down, ln2_g, ln2_b):
    bsz, s, d = x.shape
    n = bsz * s
    n_mem = mem.shape[1]
    mix_w = d - MEM_WIDTH
    nh_ml = MLSTM_HEADS
    dv = mix_w // nh_ml
    dqk = dv // 2
    qkw2 = 2 * nh_ml * dqk
    nh_mb = mix_w // HEAD_DIM
    qm_block = 3 * mix_w // MEM_WIDTH
    assert qkw2 == mix_w and 3 * mix_w % MEM_WIDTH == 0

    xf = x.reshape(n, d)
    memf = mem.reshape(bsz * n_mem, d)
    for i in range(DEPTH):
        j = i // 2
        if i % 2 == 0:
            w_in = mlstm_w_in[j]
            gate_lo = qkw2 + 2 * mix_w
            gate_hi = gate_lo + 2 * nh_ml
            w_main = jnp.concatenate([w_in[:, :gate_lo], w_in[:, gate_hi:]], axis=1).astype(BF16)
            w_gate = jnp.pad(w_in[:, gate_lo:gate_hi], ((0, 0), (0, LANES - 2 * nh_ml)))
            z = _matmul(xf, w_main, BF16, tm=1024, tn=512).reshape(bsz, s, -1)
            gates = _matmul3(xf, w_gate, tm=1024).reshape(bsz, s, LANES)
            gates_t = jnp.swapaxes(gates[:, :, :2 * SUBLANES], 1, 2)
            bias = jnp.concatenate([mlstm_b_igate[j], mlstm_b_fgate[j]])
            bias_row = jnp.pad(bias, (0, LANES - 2 * nh_ml)).reshape(1, LANES)
            bias_col = jnp.pad(bias, (0, 2 * SUBLANES - 2 * nh_ml)).reshape(2 * SUBLANES, 1)
            h_mix = _mlstm(z, gates, gates_t, mlstm_conv_w[j], mlstm_conv_b[j].reshape(1, -1),
                           bias_row, bias_col, nh=nh_ml, dqk=dqk, dv=dv, chunk=MLSTM_CHUNK)
        else:
            z = _matmul(xf, moba_w_in[j].astype(BF16), BF16, tm=1024, tn=512).reshape(bsz, s, -1)
            cos_t, s1_t, s2_t = _rotary_tables(s)
            q_r, k_r, kbar = _moba_prep(z, cos_t, s1_t, s2_t, nh_mb)
            h_mix = _moba_attn(q_r, k_r, z, 2 * mix_w, kbar.reshape(bsz, s // MOBA_BLOCK, mix_w),
                               nh_mb, hg=MOBA_HEAD_GROUP)
        kv = _matmul(memf, w_mem_kv[i].astype(BF16), BF16, tm=1024, tn=512)
        h_mem = _mem_attn(z, kv.reshape(bsz, n_mem, 2 * MEM_WIDTH), qm_block, tq=512)
        w_o = w_out[i].astype(BF16)
        x1, x1p, top_e_t, gate_t, rank_t, cnt = _outproj(
            h_mix.reshape(n, mix_w), h_mem.reshape(n, MEM_WIDTH), w_o[:mix_w], w_o[mix_w:], xf,
            ln1_g[i].reshape(1, d), ln1_b[i].reshape(1, d),
            w_router[i].T, b_router[i].reshape(N_EXPERTS, 1), tm=512)
        xf = _moe_and_norm(x1, x1p, top_e_t, gate_t, rank_t, cnt, w_gu, b_gu, w_down, b_down,
                           ln2_g[i].reshape(1, d), ln2_b[i].reshape(1, d), i)
    return xf.reshape(bsz, s, d)
```

```python
import functools

import jax
import jax.numpy as jnp
from jax import lax
from jax.experimental import pallas as pl
from jax.experimental.pallas import tpu as pltpu

F32 = jnp.float32
BF16 = jnp.bfloat16
U32 = jnp.uint32
I32 = jnp.int32

HEAD_DIM = 128
MEM_HEADS = 4
MEM_WIDTH = MEM_HEADS * HEAD_DIM
MLSTM_HEADS = 6
CONV_WIDTH = 4
MOBA_BLOCK = 256
MOBA_TOPK = 3
ROPE_THETA = 500000.0
ROPE_DIM = HEAD_DIM // 4
N_EXPERTS = 32
TOP_K = 4
SWIGLU_LIMIT = 7.0
SWIGLU_ALPHA = 1.702
LN_EPS = 1e-5
DEPTH = 2
DEEPNORM_ALPHA = (2 * DEPTH) ** 0.25

LANES = 128
SUBLANES = 8
VMEM_LIMIT_BYTES = 56 * 1024 * 1024
MLSTM_CHUNK = 128
MOE_ROW_BLOCK = 256
MOBA_HEAD_GROUP = 4
MOE_SUPER_ROWS = 2560
MOE_FF_TILE = 512
NEG_BIG = -1e30


def _cparams(sem):
    return pltpu.CompilerParams(dimension_semantics=sem, vmem_limit_bytes=VMEM_LIMIT_BYTES)


def _dot(a, b):
    return jnp.dot(a, b, preferred_element_type=F32)


def _dot_nt(a, b):
    return lax.dot_general(a, b, (((1,), (1,)), ((), ())), preferred_element_type=F32)


def _dot_tn(a, b):
    return lax.dot_general(a, b, (((0,), (0,)), ((), ())), preferred_element_type=F32)


def _split_bf16(x):
    hi = x.astype(BF16)
    lo = (x - hi.astype(F32)).astype(BF16)
    return hi, lo


def _layer_norm_rows(r, g, b):
    mu = jnp.mean(r, axis=-1, keepdims=True)
    d = r - mu
    var = jnp.mean(d * d, axis=-1, keepdims=True)
    return d * lax.rsqrt(var + LN_EPS) * g + b


def _pack_bf16_pairs(x):
    w = x.shape[1] // 2
    lo = pltpu.bitcast(x[:, :w].astype(BF16).astype(F32), U32)
    hi = pltpu.bitcast(x[:, w:].astype(BF16).astype(F32), U32)
    return (hi & jnp.uint32(0xFFFF0000)) | (lo >> 16)


def _unpack_bf16_pairs(p):
    lo = pltpu.bitcast(p << 16, F32).astype(BF16)
    hi = pltpu.bitcast(p & jnp.uint32(0xFFFF0000), F32).astype(BF16)
    return lo, hi


def _matmul_kernel(a_ref, b_ref, o_ref, a_bf_ref):
    @pl.when(pl.program_id(1) == 0)
    def _():
        a_bf_ref[...] = a_ref[...].astype(BF16)

    o_ref[...] = _dot(a_bf_ref[...], b_ref[...]).astype(o_ref.dtype)


def _matmul(a, b, out_dtype, tm, tn):
    m, k = a.shape
    n = b.shape[1]
    return pl.pallas_call(
        _matmul_kernel,
        grid=(m // tm, n // tn),
        in_specs=[pl.BlockSpec((tm, k), lambda i, j: (i, 0)),
                  pl.BlockSpec((k, tn), lambda i, j: (0, j))],
        out_specs=pl.BlockSpec((tm, tn), lambda i, j: (i, j)),
        out_shape=jax.ShapeDtypeStruct((m, n), out_dtype),
        scratch_shapes=[pltpu.VMEM((tm, k), BF16)],
        compiler_params=_cparams(("parallel", "arbitrary")),
        name="matmul",
    )(a, b)


def _matmul3_kernel(a_ref, b_ref, o_ref):
    a_hi, a_lo = _split_bf16(a_ref[...])
    b_hi, b_lo = _split_bf16(b_ref[...])
    o_ref[...] = _dot(a_hi, b_hi) + _dot(a_hi, b_lo) + _dot(a_lo, b_hi)


def _matmul3(a, b, tm):
    m, k = a.shape
    n = b.shape[1]
    return pl.pallas_call(
        _matmul3_kernel,
        grid=(m // tm,),
        in_specs=[pl.BlockSpec((tm, k), lambda i: (i, 0)),
                  pl.BlockSpec((k, n), lambda i: (0, 0))],
        out_specs=pl.BlockSpec((tm, n), lambda i: (i, 0)),
        out_shape=jax.ShapeDtypeStruct((m, n), F32),
        compiler_params=_cparams(("parallel",)),
        name="matmul3",
    )(a, b)


def _log_sigmoid(x):
    return jnp.minimum(x, 0.0) - jnp.log1p(jnp.exp(-jnp.abs(x)))


def _mlstm_kernel(qk_ref, v_ref, og_ref, g_ref, gt_ref, cw_ref, cb_ref, bias_ref, biast_ref,
                  out_ref, ext_ref, c_ref, n_ref, m_ref, *, chunk, nh, dqk, dv):
    L = chunk
    qkw = nh * dqk

    @pl.when(pl.program_id(1) == 0)
    def _():
        ext_ref[0:SUBLANES, :] = jnp.zeros((SUBLANES, 2 * qkw), F32)
        c_ref[...] = jnp.zeros_like(c_ref)
        n_ref[...] = jnp.zeros_like(n_ref)
        m_ref[...] = jnp.zeros_like(m_ref)

    ext_ref[SUBLANES:SUBLANES + L, :] = qk_ref[...].astype(F32)
    cw = cw_ref[...]
    y = jnp.broadcast_to(cb_ref[...], (L, 2 * qkw))
    for w in range(CONV_WIDTH):
        y = y + ext_ref[pl.ds(SUBLANES - (CONV_WIDTH - 1) + w, L), :] * cw[w:w + 1, :]
    ext_ref[0:SUBLANES, :] = ext_ref[L:L + SUBLANES, :]
    qk = y * jax.nn.sigmoid(y)

    gates = g_ref[...] + bias_ref[...]
    gates_t = gt_ref[...] + biast_ref[...]
    row = lax.broadcasted_iota(jnp.int32, (L, L), 0)
    col = lax.broadcasted_iota(jnp.int32, (L, L), 1)
    causal = col <= row

    for h in range(nh):
        q = qk[:, h * dqk:(h + 1) * dqk]
        k = qk[:, qkw + h * dqk:qkw + (h + 1) * dqk] * (dqk ** -0.5)
        v_bf = v_ref[:, h * dv:(h + 1) * dv].astype(BF16)
        q_bf = q.astype(BF16)
        ig_c = gates[:, h:h + 1]
        ig_r = gates_t[h:h + 1, :]
        lf_c = _log_sigmoid(gates[:, nh + h:nh + h + 1])
        lf_r = _log_sigmoid(gates_t[nh + h:nh + h + 1, :])
        bcum_c = jnp.sum(jnp.where(causal, lf_r, 0.0), axis=1, keepdims=True)
        bcum_r = jnp.sum(jnp.where(row <= col, lf_c, 0.0), axis=0, keepdims=True)
        g_tot = jnp.sum(lf_r, axis=1, keepdims=True)
        m_prev = m_ref[h:h + 1, 0:1]

        dmat = jnp.where(causal, bcum_c - bcum_r + ig_r, -jnp.inf)
        inter_log = bcum_c + m_prev
        m_t = jnp.maximum(inter_log, jnp.max(dmat, axis=1, keepdims=True))
        s_qk = _dot_nt(q_bf, k.astype(BF16)) * jnp.exp(dmat - m_t)
        inter_w = jnp.exp(inter_log - m_t)
        c_prev = c_ref[h]
        n_prev = n_ref[h:h + 1, :]
        num = inter_w * _dot(q_bf, c_prev.astype(BF16)) + _dot(s_qk.astype(BF16), v_bf)
        den = inter_w * jnp.sum(q * n_prev, axis=1, keepdims=True) \
            + jnp.sum(s_qk, axis=1, keepdims=True)
        hval = num / jnp.maximum(jnp.abs(den), jnp.exp(-m_t))
        o_gate = jax.nn.sigmoid(og_ref[:, h * dv:(h + 1) * dv].astype(F32))
        out_ref[:, h * dv:(h + 1) * dv] = (o_gate * hval).astype(out_ref.dtype)

        a_c = g_tot - bcum_c + ig_c
        m_new = jnp.maximum(g_tot + m_prev, jnp.max(a_c, axis=0, keepdims=True))
        decay = jnp.exp(g_tot + m_prev - m_new)
        kw = k * jnp.exp(a_c - m_new)
        c_ref[h] = decay * c_prev + _dot_tn(kw.astype(BF16), v_bf)
        n_ref[h:h + 1, :] = decay * n_prev + jnp.sum(kw, axis=0, keepdims=True)
        m_ref[h:h + 1, :] = jnp.broadcast_to(m_new, (1, LANES))


def _mlstm(z, gates, gates_t, conv_w, conv_b, bias_row, bias_col, *, nh, dqk, dv, chunk):
    bsz, s, _ = z.shape
    qkw2 = 2 * nh * dqk
    vw = nh * dv
    assert qkw2 == vw, "q|k, v and output-gate column blocks must share one block width"
    nc = s // chunk
    kern = functools.partial(_mlstm_kernel, chunk=chunk, nh=nh, dqk=dqk, dv=dv)
    return pl.pallas_call(
        kern,
        grid=(bsz, nc),
        in_specs=[
            pl.BlockSpec((None, chunk, vw), lambda b, c: (b, c, 0)),
            pl.BlockSpec((None, chunk, vw), lambda b, c: (b, c, 1)),
            pl.BlockSpec((None, chunk, vw), lambda b, c: (b, c, 2)),
            pl.BlockSpec((None, chunk, LANES), lambda b, c: (b, c, 0)),
            pl.BlockSpec((None, 2 * SUBLANES, chunk), lambda b, c: (b, 0, c)),
            pl.BlockSpec((CONV_WIDTH, qkw2), lambda b, c: (0, 0)),
            pl.BlockSpec((1, qkw2), lambda b, c: (0, 0)),
            pl.BlockSpec((1, LANES), lambda b, c: (0, 0)),
            pl.BlockSpec((2 * SUBLANES, 1), lambda b, c: (0, 0)),
        ],
        out_specs=pl.BlockSpec((None, chunk, vw), lambda b, c: (b, c, 0)),
        out_shape=jax.ShapeDtypeStruct((bsz, s, vw), BF16),
        scratch_shapes=[
            pltpu.VMEM((chunk + SUBLANES, qkw2), F32),
            pltpu.VMEM((nh, dqk, dv), F32),
            pltpu.VMEM((SUBLANES, dqk), F32),
            pltpu.VMEM((SUBLANES, LANES), F32),
        ],
        compiler_params=_cparams(("parallel", "arbitrary")),
        name="mlstm",
    )(z, z, z, gates, gates_t, conv_w, conv_b, bias_row, bias_col)


def _mem_attn_kernel(q_ref, k_ref, v_ref, o_ref):
    scale = HEAD_DIM ** -0.5
    for h in range(MEM_HEADS):
        sl = slice(h * HEAD_DIM, (h + 1) * HEAD_DIM)
        q = q_ref[:, sl].astype(BF16)
        sc = _dot_nt(q, k_ref[:, sl]) * scale
        sc = sc - jnp.max(sc, axis=1, keepdims=True)
        p = jnp.exp(sc)
        p = p / jnp.sum(p, axis=1, keepdims=True)
        o_ref[:, sl] = _dot(p.astype(BF16), v_ref[:, sl]).astype(o_ref.dtype)


def _mem_attn(z, kv, qm_block, tq):
    bsz, s, _ = z.shape
    n_mem = kv.shape[1]
    return pl.pallas_call(
        _mem_attn_kernel,
        grid=(bsz, s // tq),
        in_specs=[
            pl.BlockSpec((None, tq, MEM_WIDTH), lambda b, i: (b, i, qm_block)),
            pl.BlockSpec((None, n_mem, MEM_WIDTH), lambda b, i: (b, 0, 0)),
            pl.BlockSpec((None, n_mem, MEM_WIDTH), lambda b, i: (b, 0, 1)),
        ],
        out_specs=pl.BlockSpec((None, tq, MEM_WIDTH), lambda b, i: (b, i, 0)),
        out_shape=jax.ShapeDtypeStruct((bsz, s, MEM_WIDTH), BF16),
        compiler_params=_cparams(("parallel", "parallel")),
        name="mem_attn",
    )(z, kv, kv)


def _moba_prep_kernel(q_ref, k_ref, cos_ref, s1_ref, s2_ref, qo_ref, ko_ref, kbar_ref, *, nh):
    cosf = cos_ref[...]
    s1 = s1_ref[...]
    s2 = s2_ref[...]
    half = ROPE_DIM // 2
    for h in range(nh):
        sl = slice(h * HEAD_DIM, (h + 1) * HEAD_DIM)
        for src, dst, is_k in ((q_ref, qo_ref, False), (k_ref, ko_ref, True)):
            x = src[:, sl].astype(F32)
            xr = (x * cosf + pltpu.roll(x, HEAD_DIM - half, 1) * s1
                  + pltpu.roll(x, half, 1) * s2)
            if is_k:
                dst[:, sl] = xr.astype(dst.dtype)
                kbar_ref[:, sl] = jnp.mean(xr, axis=0, keepdims=True)
            else:
                dst[:, sl] = (xr * (HEAD_DIM ** -0.5)).astype(dst.dtype)


def _moba_prep(z, cos_t, s1_t, s2_t, nh):
    bsz, s, _ = z.shape
    w = nh * HEAD_DIM
    nb = s // MOBA_BLOCK
    t = MOBA_BLOCK
    kern = functools.partial(_moba_prep_kernel, nh=nh)
    tab = pl.BlockSpec((t, HEAD_DIM), lambda b, i: (i, 0))
    big = jax.ShapeDtypeStruct((bsz, s, w), BF16)
    return pl.pallas_call(
        kern,
        grid=(bsz, nb),
        in_specs=[pl.BlockSpec((None, t, w), lambda b, i: (b, i, 0)),
                  pl.BlockSpec((None, t, w), lambda b, i: (b, i, 1)),
                  tab, tab, tab],
        out_specs=[pl.BlockSpec((None, t, w), lambda b, i: (b, i, 0)),
                   pl.BlockSpec((None, t, w), lambda b, i: (b, i, 0)),
                   pl.BlockSpec((None, None, 1, w), lambda b, i: (b, i, 0, 0))],
        out_shape=[big, big, jax.ShapeDtypeStruct((bsz, nb, 1, w), F32)],
        compiler_params=_cparams(("parallel", "parallel")),
        name="moba_prep",
    )(z, z, cos_t, s1_t, s2_t)


def _moba_attn_kernel(q_ref, k_ref, v_ref, kbar_ref, o_ref, *, nb, hg):
    t = MOBA_BLOCK
    qb = pl.program_id(2)
    blk = lax.broadcasted_iota(jnp.int32, (nb, t), 0)
    valid = blk < qb
    pad_rows = 2 * SUBLANES - nb
    eye = jnp.where(lax.broadcasted_iota(jnp.int32, (nb + pad_rows, LANES), 0)
                    == lax.broadcasted_iota(jnp.int32, (nb + pad_rows, LANES), 1),
                    1.0, 0.0).astype(BF16)

    qs = []
    sel_biases = []
    for hh in range(hg):
        sl = slice(hh * HEAD_DIM, (hh + 1) * HEAD_DIM)
        q = q_ref[:, sl]
        kb_hi, kb_lo = _split_bf16(kbar_ref[:, sl])
        gate = _dot_nt(kb_hi, q) + _dot_nt(kb_lo, q)
        gm = jnp.where(valid, gate, -jnp.inf)
        rank = jnp.zeros((nb, t), jnp.int32)
        for n2 in range(nb):
            gc = gm[n2:n2 + 1, :]
            beats = (gc > gm) | ((gc == gm) & (n2 < blk))
            rank = rank + beats.astype(jnp.int32)
        bias_t = jnp.where(valid & (rank < MOBA_TOPK), 0.0, NEG_BIG)
        bias_t = jnp.concatenate([bias_t, jnp.zeros((pad_rows, t), F32)], axis=0).astype(BF16)
        sel_biases.append(_dot_tn(bias_t, eye))
        qs.append(q)

    row = lax.broadcasted_iota(jnp.int32, (t, t), 0)
    col = lax.broadcasted_iota(jnp.int32, (t, t), 1)
    causal = col <= row

    for c in range(nb):
        @pl.when(qb == c)
        def _(c=c):
            for hh in range(hg):
                sl = slice(hh * HEAD_DIM, (hh + 1) * HEAD_DIM)
                pieces = []
                for n in range(c + 1):
                    s = _dot_nt(qs[hh], k_ref[n * t:(n + 1) * t, sl])
                    if n == c:
                        pieces.append(jnp.where(causal, s, NEG_BIG))
                    else:
                        pieces.append(s + sel_biases[hh][:, n:n + 1])
                mx = pieces[0]
                for piece in pieces[1:]:
                    mx = jnp.maximum(mx, piece)
                m = jnp.max(mx, axis=1, keepdims=True)
                psum = None
                acc = None
                for n in range(c + 1):
                    p = jnp.exp(pieces[n] - m)
                    pv = _dot(p.astype(BF16), v_ref[n * t:(n + 1) * t, sl])
                    psum = p if psum is None else psum + p
                    acc = pv if acc is None else acc + pv
                l = jnp.sum(psum, axis=1, keepdims=True)
                o_ref[:, sl] = (acc / l).astype(o_ref.dtype)


def _moba_attn(q, k, z, v_col, kbar, nh, hg):
    bsz, s, _ = q.shape
    nb = s // MOBA_BLOCK
    t = MOBA_BLOCK
    w = hg * HEAD_DIM
    assert v_col % w == 0
    v_blk = v_col // w
    kern = functools.partial(_moba_attn_kernel, nb=nb, hg=hg)
    return pl.pallas_call(
        kern,
        grid=(bsz, nh // hg, nb),
        in_specs=[pl.BlockSpec((None, t, w), lambda b, h, i: (b, i, h)),
                  pl.BlockSpec((None, s, w), lambda b, h, i: (b, 0, h)),
                  pl.BlockSpec((None, s, w), lambda b, h, i: (b, 0, v_blk + h)),
                  pl.BlockSpec((None, nb, w), lambda b, h, i: (b, 0, h))],
        out_specs=pl.BlockSpec((None, t, w), lambda b, h, i: (b, i, h)),
        out_shape=jax.ShapeDtypeStruct((bsz, s, nh * HEAD_DIM), BF16),
        compiler_params=_cparams(("parallel", "parallel", "arbitrary")),
        name="moba_attn",
    )(q, k, z, kbar)


def _outproj_kernel(hmix_ref, hmem_ref, w1_ref, w2_ref, x_ref, g_ref, b_ref, wr_ref, br_ref,
                    x1_ref, x1p_ref, tope_ref, gate_ref, rank_ref, cnt_ref, tri_ref, run_ref):
    i = pl.program_id(0)
    y = _dot(hmix_ref[...], w1_ref[...]) + _dot(hmem_ref[...], w2_ref[...])
    x1 = _layer_norm_rows(DEEPNORM_ALPHA * x_ref[...] + y, g_ref[...], b_ref[...])
    x1_ref[...] = x1
    packed = _pack_bf16_pairs(x1)
    for jj in range(packed.shape[1] // LANES):
        x1p_ref[pl.ds(jj, packed.shape[0], stride=SUBLANES), :] = packed[:, jj * LANES:(jj + 1) * LANES]

    x_hi, x_lo = _split_bf16(x1)
    w_hi, w_lo = _split_bf16(wr_ref[...])
    logits = _dot_nt(w_hi, x_hi) + _dot_nt(w_hi, x_lo) + _dot_nt(w_lo, x_hi) + br_ref[...]
    ne, tm = logits.shape

    @pl.when(i == 0)
    def _():
        r = lax.broadcasted_iota(jnp.int32, (tm, tm), 0)
        c = lax.broadcasted_iota(jnp.int32, (tm, tm), 1)
        tri_ref[...] = jnp.where(r <= c, 1.0, 0.0).astype(BF16)
        run_ref[...] = jnp.zeros_like(run_ref)

    eid = lax.broadcasted_iota(jnp.int32, (ne, tm), 0)
    run = run_ref[...]
    vals = []
    for k in range(TOP_K):
        mx = jnp.max(logits, axis=0, keepdims=True)
        idx = jnp.min(jnp.where(logits == mx, eid, ne), axis=0, keepdims=True)
        tope_ref[k:k + 1, :] = idx
        vals.append(mx)
        hit = eid == idx
        logits = jnp.where(hit, -jnp.inf, logits)
        incl = _dot(jnp.where(hit, 1.0, 0.0).astype(BF16), tri_ref[...])
        rank = jnp.sum(jnp.where(hit, run + incl - 1.0, 0.0), axis=0, keepdims=True)
        rank_ref[k:k + 1, :] = rank.astype(jnp.int32)
        run = run + incl[:, tm - 1:tm]
    run_ref[...] = run
    cnt_ref[...] = jnp.broadcast_to(run, cnt_ref.shape)
    ex = [jnp.exp(vk - vals[0]) for vk in vals]
    tot = ex[0] + ex[1] + ex[2] + ex[3]
    for k in range(TOP_K):
        gate_ref[k:k + 1, :] = ex[k] / tot


def _outproj(hmix, hmem, w1, w2, x, g, b, wr_t, br_col, tm):
    m, d = x.shape
    ne = wr_t.shape[0]
    return pl.pallas_call(
        _outproj_kernel,
        grid=(m // tm,),
        in_specs=[pl.BlockSpec((tm, hmix.shape[1]), lambda i: (i, 0)),
                  pl.BlockSpec((tm, hmem.shape[1]), lambda i: (i, 0)),
                  pl.BlockSpec(w1.shape, lambda i: (0, 0)),
                  pl.BlockSpec(w2.shape, lambda i: (0, 0)),
                  pl.BlockSpec((tm, d), lambda i: (i, 0)),
                  pl.BlockSpec((1, d), lambda i: (0, 0)),
                  pl.BlockSpec((1, d), lambda i: (0, 0)),
                  pl.BlockSpec(wr_t.shape, lambda i: (0, 0)),
                  pl.BlockSpec(br_col.shape, lambda i: (0, 0))],
        out_specs=[pl.BlockSpec((tm, d), lambda i: (i, 0)),
                   pl.BlockSpec((tm * SUBLANES, LANES), lambda i: (i, 0)),
                   pl.BlockSpec((TOP_K, tm), lambda i: (0, i)),
                   pl.BlockSpec((TOP_K, tm), lambda i: (0, i)),
                   pl.BlockSpec((TOP_K, tm), lambda i: (0, i)),
                   pl.BlockSpec((ne, LANES), lambda i: (0, 0))],
        out_shape=[jax.ShapeDtypeStruct((m, d), F32),
                   jax.ShapeDtypeStruct((m * SUBLANES, LANES), U32),
                   jax.ShapeDtypeStruct((TOP_K, m), jnp.int32),
                   jax.ShapeDtypeStruct((TOP_K, m), F32),
                   jax.ShapeDtypeStruct((TOP_K, m), jnp.int32),
                   jax.ShapeDtypeStruct((ne, LANES), F32)],
        scratch_shapes=[pltpu.VMEM((tm, tm), BF16), pltpu.VMEM((ne, 1), F32)],
        compiler_params=_cparams(("arbitrary",)),
        name="outproj_ln_router",
    )(hmix, hmem, w1, w2, x, g, b, wr_t, br_col)


INVERT_UNROLL = 16


def _invert_kernel(trips_ref, zoff_ref, zpad_ref, dest_ref, tok_ref, *, n):
    def per_expert(e, carry):
        base = zoff_ref[e]

        def clear(i, c):
            tok_ref[base + i] = 0
            return c
        lax.fori_loop(0, zpad_ref[e], clear, 0)
        return carry

    lax.fori_loop(0, N_EXPERTS, per_expert, 0)

    def tail(i, carry):
        for u in range(INVERT_UNROLL):
            tok_ref[i * INVERT_UNROLL + u] = 0
        return carry

    lax.fori_loop(trips_ref[0], trips_ref[1], tail, 0)
    for k in range(TOP_K):
        def put(i, carry, k=k):
            for u in range(INVERT_UNROLL):
                t = i * INVERT_UNROLL + u
                tok_ref[dest_ref[k * n + t]] = t
            return carry

        lax.fori_loop(0, trips_ref[2], put, 0)


def _invert(dest_flat, zoff, zpad, used_rows, rows, n):
    assert rows % INVERT_UNROLL == 0 and n % INVERT_UNROLL == 0 and MOE_ROW_BLOCK % INVERT_UNROLL == 0
    trips = jnp.stack([used_rows // INVERT_UNROLL,
                       jnp.asarray(rows // INVERT_UNROLL, I32),
                       jnp.asarray(n // INVERT_UNROLL, I32)]).astype(I32)
    smem = pl.BlockSpec(memory_space=pltpu.SMEM)
    return pl.pallas_call(
        functools.partial(_invert_kernel, n=n),
        in_specs=[smem, smem, smem, smem],
        out_specs=smem,
        out_shape=jax.ShapeDtypeStruct((rows,), I32),
        name="moe_invert",
    )(trips, zoff, zpad, dest_flat)


ROW_DMA_UNROLL = 8


def _moe_ffn_kernel(sbe_ref, sbrow_ref, sbnsub_ref, meta_ref, tok_ref,
                    x_hbm, wgu_hbm, wdn_hbm, bg_ref, bu_ref, bd_ref, ys_hbm,
                    xraw, acc, stg_g, stg_u, stg_d, wgu_bf, wd_bf, hid_ref,
                    in_sem, w_sem, out_sem, *, nblk, layer):
    s = pl.program_id(0)
    f = pl.program_id(1)
    ns = pl.num_programs(0)
    nf = pl.num_programs(1)
    r = MOE_ROW_BLOCK
    tf = wd_bf.shape[0]
    dff = nf * tf
    nxl = wgu_bf.shape[0] // (2 * LANES)
    nsub = sbnsub_ref[s]
    nsub_prev = jnp.where(s > 0, sbnsub_ref[jnp.maximum(s - 1, 0)], 0)

    def row_copy(src_row, dst_row):
        return pltpu.make_async_copy(
            x_hbm.at[pl.ds(pl.multiple_of(src_row * nxl, nxl), nxl), :],
            xraw.at[pl.ds(pl.multiple_of(dst_row * nxl, nxl), nxl), :], in_sem)

    def out_copy(ss, sub):
        row0 = pl.multiple_of(sub * r, r)
        dst = pl.multiple_of(sbrow_ref[ss] + sub * r, r)
        return pltpu.make_async_copy(acc.at[pl.ds(row0, r), :], ys_hbm.at[pl.ds(dst, r), :], out_sem)

    def weight_copies(ss, ff):
        e = sbe_ref[ss]
        col = pl.multiple_of(ff * tf, tf)
        return (
            pltpu.make_async_copy(wgu_hbm.at[layer, e, :, pl.ds(col, tf)], stg_g, w_sem),
            pltpu.make_async_copy(wgu_hbm.at[layer, e, :, pl.ds(pl.multiple_of(dff + col, tf), tf)],
                                  stg_u, w_sem),
            pltpu.make_async_copy(wdn_hbm.at[layer, e, pl.ds(col, tf), :], stg_d, w_sem))

    def await_outputs(ss, count):
        def flush(sub, carry):
            out_copy(ss, sub).wait()
            return carry
        lax.fori_loop(0, count, flush, 0)

    @pl.when(f == 0)
    def _():
        await_outputs(jnp.maximum(s - 1, 0), nsub_prev)

    @pl.when(nsub > 0)
    def _():
        @pl.when((s == 0) & (f == 0))
        def _():
            for cp in weight_copies(0, 0):
                cp.start()

        @pl.when(f == 0)
        def _():
            base = sbrow_ref[s]
            trips = nsub * (r // ROW_DMA_UNROLL)

            def request(i, carry):
                for u in range(ROW_DMA_UNROLL):
                    j = i * ROW_DMA_UNROLL + u
                    row_copy(tok_ref[base + j], j).start()
                return carry
            lax.fori_loop(0, trips, request, 0)

            def init(sub, carry):
                acc[pl.ds(pl.multiple_of(sub * r, r), r), :] = jnp.broadcast_to(
                    bd_ref[...], (r, acc.shape[1]))
                return carry
            lax.fori_loop(0, nsub, init, 0)

            def land(i, carry):
                for u in range(ROW_DMA_UNROLL):
                    row_copy(0, i * ROW_DMA_UNROLL + u).wait()
                return carry
            lax.fori_loop(0, trips, land, 0)

        for cp in weight_copies(s, f):
            cp.wait()
        wgu_bf[:, 0:tf] = stg_g[...].astype(BF16)
        wgu_bf[:, tf:] = stg_u[...].astype(BF16)
        wd_bf[...] = stg_d[...].astype(BF16)

        def hidden(sub):
            row0 = sub * r if isinstance(sub, int) else pl.multiple_of(sub * r, r)
            xu = jnp.concatenate([xraw[pl.ds(row0 * nxl + jj, r, stride=nxl), :]
                                  for jj in range(nxl)], axis=1)
            x = jnp.concatenate(_unpack_bf16_pairs(xu), axis=1)
            h = _dot(x, wgu_bf[...])
            hg = h[:, 0:tf] + bg_ref[...]
            hu = h[:, tf:] + bu_ref[...]
            hg = jnp.minimum(hg, SWIGLU_LIMIT)
            hu = jnp.clip(hu, -SWIGLU_LIMIT, SWIGLU_LIMIT)
            hid = (hu + 1.0) * (hg * jax.nn.sigmoid(SWIGLU_ALPHA * hg))
            hid_ref[sub % 2] = hid.astype(BF16)

        def accumulate(sub, part):
            acc[pl.ds(pl.multiple_of(sub * r, r), r), :] += part

            @pl.when(f == nf - 1)
            def _():
                out_copy(s, sub).start()

        hidden(0)

        last_f = f == nf - 1
        s_next = jnp.where(last_f, jnp.minimum(s + 1, ns - 1), s)
        f_next = jnp.where(last_f, 0, f + 1)

        @pl.when(jnp.logical_not(last_f) | (s + 1 < meta_ref[0]))
        def _():
            for cp in weight_copies(s_next, f_next):
                cp.start()

        def step(i, carry):
            part = _dot(hid_ref[i % 2], wd_bf[...])
            hidden(i + 1)
            accumulate(i, part)
            return carry

        lax.fori_loop(0, nsub - 1, step, 0)
        last = nsub - 1
        accumulate(last, _dot(hid_ref[last % 2], wd_bf[...]))

    @pl.when((s == ns - 1) & (f == nf - 1))
    def _():
        await_outputs(s, nsub)
        acc[0:r, :] = jnp.zeros((r, acc.shape[1]), F32)

        def tail_copy(blk):
            return pltpu.make_async_copy(acc.at[pl.ds(0, r), :],
                                         ys_hbm.at[pl.ds(pl.multiple_of(blk * r, r), r), :], out_sem)

        def go(blk, carry):
            tail_copy(blk).start()
            return carry
        lax.fori_loop(meta_ref[1], nblk, go, 0)

        def done(blk, carry):
            tail_copy(blk).wait()
            return carry
        lax.fori_loop(meta_ref[1], nblk, done, 0)


def _moe_ffn(xp, n_tok, row_tok, w_gu, b_gu, w_down, b_down, sbe, sbrow, sbnsub, meta, layer, nblk):
    half = xp.shape[0] * xp.shape[1] // n_tok
    assert half == SUBLANES * LANES, "a packed token must be exactly one (8, 128) tile"
    rows = row_tok.shape[0]
    d = 2 * half
    dff = w_gu.shape[-1] // 2
    tf = MOE_FF_TILE
    nf = dff // tf
    ns = sbe.shape[0]
    bgu = b_gu.reshape(b_gu.shape[0], b_gu.shape[1], 1, 2 * dff)
    bdn = b_down.reshape(b_down.shape[0], b_down.shape[1], 1, d)

    def feff(s, f, meta):
        return jnp.where(s < meta[0], f, nf - 1)

    kern = functools.partial(_moe_ffn_kernel, nblk=nblk, layer=layer)
    return pl.pallas_call(
        kern,
        grid_spec=pltpu.PrefetchScalarGridSpec(
            num_scalar_prefetch=5,
            grid=(ns, nf),
            in_specs=[
                pl.BlockSpec(memory_space=pl.ANY),
                pl.BlockSpec(memory_space=pl.ANY),
                pl.BlockSpec(memory_space=pl.ANY),
                pl.BlockSpec((None, None, 1, tf),
                             lambda s, f, se, sr, sn, mt, tk: (layer, se[s], 0, feff(s, f, mt))),
                pl.BlockSpec((None, None, 1, tf),
                             lambda s, f, se, sr, sn, mt, tk: (layer, se[s], 0, nf + feff(s, f, mt))),
                pl.BlockSpec((None, None, 1, d),
                             lambda s, f, se, sr, sn, mt, tk: (layer, se[s], 0, 0)),
            ],
            out_specs=pl.BlockSpec(memory_space=pl.ANY),
            scratch_shapes=[pltpu.VMEM((MOE_SUPER_ROWS * SUBLANES, LANES), U32),
                            pltpu.VMEM((MOE_SUPER_ROWS, d), F32),
                            pltpu.VMEM((d, tf), F32), pltpu.VMEM((d, tf), F32),
                            pltpu.VMEM((tf, d), F32),
                            pltpu.VMEM((d, 2 * tf), BF16),
                            pltpu.VMEM((tf, d), BF16),
                            pltpu.VMEM((2, MOE_ROW_BLOCK, tf), BF16),
                            pltpu.SemaphoreType.DMA(()), pltpu.SemaphoreType.DMA(()),
                            pltpu.SemaphoreType.DMA(())]),
        out_shape=jax.ShapeDtypeStruct((rows, d), F32),
        compiler_params=_cparams(("arbitrary", "arbitrary")),
        name="moe_ffn",
    )(sbe, sbrow, sbnsub, meta, row_tok, xp, w_gu, w_down, bgu, bgu, bdn)


def _combine_kernel(pos_ref, ys_hbm, x_ref, gate_ref, g_ref, b_ref, o_ref, buf, sem):
    i = pl.program_id(0)
    tb = x_ref.shape[0]
    slot = i % 2

    def row_copy(src_row, sl, k, dst_row):
        return pltpu.make_async_copy(ys_hbm.at[pl.ds(src_row, 1), :],
                                     buf.at[sl, k, pl.ds(dst_row, 1), :], sem.at[sl])

    def request(blk, sl):
        base = blk * (tb * TOP_K)

        def go(r, carry):
            for k in range(TOP_K):
                row_copy(pos_ref[base + r * TOP_K + k], sl, k, r).start()
            return carry
        lax.fori_loop(0, tb, go, 0, unroll=4)

    @pl.when(i == 0)
    def _():
        request(0, 0)

    @pl.when(i + 1 < pl.num_programs(0))
    def _():
        request(i + 1, 1 - slot)

    def land(r, carry):
        for k in range(TOP_K):
            row_copy(0, slot, k, r).wait()
        return carry

    lax.fori_loop(0, tb, land, 0, unroll=4)

    gate = gate_ref[...]
    f = gate[:, 0:1] * buf[slot, 0]
    for k in range(1, TOP_K):
        f = f + gate[:, k:k + 1] * buf[slot, k]
    o_ref[...] = _layer_norm_rows(DEEPNORM_ALPHA * x_ref[...] + f, g_ref[...], b_ref[...])


def _combine(ys, pos, gate, x, g, b, tb):
    n, d = x.shape
    return pl.pallas_call(
        _combine_kernel,
        grid_spec=pltpu.PrefetchScalarGridSpec(
            num_scalar_prefetch=1,
            grid=(n // tb,),
            in_specs=[pl.BlockSpec(memory_space=pl.ANY),
                      pl.BlockSpec((tb, d), lambda i, ps: (i, 0)),
                      pl.BlockSpec((tb, TOP_K), lambda i, ps: (i, 0)),
                      pl.BlockSpec((1, d), lambda i, ps: (0, 0)),
                      pl.BlockSpec((1, d), lambda i, ps: (0, 0))],
            out_specs=pl.BlockSpec((tb, d), lambda i, ps: (i, 0)),
            scratch_shapes=[pltpu.VMEM((2, TOP_K, tb, d), F32), pltpu.SemaphoreType.DMA((2,))]),
        out_shape=jax.ShapeDtypeStruct((n, d), F32),
        compiler_params=_cparams(("arbitrary",)),
        name="moe_combine",
    )(pos.reshape(-1), ys, x, gate, g, b)


def _dispatch_plan(top_e_t, rank_t, cnt):
    n = top_e_t.shape[1]
    r = MOE_ROW_BLOCK
    sb = MOE_SUPER_ROWS
    nblk = -(-(n * TOP_K) // r) + N_EXPERTS
    counts = cnt[:, 0].astype(I32)
    padded = (counts + r - 1) // r * r
    pend = jnp.cumsum(padded)
    pstart = pend - padded
    eids = jnp.arange(N_EXPERTS, dtype=I32)
    base = jnp.sum(jnp.where(top_e_t[:, :, None] == eids[None, None, :],
                             pstart[None, None, :], 0), axis=-1)
    dest_t = base + rank_t
    nused = pend[-1] // r
    sbb = sb // r
    nb_e = padded // r
    nsb = (nb_e + sbb - 1) // sbb
    per = (nb_e + jnp.maximum(nsb, 1) - 1) // jnp.maximum(nsb, 1)
    sb_end = jnp.cumsum(nsb)
    sb_start = sb_end - nsb
    total = sb_end[-1]
    n_sb_max = nblk // sbb + N_EXPERTS
    sid = jnp.arange(n_sb_max, dtype=I32)
    used = sid < total
    e_of = jnp.minimum(jnp.sum(sb_end[None, :] <= jnp.minimum(sid, total - 1)[:, None], axis=1),
                       N_EXPERTS - 1).astype(I32)
    idx = jnp.minimum(sid, total - 1) - sb_start[e_of]
    row0 = jnp.where(used, pstart[e_of] + idx * per[e_of] * r, 0).astype(I32)
    nsub = jnp.where(used, jnp.clip(nb_e[e_of] - idx * per[e_of], 0, per[e_of]), 0).astype(I32)
    meta = jnp.stack([total, nused]).astype(I32)
    pads = ((pstart + counts).astype(I32), (padded - counts).astype(I32), pend[-1].astype(I32))
    return dest_t.astype(I32), meta, e_of, row0, nsub, pads, nblk


def _moe_and_norm(x1, x1p, top_e_t, gate_t, rank_t, cnt, w_gu, b_gu, w_down, b_down, g, b, layer):
    dest_t, meta, sbe, sbrow, sbnsub, pads, nblk = _dispatch_plan(top_e_t, rank_t, cnt)
    row_tok = _invert(dest_t.reshape(-1), *pads, nblk * MOE_ROW_BLOCK, dest_t.shape[1])
    ys = _moe_ffn(x1p, x1.shape[0], row_tok, w_gu, b_gu, w_down, b_down, sbe, sbrow, sbnsub, meta, layer, nblk)
    return _combine(ys, dest_t.T, gate_t.T, x1, g, b, tb=128)


def _rotary_tables(s):
    half = ROPE_DIM // 2
    inv_freq = ROPE_THETA ** (-jnp.arange(half, dtype=F32) * 2.0 / ROPE_DIM)
    ang = jnp.arange(s, dtype=F32)[:, None] * inv_freq[None, :]
    cos, sin = jnp.cos(ang), jnp.sin(ang)
    rest = HEAD_DIM - ROPE_DIM
    cos_t = jnp.concatenate([cos, cos, jnp.ones((s, rest), F32)], axis=1)
    s1_t = jnp.concatenate([-sin, jnp.zeros((s, half + rest), F32)], axis=1)
    s2_t = jnp.concatenate([jnp.zeros((s, half), F32), sin, jnp.zeros((s, rest), F32)], axis=1)
    return cos_t, s1_t, s2_t


def kernel(x, mem, mlstm_w_in, mlstm_conv_w, mlstm_conv_b, mlstm_b_igate, mlstm_b_fgate, moba_w_in, w_mem_kv, w_out, ln1_g, ln1_b, w_router, b_router, w_gu, b_gu, w_down, b_down, ln2_g, ln2_b):
    bsz, s, d = x.shape
    n = bsz * s
    n_mem = mem.shape[1]
    mix_w = d - MEM_WIDTH
    nh_ml = MLSTM_HEADS
    dv = mix_w // nh_ml
    dqk = dv // 2
    qkw2 = 2 * nh_ml * dqk
    nh_mb = mix_w // HEAD_DIM
    qm_block = 3 * mix_w // MEM_WIDTH
    assert qkw2 == mix_w and 3 * mix_w % MEM_WIDTH == 0

    xf = x.reshape(n, d)
    memf = mem.reshape(bsz * n_mem, d)
    for i in range(DEPTH):
        j = i // 2
        if i % 2 == 0:
            w_in = mlstm_w_in[j]
            gate_lo = qkw2 + 2 * mix_w
            gate_hi = gate_lo + 2 * nh_ml
            w_main = jnp.concatenate([w_in[:, :gate_lo], w_in[:, gate_hi:]], axis=1).astype(BF16)
            w_gate = jnp.pad(w_in[:, gate_lo:gate_hi], ((0, 0), (0, LANES - 2 * nh_ml)))
            z = _matmul(xf, w_main, BF16, tm=1024, tn=1024).reshape(bsz, s, -1)
            gates = _matmul3(xf, w_gate, tm=1024).reshape(bsz, s, LANES)
            gates_t = jnp.swapaxes(gates[:, :, :2 * SUBLANES], 1, 2)
            bias = jnp.concatenate([mlstm_b_igate[j], mlstm_b_fgate[j]])
            bias_row = jnp.pad(bias, (0, LANES - 2 * nh_ml)).reshape(1, LANES)
            bias_col = jnp.pad(bias, (0, 2 * SUBLANES - 2 * nh_ml)).reshape(2 * SUBLANES, 1)
            h_mix = _mlstm(z, gates, gates_t, mlstm_conv_w[j], mlstm_conv_b[j].reshape(1, -1),
                           bias_row, bias_col, nh=nh_ml, dqk=dqk, dv=dv, chunk=MLSTM_CHUNK)
        else:
            z = _matmul(xf, moba_w_in[j].astype(BF16), BF16, tm=1024, tn=1024).reshape(bsz, s, -1)
            cos_t, s1_t, s2_t = _rotary_tables(s)
            q_r, k_r, kbar = _moba_prep(z, cos_t, s1_t, s2_t, nh_mb)
            h_mix = _moba_attn(q_r, k_r, z, 2 * mix_w, kbar.reshape(bsz, s // MOBA_BLOCK, mix_w),
                               nh_mb, hg=MOBA_HEAD_GROUP)
        kv = _matmul(memf, w_mem_kv[i].astype(BF16), BF16, tm=1024, tn=512)
        h_mem = _mem_attn(z, kv.reshape(bsz, n_mem, 2 * MEM_WIDTH), qm_block, tq=512)
        w_o = w_out[i].astype(BF16)
        x1, x1p, top_e_t, gate_t, rank_t, cnt = _outproj(
            h_mix.reshape(n, mix_w), h_mem.reshape(n, MEM_WIDTH), w_o[:mix_w], w_o[mix_w:], xf,
            ln1_g[i].reshape(1, d), ln1_b[i].reshape(1, d),
            w_router[i].T, b_router[i].reshape(N_EXPERTS, 1), tm=512)
        xf = _moe_and_norm(x1, x1p, top_e_t, gate_t, rank_t, cnt, w_gu, b_gu, w_down, b_down,
                           ln2_g[i].reshape(1, d), ln2_b[i].reshape(1, d), i)
    return xf.reshape(bsz, s, d)
```
